```python
import math
import jax, jax.numpy as jnp
from jax import lax
import numpy as np

D_MODEL = 2048
BATCH = 4
SEQ = 4096
DEPTH = 2
DEC_BATCH = 32
DEC_SEQ = 32
PAST_LEN = 4096

F32 = jnp.float32
EPS = 1e-6
CHUNK = 64
Q_BLOCK = 128
HEAD_DIM = 64
ROPE_DIM = HEAD_DIM // 4
ROPE_THETA = 500000.0
A_HEADS = D_MODEL // 4 // HEAD_DIM
A_KV_HEADS = 2
A_WIDTH = A_HEADS * HEAD_DIM
IDX_HEADS = 16
IDX_DIM = 64
TOPK_MAX = 256
B_HEADS = D_MODEL // 4 // HEAD_DIM
B_WIDTH = B_HEADS * HEAD_DIM
C_D_INNER = D_MODEL // 2
C_HEAD_DIM = 64
C_HEADS = C_D_INNER // C_HEAD_DIM
C_GROUPS = 4
D_STATE = 128
CONV_WIDTH = 4
C_CONV_DIM = C_D_INNER + 2 * C_GROUPS * D_STATE
SSD_CHUNK = CHUNK
MIX_WIDTH = A_WIDTH + B_WIDTH + C_D_INNER
D_FF = -(-8 * D_MODEL // (3 * 256)) * 256
IN_SIZES = (A_WIDTH, A_KV_HEADS * HEAD_DIM, A_KV_HEADS * HEAD_DIM,
            IDX_HEADS * IDX_DIM, IDX_DIM, IDX_HEADS,
            B_WIDTH, B_WIDTH, B_WIDTH,
            C_D_INNER, C_CONV_DIM, C_HEADS)
IN_COLS = sum(IN_SIZES)

kernel_name = 'hybrid_dsa_stickbreak_ssd_stream_step'


def split_cols(p):
    outs = []
    off = 0
    for n in IN_SIZES:
        outs.append(p[..., off:off + n])
        off += n
    return outs


def rms_norm(x, g):
    xf = x.astype(F32)
    y = xf * lax.rsqrt(jnp.mean(xf * xf, axis=-1, keepdims=True) + EPS)
    return (y * g.astype(F32)).astype(x.dtype)


def rope_tables(pos):
    inv = 1.0 / (ROPE_THETA ** (jnp.arange(0, ROPE_DIM, 2, dtype=F32) / ROPE_DIM))
    ang = pos.astype(F32)[:, None] * inv[None, :]
    return jnp.cos(ang), jnp.sin(ang)


def partial_rope(x, cos, sin):
    half = ROPE_DIM // 2
    x1 = x[..., :half].astype(F32)
    x2 = x[..., half:ROPE_DIM].astype(F32)
    c = cos[None, :, None, :]
    s = sin[None, :, None, :]
    rot = jnp.concatenate([x1 * c - x2 * s, x2 * c + x1 * s], axis=-1).astype(x.dtype)
    return jnp.concatenate([rot, x[..., ROPE_DIM:]], axis=-1)


def sweep_query_blocks(block_fn, q_args, qpos):
    b, t = q_args[0].shape[:2]
    nb = t // Q_BLOCK
    blocks = tuple(jnp.moveaxis(a.reshape((b, nb, Q_BLOCK) + a.shape[2:]), 1, 0) for a in q_args)
    out = lax.map(lambda xs: block_fn(*xs), blocks + (qpos.reshape(nb, Q_BLOCK),))
    return jnp.moveaxis(out, 0, 1).reshape(b, t, out.shape[-1])


def dsa_block(q, iq, iw, qpos, k, v, ik, kpos):
    b, tq = q.shape[:2]
    topk = min(TOPK_MAX, k.shape[1] // 4)
    admissible = (kpos[None, :] // CHUNK) <= (qpos[:, None] // CHUNK)
    idx_logits = jnp.einsum('bqhd,bsd->bhqs', iq, ik, preferred_element_type=F32) * (IDX_DIM ** -0.5)
    score = jnp.einsum('bhqs,bqh->bqs', jax.nn.relu(idx_logits), iw.astype(F32))
    score = jnp.where(admissible[None], score, -jnp.inf)
    top_val, top_idx = lax.top_k(score, topk)
    valid = jnp.isfinite(top_val)
    gather = jax.vmap(lambda arr, ix: arr[ix])
    k_sel = gather(k, top_idx)
    v_sel = gather(v, top_idx)
    qg = q.reshape(b, tq, A_KV_HEADS, A_HEADS // A_KV_HEADS, HEAD_DIM)
    s = jnp.einsum('bqngd,bqknd->bqngk', qg, k_sel, preferred_element_type=F32) * (HEAD_DIM ** -0.5)
    s = jnp.where(valid[:, :, None, None, :], s, -jnp.inf)
    p = jax.nn.softmax(s, axis=-1)
    o = jnp.einsum('bqngk,bqknd->bqngd', p.astype(v.dtype), v_sel)
    return o.reshape(b, tq, A_WIDTH)


def stick_breaking_block(q, qpos, k, v, kpos):
    b, tq = q.shape[:2]
    z = jnp.einsum('bqhd,bshd->bhqs', q, k, preferred_element_type=F32) * (HEAD_DIM ** -0.5)
    causal = (kpos[None, :] < qpos[:, None])[None, None]
    log_keep = jnp.where(causal, jax.nn.log_sigmoid(-z), 0.0)
    later = lax.cumsum(log_keep, axis=3, reverse=True) - log_keep
    a = jnp.where(causal, jnp.exp(jax.nn.log_sigmoid(z) + later), 0.0)
    o = jnp.einsum('bhqs,bshd->bqhd', a.astype(v.dtype), v)
    return o.reshape(b, tq, B_WIDTH)


def causal_conv(xbc, prev, w, bias):
    xpad = jnp.concatenate([prev.astype(xbc.dtype), xbc], axis=1)
    y = lax.conv_general_dilated(xpad, w[:, None, :].astype(xbc.dtype), window_strides=(1,),
                                 padding='VALID', dimension_numbers=('NWC', 'WIO', 'NWC'),
                                 feature_group_count=xbc.shape[-1])
    return y + bias.astype(xbc.dtype), xpad[:, -(CONV_WIDTH - 1):]


def ssd_scan(x, dt, a_neg, bm, cm, init_state, chunk):
    b, l, h, p = x.shape
    g, n = bm.shape[2], bm.shape[3]
    r = h // g
    nc = l // chunk
    xd = (x * dt[..., None]).reshape(b, nc, chunk, g, r, p)
    a_cum = jnp.cumsum((dt * a_neg).reshape(b, nc, chunk, g, r), axis=2)
    bm = bm.reshape(b, nc, chunk, g, n)
    cm = cm.reshape(b, nc, chunk, g, n)
    seg = a_cum[:, :, :, None] - a_cum[:, :, None, :]
    tril = jnp.tril(jnp.ones((chunk, chunk), bool))[None, None, :, :, None, None]
    decay = jnp.where(tril, jnp.exp(jnp.where(tril, seg, 0.0)), 0.0)
    cb = jnp.einsum('bctgn,bcsgn->bctsg', cm, bm)
    y_diag = jnp.einsum('bctsg,bctsgr,bcsgrp->bctgrp', cb, decay, xd)
    decay_to_end = jnp.exp(a_cum[:, :, -1:] - a_cum)
    chunk_states = jnp.einsum('bcsgn,bcsgr,bcsgrp->bcgrpn', bm, decay_to_end, xd)
    chunk_decay = jnp.exp(a_cum[:, :, -1])

    def step(state, inp):
        st, dec = inp
        return state * dec[..., None, None] + st, state

    final, entering = lax.scan(step, init_state.reshape(b, g, r, p, n),
                               (jnp.moveaxis(chunk_states, 1, 0), jnp.moveaxis(chunk_decay, 1, 0)))
    entering = jnp.moveaxis(entering, 0, 1)
    y_off = jnp.einsum('bctgn,bcgrpn,bctgr->bctgrp', cm, entering, jnp.exp(a_cum))
    return (y_diag + y_off).reshape(b, l, h, p), final.reshape(b, h, p, n)


def ssd_mixer(z, xbc, dt_raw, conv_prev, ssm_prev, conv_w, conv_b, dt_bias, a_log, d_skip, norm_g, chunk):
    b, t, _ = xbc.shape
    xc, conv_new = causal_conv(xbc, conv_prev, conv_w, conv_b)
    xc = jax.nn.silu(xc.astype(F32))
    xs = xc[..., :C_D_INNER].reshape(b, t, C_HEADS, C_HEAD_DIM)
    bm = xc[..., C_D_INNER:C_D_INNER + C_GROUPS * D_STATE].reshape(b, t, C_GROUPS, D_STATE)
    cm = xc[..., C_D_INNER + C_GROUPS * D_STATE:].reshape(b, t, C_GROUPS, D_STATE)
    dt = jax.nn.softplus(dt_raw.astype(F32) + dt_bias.astype(F32))
    a_neg = -jnp.exp(a_log.astype(F32))
    y, ssm_new = ssd_scan(xs, dt, a_neg, bm, cm, ssm_prev.astype(F32), chunk)
    y = y + xs * d_skip.astype(F32)[:, None]
    y = y.reshape(b, t, C_D_INNER) * jax.nn.silu(z.astype(F32))
    yg = y.reshape(b, t, C_GROUPS, C_D_INNER // C_GROUPS)
    yg = yg * lax.rsqrt(jnp.mean(yg * yg, axis=-1, keepdims=True) + EPS)
    y = yg.reshape(b, t, C_D_INNER) * norm_g.astype(F32)
    return y.astype(xbc.dtype), conv_new, ssm_new


def layer_forward(x, pos, past, lp, ssd_chunk):
    (norm1_g, w_in, a_q_norm_g, a_k_norm_g, idx_k_norm_g, b_q_norm_g, b_k_norm_g,
     c_conv_w, c_conv_b, c_dt_bias, c_a_log, c_d, c_norm_g, w_out, norm2_g,
     w_gate, w_up, w_down) = lp
    b, t, _ = x.shape
    h = rms_norm(x, norm1_g)
    (a_q, a_k, a_v, i_q, i_k, i_w, b_q, b_k, b_v, c_z, c_xbc, c_dt) = split_cols(h @ w_in)
    cos, sin = rope_tables(pos)
    aq = partial_rope(rms_norm(a_q.reshape(b, t, A_HEADS, HEAD_DIM), a_q_norm_g), cos, sin)
    ak = partial_rope(rms_norm(a_k.reshape(b, t, A_KV_HEADS, HEAD_DIM), a_k_norm_g), cos, sin)
    av = a_v.reshape(b, t, A_KV_HEADS, HEAD_DIM)
    iq = partial_rope(i_q.reshape(b, t, IDX_HEADS, IDX_DIM), cos, sin)
    ik = partial_rope(rms_norm(i_k, idx_k_norm_g)[:, :, None, :], cos, sin)[:, :, 0]
    iw = i_w.astype(F32) * (IDX_HEADS ** -0.5)
    bq = rms_norm(b_q.reshape(b, t, B_HEADS, HEAD_DIM), b_q_norm_g)
    bk = rms_norm(b_k.reshape(b, t, B_HEADS, HEAD_DIM), b_k_norm_g)
    bv = b_v.reshape(b, t, B_HEADS, HEAD_DIM)
    if past is None:
        kpos = pos
        ka, va, ika, kb, vb = ak, av, ik, bk, bv
        conv_prev = jnp.zeros((b, CONV_WIDTH - 1, C_CONV_DIM), x.dtype)
        ssm_prev = jnp.zeros((b, C_HEADS, C_HEAD_DIM, D_STATE), F32)
        y_a = sweep_query_blocks(lambda q_, iq_, iw_, qp: dsa_block(q_, iq_, iw_, qp, ka, va, ika, kpos),
                                 (aq, iq, iw), pos)
        y_b = sweep_query_blocks(lambda q_, qp: stick_breaking_block(q_, qp, kb, vb, kpos), (bq,), pos)
    else:
        (pk_a, pv_a, pik, pk_b, pv_b, ssm_prev, conv_prev) = past
        kpos = jnp.arange(pk_a.shape[1] + t, dtype=jnp.int32)
        ka = jnp.concatenate([pk_a.astype(x.dtype), ak], axis=1)
        va = jnp.concatenate([pv_a.astype(x.dtype), av], axis=1)
        ika = jnp.concatenate([pik.astype(x.dtype), ik], axis=1)
        kb = jnp.concatenate([pk_b.astype(x.dtype), bk], axis=1)
        vb = jnp.concatenate([pv_b.astype(x.dtype), bv], axis=1)
        y_a = dsa_block(aq, iq, iw, pos, ka, va, ika, kpos)
        y_b = stick_breaking_block(bq, pos, kb, vb, kpos)
    y_c, conv_new, ssm_new = ssd_mixer(c_z, c_xbc, c_dt, conv_prev, ssm_prev, c_conv_w, c_conv_b,
                                       c_dt_bias, c_a_log, c_d, c_norm_g, ssd_chunk)
    x = x + jnp.concatenate([y_a, y_b, y_c], axis=-1) @ w_out
    h2 = rms_norm(x, norm2_g)
    x = x + (jax.nn.silu(h2 @ w_gate) * (h2 @ w_up)) @ w_down
    return x, (ak, av, ik, bk, bv, ssm_new.astype(x.dtype), conv_new)


def setup_inputs(seed: int = 0) -> dict:
    key = jax.random.key(seed)
    ks = jax.random.split(key, 32)

    def nrm(k, shape, scale):
        return jax.random.normal(k, shape, F32) * scale

    def gain(k, shape):
        return 1.0 + 0.02 * jax.random.normal(k, shape, F32)

    dt0 = jnp.exp(jax.random.uniform(ks[20], (DEPTH, C_HEADS), F32,
                                     minval=math.log(1e-3), maxval=math.log(1e-1)))
    return {
        'x_prompt': nrm(ks[0], (BATCH, SEQ, D_MODEL), 1.0),
        'x_sample': nrm(ks[1], (DEC_BATCH, DEC_SEQ, D_MODEL), 1.0),
        'cache_a_k': nrm(ks[2], (DEPTH, DEC_BATCH, PAST_LEN, A_KV_HEADS, HEAD_DIM), 1.0),
        'cache_a_v': nrm(ks[3], (DEPTH, DEC_BATCH, PAST_LEN, A_KV_HEADS, HEAD_DIM), 1.0),
        'cache_a_idx_k': nrm(ks[4], (DEPTH, DEC_BATCH, PAST_LEN, IDX_DIM), 1.0),
        'cache_b_k': nrm(ks[5], (DEPTH, DEC_BATCH, PAST_LEN, B_HEADS, HEAD_DIM), 1.0),
        'cache_b_v': nrm(ks[6], (DEPTH, DEC_BATCH, PAST_LEN, B_HEADS, HEAD_DIM), 1.0),
        'state_c_ssm': nrm(ks[7], (DEPTH, DEC_BATCH, C_HEADS, C_HEAD_DIM, D_STATE), 0.5),
        'state_c_conv': nrm(ks[8], (DEPTH, DEC_BATCH, CONV_WIDTH - 1, C_CONV_DIM), 1.0),
        'norm1_g': gain(ks[9], (DEPTH, D_MODEL)),
        'w_in': nrm(ks[10], (DEPTH, D_MODEL, IN_COLS), D_MODEL ** -0.5),
        'a_q_norm_g': gain(ks[11], (DEPTH, HEAD_DIM)),
        'a_k_norm_g': gain(ks[12], (DEPTH, HEAD_DIM)),
        'idx_k_norm_g': gain(ks[13], (DEPTH, IDX_DIM)),
        'b_q_norm_g': gain(ks[14], (DEPTH, HEAD_DIM)),
        'b_k_norm_g': gain(ks[15], (DEPTH, HEAD_DIM)),
        'c_conv_w': nrm(ks[16], (DEPTH, CONV_WIDTH, C_CONV_DIM), CONV_WIDTH ** -0.5),
        'c_conv_b': nrm(ks[17], (DEPTH, C_CONV_DIM), 0.01),
        'c_dt_bias': dt0 + jnp.log(-jnp.expm1(-dt0)),
        'c_a_log': jnp.log(jax.random.uniform(ks[18], (DEPTH, C_HEADS), F32, minval=1.0, maxval=16.0)),
        'c_d': 1.0 + 0.1 * jax.random.normal(ks[19], (DEPTH, C_HEADS), F32),
        'c_norm_g': gain(ks[21], (DEPTH, C_D_INNER)),
        'w_out': nrm(ks[22], (DEPTH, MIX_WIDTH, D_MODEL), MIX_WIDTH ** -0.5),
        'norm2_g': gain(ks[23], (DEPTH, D_MODEL)),
        'w_gate': nrm(ks[24], (DEPTH, D_MODEL, D_FF), D_MODEL ** -0.5),
        'w_up': nrm(ks[25], (DEPTH, D_MODEL, D_FF), D_MODEL ** -0.5),
        'w_down': nrm(ks[26], (DEPTH, D_FF, D_MODEL), D_FF ** -0.5),
    }


def reference(x_prompt, x_sample, cache_a_k, cache_a_v, cache_a_idx_k, cache_b_k, cache_b_v,
              state_c_ssm, state_c_conv, norm1_g, w_in, a_q_norm_g, a_k_norm_g, idx_k_norm_g,
              b_q_norm_g, b_k_norm_g, c_conv_w, c_conv_b, c_dt_bias, c_a_log, c_d, c_norm_g,
              w_out, norm2_g, w_gate, w_up, w_down):
    seq_p = x_prompt.shape[1]
    seq_s = x_sample.shape[1]
    past_len = cache_a_k.shape[2]
    pos_p = jnp.arange(seq_p, dtype=jnp.int32)
    pos_s = past_len + jnp.arange(seq_s, dtype=jnp.int32)
    yp = x_prompt
    ys = x_sample
    p_states = []
    s_states = []
    for l in range(DEPTH):
        lp = (norm1_g[l], w_in[l], a_q_norm_g[l], a_k_norm_g[l], idx_k_norm_g[l], b_q_norm_g[l],
              b_k_norm_g[l], c_conv_w[l], c_conv_b[l], c_dt_bias[l], c_a_log[l], c_d[l], c_norm_g[l],
              w_out[l], norm2_g[l], w_gate[l], w_up[l], w_down[l])
        yp, st_p = layer_forward(yp, pos_p, None, lp, min(SSD_CHUNK, seq_p))
        past = (cache_a_k[l], cache_a_v[l], cache_a_idx_k[l], cache_b_k[l], cache_b_v[l],
                state_c_ssm[l], state_c_conv[l])
        ys, st_s = layer_forward(ys, pos_s, past, lp, seq_s)
        p_states.append(st_p)
        s_states.append(st_s)
    (p_a_k, p_a_v, p_a_idx_k, p_b_k, p_b_v, p_c_ssm, p_c_conv) = [jnp.stack(z) for z in zip(*p_states)]
    (s_a_k, s_a_v, s_a_idx_k, s_b_k, s_b_v, s_c_ssm, s_c_conv) = [jnp.stack(z) for z in zip(*s_states)]
    return (yp, ys, p_a_k, p_a_v, p_a_idx_k, p_b_k, p_b_v, p_c_ssm, p_c_conv,
            s_a_k, s_a_v, s_a_idx_k, s_b_k, s_b_v, s_c_ssm, s_c_conv)
```

```python
import functools
import math

import jax
import jax.numpy as jnp
from jax import lax
from jax.experimental import pallas as pl
from jax.experimental.pallas import tpu as pltpu

F32 = jnp.float32
BF16 = jnp.bfloat16
I32 = jnp.int32

D_MODEL = 2048
EPS = 1e-6
CHUNK = 64
HEAD_DIM = 64
ROPE_DIM = HEAD_DIM // 4
ROPE_THETA = 500000.0
A_HEADS = 8
A_KV_HEADS = 2
A_WIDTH = A_HEADS * HEAD_DIM
IDX_HEADS = 16
IDX_DIM = 64
TOPK_MAX = 256
B_HEADS = 8
B_WIDTH = B_HEADS * HEAD_DIM
C_D_INNER = D_MODEL // 2
C_HEAD_DIM = 64
C_HEADS = C_D_INNER // C_HEAD_DIM
C_GROUPS = 4
D_STATE = 128
CONV_WIDTH = 4
C_CONV_DIM = C_D_INNER + 2 * C_GROUPS * D_STATE
D_FF = -(-8 * D_MODEL // (3 * 256)) * 256
IN_SIZES = (A_WIDTH, A_KV_HEADS * HEAD_DIM, A_KV_HEADS * HEAD_DIM, IDX_HEADS * IDX_DIM, IDX_DIM, IDX_HEADS,
            B_WIDTH, B_WIDTH, B_WIDTH, C_D_INNER, C_CONV_DIM, C_HEADS)

LANES = 128
VMEM_LIMIT = 48 * 1024 * 1024

COL_XBC, COL_IQ, COL_Z, COL_AQ, COL_BQ, COL_BK, COL_BV, COL_AK, COL_AV, COL_MISC = (
    0, 2048, 3072, 4096, 4608, 5120, 5632, 6144, 6272, 6400)
P_COLS = 6656
MISC_IK, MISC_IW, MISC_DT = 0, 64, 80

NEG_BIG = -1e30
INT_MIN = -2 ** 31
KEY_POS_INF = 0x7F800000
KEY_NEG_INF = INT_MIN + 0x7FFFFF
SB_DEAD = -104.0


def _params(sem):
    return pltpu.CompilerParams(dimension_semantics=sem, vmem_limit_bytes=VMEM_LIMIT)


def _tile(n, pref):
    t = min(n, pref)
    while n % t:
        t -= 8
    return t


def _dot_nt(a, b):
    return lax.dot_general(a, b, (((1,), (1,)), ((), ())), preferred_element_type=F32)


def _dot_tn(a, b):
    return lax.dot_general(a, b, (((0,), (0,)), ((), ())), preferred_element_type=F32)


def _dot(a, b):
    return jnp.dot(a, b, preferred_element_type=F32)


def _split3(x):
    h1 = x.astype(BF16)
    r1 = x - h1.astype(F32)
    h2 = r1.astype(BF16)
    h3 = (r1 - h2.astype(F32)).astype(BF16)
    return h1, h2, h3


def _softplus(x):
    return jnp.maximum(x, 0.0) + jnp.log1p(jnp.exp(-jnp.abs(x)))


def _silu(x):
    return x * (1.0 / (1.0 + jnp.exp(-x)))


def _proj_kernel(x_ref, g_ref, w_ref, o_ref, h_ref):
    @pl.when(pl.program_id(1) == 0)
    def _():
        x = x_ref[...]
        ms = jnp.mean(x * x, axis=-1, keepdims=True)
        h_ref[...] = (x * lax.rsqrt(ms + EPS) * g_ref[...]).astype(BF16)

    o_ref[...] = _dot(h_ref[...], w_ref[...])


def _proj(x, g, w):
    t, d = x.shape
    n = w.shape[1]
    tm = _tile(t, 512)
    tn = 512
    return pl.pallas_call(
        _proj_kernel,
        grid=(t // tm, n // tn),
        in_specs=[pl.BlockSpec((tm, d), lambda i, j: (i, 0)),
                  pl.BlockSpec((1, d), lambda i, j: (0, 0)),
                  pl.BlockSpec((d, tn), lambda i, j: (0, j))],
        out_specs=pl.BlockSpec((tm, tn), lambda i, j: (i, j)),
        out_shape=jax.ShapeDtypeStruct((t, n), F32),
        scratch_shapes=[pltpu.VMEM((tm, d), BF16)],
        compiler_params=_params(("parallel", "arbitrary")),
        name="in_proj",
    )(x, g, w)


def _post_kernel(aq_r, bq_r, bk_r, bv_r, iq_r, ak_r, av_r, misc_r, rc_r, rs1_r, rs2_r,
                 gaq_r, gak_r, gik_r, gbq_r, gbk_r,
                 aq_o, ak_o, ak16_o, av16_o, iq_o, ik_o, ik16_o, iw_o, bq_o, bk_o, bk16_o, bv16_o):
    r = lax.broadcasted_iota(I32, (LANES, LANES), 0) // HEAD_DIM
    c = lax.broadcasted_iota(I32, (LANES, LANES), 1) // HEAD_DIM
    segm = jnp.where(r == c, 1.0, 0.0).astype(BF16)
    rc, rs1, rs2 = rc_r[...], rs1_r[...], rs2_r[...]

    def headnorm(xc, g_row):
        h1, h2, h3 = _split3(xc * xc)
        ss = _dot(h1, segm) + _dot(h2, segm) + _dot(h3, segm)
        return xc * lax.rsqrt(ss * (1.0 / HEAD_DIM) + EPS) * g_row

    def rope(xc):
        return xc * rc + pltpu.roll(xc, 8, 1) * rs1 + pltpu.roll(xc, LANES - 8, 1) * rs2

    for k in range(A_WIDTH // LANES):
        sl = slice(k * LANES, (k + 1) * LANES)
        aq_o[:, sl] = (rope(headnorm(aq_r[:, sl], gaq_r[...])) * (HEAD_DIM ** -0.5)).astype(BF16)
        bq_o[:, sl] = (headnorm(bq_r[:, sl], gbq_r[...]) * (HEAD_DIM ** -0.5)).astype(BF16)
        bk = headnorm(bk_r[:, sl], gbk_r[...])
        bk_o[:, sl] = bk
        bk16_o[:, sl] = bk.astype(BF16)
        bv16_o[:, sl] = bv_r[:, sl].astype(BF16)
    for k in range(IDX_HEADS * IDX_DIM // LANES):
        sl = slice(k * LANES, (k + 1) * LANES)
        iq_o[:, sl] = rope(iq_r[:, sl]).astype(BF16)
    ak = rope(headnorm(ak_r[...], gak_r[...]))
    ak_o[...] = ak
    ak16_o[...] = ak.astype(BF16)
    av16_o[...] = av_r[...].astype(BF16)
    misc = misc_r[...]
    ik = rope(headnorm(misc, gik_r[...]))[:, MISC_IK:MISC_IK + IDX_DIM]
    ik_o[...] = ik
    ik16_o[...] = ik.astype(BF16)
    iw_o[...] = misc[:, MISC_IW:MISC_IW + IDX_HEADS] * (IDX_HEADS ** -0.5 * IDX_DIM ** -0.5)


def _post(p, rc, rs1, rs2, gaq, gak, gik, gbq, gbk):
    t = p.shape[0]
    tm = _tile(t, 256)

    def col(width, off):
        return pl.BlockSpec((tm, width), lambda i: (i, off // width))

    def row(width):
        return pl.BlockSpec((1, width), lambda i: (0, 0))

    def out(width):
        return pl.BlockSpec((tm, width), lambda i: (i, 0))

    widths_dtypes = [(A_WIDTH, BF16), (LANES, F32), (LANES, BF16), (LANES, BF16), (IDX_HEADS * IDX_DIM, BF16),
                     (IDX_DIM, F32), (IDX_DIM, BF16), (IDX_HEADS, F32), (B_WIDTH, BF16), (B_WIDTH, F32),
                     (B_WIDTH, BF16), (B_WIDTH, BF16)]
    return pl.pallas_call(
        _post_kernel,
        grid=(t // tm,),
        in_specs=[col(A_WIDTH, COL_AQ), col(B_WIDTH, COL_BQ), col(B_WIDTH, COL_BK), col(B_WIDTH, COL_BV),
                  col(IDX_HEADS * IDX_DIM, COL_IQ), col(LANES, COL_AK), col(LANES, COL_AV), col(LANES, COL_MISC),
                  out(LANES), out(LANES), out(LANES),
                  row(LANES), row(LANES), row(LANES), row(LANES), row(LANES)],
        out_specs=[out(w) for w, _ in widths_dtypes],
        out_shape=[jax.ShapeDtypeStruct((t, w), dt) for w, dt in widths_dtypes],
        compiler_params=_params(("parallel",)),
        name="head_post",
    )(p, p, p, p, p, p, p, p, rc, rs1, rs2, gaq, gak, gik, gbq, gbk)


def _dsa_kernel(aq_ref, iq_ref, iw_ref, ik_ref, ak_ref, av_ref, o_ref, key_ref, bias_ref, wb_ref, *,
                tq, n_keys, q0, topk, kblk):
    qstart = q0 + pl.program_id(1) * tq
    kmax = jnp.minimum(n_keys, ((qstart + tq - 1) // CHUNK + 1) * CHUNK)
    nkb = (kmax + kblk - 1) // kblk
    qpos = qstart + lax.broadcasted_iota(I32, (tq, 1), 0)
    kiota = lax.broadcasted_iota(I32, (1, kblk), 1)

    iw = iw_ref[...]
    for h in range(IDX_HEADS):
        wb_ref[h] = jnp.broadcast_to(iw[:, h:h + 1], (tq, kblk))

    def score_body(kb, carry):
        start = pl.multiple_of(kb * kblk, kblk)
        ikb = ik_ref[0, pl.ds(start, kblk), :]
        acc = jnp.zeros((tq, kblk), F32)
        for h in range(IDX_HEADS):
            logit = _dot_nt(iq_ref[:, h * IDX_DIM:(h + 1) * IDX_DIM], ikb)
            acc = acc + wb_ref[h] * jnp.maximum(logit, 0.0)
        kpos = start + kiota
        adm = jnp.logical_and(kpos // CHUNK <= qpos // CHUNK, kpos < n_keys)
        score = jnp.where(adm, acc, -jnp.inf)
        bits = pltpu.bitcast(score, I32)
        bits = jnp.where(bits == INT_MIN, 0, bits)
        key_ref[kb] = bits ^ ((bits >> 31) & 0x7FFFFFFF)
        return carry

    lax.fori_loop(0, nkb, score_body, 0)

    def count(pred):
        def body(kb, acc):
            hit = pred(key_ref[kb], kb)
            for c in range(kblk // LANES):
                acc = acc + jnp.where(hit[:, c * LANES:(c + 1) * LANES], 1.0, 0.0)
            return acc
        acc = lax.fori_loop(0, nkb, body, jnp.zeros((tq, LANES), F32))
        return jnp.sum(acc, axis=1, keepdims=True)

    kf = float(topk)

    def bit_body(i, thr):
        cand = thr + (jnp.int32(1) << (31 - i))
        cnt = count(lambda key, kb: key >= cand)
        return jnp.where(cnt >= kf, cand, thr)

    thr = lax.fori_loop(0, 32, bit_body, jnp.full((tq, 1), INT_MIN, I32))
    n_gt = count(lambda key, kb: key > thr)
    n_ge = count(lambda key, kb: key >= thr)
    need = kf - n_gt

    def tie_search():
        def jbody(i, jl):
            cand = jl + (jnp.int32(1) << (14 - i))
            cnt = count(lambda key, kb: jnp.logical_and(key == thr, kb * kblk + kiota < cand))
            return jnp.where(cnt <= need, cand, jl)
        return lax.fori_loop(0, 15, jbody, jnp.zeros((tq, 1), I32))

    jlim = lax.cond(jnp.max(n_ge) > kf, tie_search, lambda: jnp.full((tq, 1), 2 ** 30, I32))

    def bias_body(kb, carry):
        key = key_ref[kb]
        kpos = kb * kblk + kiota
        sel = jnp.logical_or(key > thr, jnp.logical_and(key == thr, kpos < jlim))
        sel = jnp.logical_and(sel, jnp.logical_and(key > KEY_NEG_INF, key < KEY_POS_INF))
        bias_ref[kb] = jnp.where(sel, 0.0, NEG_BIG)
        return carry

    lax.fori_loop(0, nkb, bias_body, 0)

    rep = A_HEADS // A_KV_HEADS
    for h in range(A_HEADS):
        g = h // rep
        qh = aq_ref[:, h * HEAD_DIM:(h + 1) * HEAD_DIM]

        def att_body(kb, carry):
            m, l, acc = carry
            start = pl.multiple_of(kb * kblk, kblk)
            kk = ak_ref[0, pl.ds(start, kblk), g * HEAD_DIM:(g + 1) * HEAD_DIM]
            vv = av_ref[0, pl.ds(start, kblk), g * HEAD_DIM:(g + 1) * HEAD_DIM]
            s = _dot_nt(qh, kk) + bias_ref[kb]
            m_new = jnp.maximum(m, jnp.max(s, axis=1, keepdims=True))
            alpha = jnp.exp(m - m_new)
            p = jnp.exp(s - m_new)
            l = alpha * l + jnp.sum(p, axis=1, keepdims=True)
            acc = alpha * acc + _dot(p.astype(BF16), vv)
            return m_new, l, acc

        m, l, acc = lax.fori_loop(
            0, nkb, att_body,
            (jnp.full((tq, 1), -3e38, F32), jnp.zeros((tq, 1), F32), jnp.zeros((tq, HEAD_DIM), F32)))
        o_ref[:, h * HEAD_DIM:(h + 1) * HEAD_DIM] = (acc / l).astype(o_ref.dtype)


def _dsa_call(aq, iq, iw, ik16, ak16, av16, *, batch, seq_q, tq, row0, n_keys, q0, kblk=256):
    lp = ik16.shape[1]
    nqb = seq_q // tq
    rb0 = row0 // tq
    topk = min(TOPK_MAX, n_keys // 4)
    nkb_max = lp // kblk

    def qspec(width):
        return pl.BlockSpec((tq, width), lambda b, i: (rb0 + b * nqb + i, 0))

    def kspec(width):
        return pl.BlockSpec((1, lp, width), lambda b, i: (b, 0, 0))

    kern = functools.partial(_dsa_kernel, tq=tq, n_keys=n_keys, q0=q0, topk=topk, kblk=kblk)
    return pl.pallas_call(
        kern,
        grid=(batch, nqb),
        in_specs=[qspec(A_WIDTH), qspec(IDX_HEADS * IDX_DIM), qspec(IDX_HEADS),
                  kspec(IDX_DIM), kspec(LANES), kspec(LANES)],
        out_specs=pl.BlockSpec((tq, A_WIDTH), lambda b, i: (b * nqb + i, 0)),
        out_shape=jax.ShapeDtypeStruct((batch * seq_q, A_WIDTH), BF16),
        scratch_shapes=[pltpu.VMEM((nkb_max, tq, kblk), I32),
                        pltpu.VMEM((nkb_max, tq, kblk), F32),
                        pltpu.VMEM((IDX_HEADS, tq, kblk), F32)],
        compiler_params=_params(("parallel", "arbitrary")),
        name="dsa_attention",
    )(aq, iq, iw, ik16, ak16, av16)


def _sb_kernel(q_ref, k_ref, v_ref, o_ref, *, tq, q0, kblk):
    qstart = q0 + pl.program_id(1) * tq
    j0 = (qstart + tq - 1) // kblk
    qpos = qstart + lax.broadcasted_iota(I32, (tq, 1), 0)
    kiota = lax.broadcasted_iota(I32, (1, kblk), 1)
    r = lax.broadcasted_iota(I32, (kblk, kblk), 0)
    c = lax.broadcasted_iota(I32, (kblk, kblk), 1)
    upper = jnp.where(r > c, 1.0, 0.0).astype(BF16)

    for h in range(B_HEADS):
        hs = slice(h * HEAD_DIM, (h + 1) * HEAD_DIM)
        qh = q_ref[:, hs]

        def cond(state):
            j, carry, _ = state
            return jnp.logical_and(j >= 0, jnp.max(carry) > SB_DEAD)

        def body(state):
            j, carry, acc = state
            start = pl.multiple_of(j * kblk, kblk)
            z = _dot_nt(qh, k_ref[0, pl.ds(start, kblk), hs])
            causal = start + kiota < qpos
            sp = _softplus(z)
            lk = jnp.where(causal, -sp, 0.0)
            l1, l2, l3 = _split3(lk)
            later = carry + (_dot(l1, upper) + _dot(l2, upper) + _dot(l3, upper))
            a = jnp.where(causal, jnp.exp((z - sp) + later), 0.0)
            acc = acc + _dot(a.astype(BF16), v_ref[0, pl.ds(start, kblk), hs])
            carry = carry + jnp.sum(lk, axis=1, keepdims=True)
            return j - 1, carry, acc

        _, _, acc = lax.while_loop(
            cond, body, (j0, jnp.zeros((tq, 1), F32), jnp.zeros((tq, HEAD_DIM), F32)))
        o_ref[:, hs] = acc.astype(o_ref.dtype)


def _sb_call(bq, k16, v16, *, batch, seq_q, tq, row0, q0, kblk=128):
    lp = k16.shape[1]
    nqb = seq_q // tq
    rb0 = row0 // tq
    kern = functools.partial(_sb_kernel, tq=tq, q0=q0, kblk=kblk)
    return pl.pallas_call(
        kern,
        grid=(batch, nqb),
        in_specs=[pl.BlockSpec((tq, B_WIDTH), lambda b, i: (rb0 + b * nqb + i, 0)),
                  pl.BlockSpec((1, lp, B_WIDTH), lambda b, i: (b, 0, 0)),
                  pl.BlockSpec((1, lp, B_WIDTH), lambda b, i: (b, 0, 0))],
        out_specs=pl.BlockSpec((tq, B_WIDTH), lambda b, i: (b * nqb + i, 0)),
        out_shape=jax.ShapeDtypeStruct((batch * seq_q, B_WIDTH), BF16),
        compiler_params=_params(("parallel", "arbitrary")),
        name="stick_breaking",
    )(bq, k16, v16)


def _ssd_kernel(z_ref, xbc_ref, misc_ref, cprev_ref, sprev_ref, cw_ref, cb_ref, dtb_ref, alog_ref, dsk_ref,
                ng_ref, y_ref, snew_ref, xpad_ref, st_ref, *, cs):
    ci = pl.program_id(1)
    nci = pl.num_programs(1)
    hpg = C_HEADS // C_GROUPS
    gw = hpg * C_HEAD_DIM

    @pl.when(ci == 0)
    def _():
        xpad_ref[5:8, :] = cprev_ref[0]
        for g in range(C_GROUPS):
            st_ref[g] = sprev_ref[0, g * hpg:(g + 1) * hpg].reshape(gw, D_STATE).T

    @pl.when(ci > 0)
    def _():
        xpad_ref[5:8, :] = xpad_ref[cs + 5:cs + 8, :]

    xpad_ref[8:8 + cs, :] = xbc_ref[...]
    xc = cb_ref[...] + xpad_ref[5:5 + cs, :] * cw_ref[0:1, :]
    for w in range(1, CONV_WIDTH):
        xc = xc + xpad_ref[5 + w:5 + w + cs, :] * cw_ref[w:w + 1, :]
    xc = _silu(xc)

    dt = _softplus(misc_ref[...] + dtb_ref[...])
    a = dt * (-jnp.exp(alog_ref[...]))
    r = lax.broadcasted_iota(I32, (cs, cs), 0)
    c = lax.broadcasted_iota(I32, (cs, cs), 1)
    tril = r >= c
    tril16 = jnp.where(tril, 1.0, 0.0).astype(BF16)
    a1, a2, a3 = _split3(a)
    a_cum = _dot(tril16, a1) + _dot(tril16, a2) + _dot(tril16, a3)
    pad = LANES - cs
    a_sq = a_cum if pad == 0 else jnp.concatenate([a_cum, jnp.zeros((pad, LANES), F32)], axis=0)
    a_cum_t = a_sq.T
    a_last = a_cum[cs - 1:cs, :]
    exp_a = jnp.exp(a_cum)
    dte = jnp.exp(a_last - a_cum)
    exp_last = jnp.exp(a_last)

    for g in range(C_GROUPS):
        bm = xc[:, C_D_INNER + g * D_STATE:C_D_INNER + (g + 1) * D_STATE].astype(BF16)
        cm = xc[:, C_D_INNER + (C_GROUPS + g) * D_STATE:C_D_INNER + (C_GROUPS + g + 1) * D_STATE].astype(BF16)
        cb = _dot_nt(cm, bm)
        st = st_ref[g]
        y_off = _dot(cm, st.astype(BF16))
        y_heads, x_heads, ea_heads, dec_heads = [], [], [], []
        for rr in range(hpg):
            h = g * hpg + rr
            ln = MISC_DT + h
            xs_h = xc[:, h * C_HEAD_DIM:(h + 1) * C_HEAD_DIM]
            xd = xs_h * dt[:, ln:ln + 1]
            seg = a_cum[:, ln:ln + 1] - a_cum_t[ln:ln + 1, 0:cs]
            decay = jnp.where(tril, jnp.exp(jnp.where(tril, seg, 0.0)), 0.0)
            y_heads.append(_dot((cb * decay).astype(BF16), xd.astype(BF16)))
            x_heads.append(xd * dte[:, ln:ln + 1])
            ea_heads.append(jnp.broadcast_to(exp_a[:, ln:ln + 1], (cs, C_HEAD_DIM)))
            dec_heads.append(jnp.broadcast_to(exp_last[:, ln:ln + 1], (1, C_HEAD_DIM)))
        y = jnp.concatenate(y_heads, axis=1) + y_off * jnp.concatenate(ea_heads, axis=1)
        st_new = st * jnp.concatenate(dec_heads, axis=1) + _dot_tn(bm, jnp.concatenate(x_heads, axis=1).astype(BF16))
        st_ref[g] = st_new

        gs = slice(g * gw, (g + 1) * gw)
        y = y + xc[:, gs] * dsk_ref[:, gs]
        y = y * _silu(z_ref[:, gs])
        y = y * lax.rsqrt(jnp.mean(y * y, axis=1, keepdims=True) + EPS) * ng_ref[:, gs]
        y_ref[:, gs] = y.astype(y_ref.dtype)

        @pl.when(ci == nci - 1)
        def _():
            snew_ref[0, g * hpg:(g + 1) * hpg] = st_new.T.reshape(hpg, C_HEAD_DIM, D_STATE)


def _ssd_call(p, cprev, sprev, cw, cb, dtb, alog, dsk, ng, *, batch, seq, cs, row0):
    nci = seq // cs
    rb0 = row0 // cs

    def pcol(width, off):
        return pl.BlockSpec((cs, width), lambda b, i: (rb0 + b * nci + i, off // width))

    def row(width, rows=1):
        return pl.BlockSpec((rows, width), lambda b, i: (0, 0))

    kern = functools.partial(_ssd_kernel, cs=cs)
    return pl.pallas_call(
        kern,
        grid=(batch, nci),
        in_specs=[pcol(C_D_INNER, COL_Z), pcol(C_CONV_DIM, COL_XBC), pcol(LANES, COL_MISC),
                  pl.BlockSpec((1, CONV_WIDTH - 1, C_CONV_DIM), lambda b, i: (b, 0, 0)),
                  pl.BlockSpec((1, C_HEADS, C_HEAD_DIM, D_STATE), lambda b, i: (b, 0, 0, 0)),
                  row(C_CONV_DIM, CONV_WIDTH), row(C_CONV_DIM), row(LANES), row(LANES),
                  row(C_D_INNER), row(C_D_INNER)],
        out_specs=[pl.BlockSpec((cs, C_D_INNER), lambda b, i: (b * nci + i, 0)),
                   pl.BlockSpec((1, C_HEADS, C_HEAD_DIM, D_STATE), lambda b, i: (b, 0, 0, 0))],
        out_shape=[jax.ShapeDtypeStruct((batch * seq, C_D_INNER), BF16),
                   jax.ShapeDtypeStruct((batch, C_HEADS, C_HEAD_DIM, D_STATE), F32)],
        scratch_shapes=[pltpu.VMEM((cs + 8, C_CONV_DIM), F32),
                        pltpu.VMEM((C_GROUPS, D_STATE, C_HEADS // C_GROUPS * C_HEAD_DIM), F32)],
        compiler_params=_params(("parallel", "arbitrary")),
        name="ssd_mixer",
    )(p, p, p, cprev, sprev, cw, cb, dtb, alog, dsk, ng)


def _outproj_kernel(x_ref, ya_ref, yb_ref, yc_ref, wa_ref, wb_ref, wc_ref, o_ref):
    o_ref[...] = (x_ref[...] + _dot(ya_ref[...], wa_ref[...]) + _dot(yb_ref[...], wb_ref[...])
                  + _dot(yc_ref[...], wc_ref[...]))


def _outproj(x, ya, yb, yc, w):
    t, d = x.shape
    tm = _tile(t, 256)

    def tok(width):
        return pl.BlockSpec((tm, width), lambda i: (i, 0))

    return pl.pallas_call(
        _outproj_kernel,
        grid=(t // tm,),
        in_specs=[tok(d), tok(A_WIDTH), tok(B_WIDTH), tok(C_D_INNER),
                  pl.BlockSpec((A_WIDTH, d), lambda i: (0, 0)),
                  pl.BlockSpec((B_WIDTH, d), lambda i: (1, 0)),
                  pl.BlockSpec((C_D_INNER, d), lambda i: (1, 0))],
        out_specs=tok(d),
        out_shape=jax.ShapeDtypeStruct((t, d), F32),
        compiler_params=_params(("parallel",)),
        name="out_proj",
    )(x, ya, yb, yc, w, w, w)


def _ffn_kernel(x_ref, g_ref, wg_ref, wu_ref, wd_ref, o_ref, h_ref):
    @pl.when(pl.program_id(1) == 0)
    def _():
        x = x_ref[...]
        ms = jnp.mean(x * x, axis=-1, keepdims=True)
        h_ref[...] = (x * lax.rsqrt(ms + EPS) * g_ref[...]).astype(BF16)
        o_ref[...] = x

    h = h_ref[...]
    act = _silu(_dot(h, wg_ref[...])) * _dot(h, wu_ref[...])
    o_ref[...] += _dot(act.astype(BF16), wd_ref[...])


def _ffn(x, g, wg, wu, wd):
    t, d = x.shape
    f = wg.shape[1]
    tm = _tile(t, 512)
    tf = 512
    return pl.pallas_call(
        _ffn_kernel,
        grid=(t // tm, f // tf),
        in_specs=[pl.BlockSpec((tm, d), lambda i, j: (i, 0)),
                  pl.BlockSpec((1, d), lambda i, j: (0, 0)),
                  pl.BlockSpec((d, tf), lambda i, j: (0, j)),
                  pl.BlockSpec((d, tf), lambda i, j: (0, j)),
                  pl.BlockSpec((tf, d), lambda i, j: (j, 0))],
        out_specs=pl.BlockSpec((tm, d), lambda i, j: (i, 0)),
        out_shape=jax.ShapeDtypeStruct((t, d), F32),
        scratch_shapes=[pltpu.VMEM((tm, d), BF16)],
        compiler_params=_params(("parallel", "arbitrary")),
        name="swiglu",
    )(x, g, wg, wu, wd)


def _rope_rows(pos):
    half = ROPE_DIM // 2
    inv = 1.0 / (ROPE_THETA ** (jnp.arange(0, ROPE_DIM, 2, dtype=F32) / ROPE_DIM))
    ang = pos.astype(F32)[:, None] * inv[None, :]
    cos, sin = jnp.cos(ang), jnp.sin(ang)
    n = pos.shape[0]
    one = jnp.ones((n, HEAD_DIM - ROPE_DIM), F32)
    zero = jnp.zeros((n, HEAD_DIM - ROPE_DIM), F32)
    zh = jnp.zeros((n, half), F32)
    rc = jnp.concatenate([cos, cos, one], axis=1)
    rs1 = jnp.concatenate([zh, sin, zero], axis=1)
    rs2 = jnp.concatenate([-sin, zh, zero], axis=1)
    return tuple(jnp.tile(a, (1, LANES // HEAD_DIM)) for a in (rc, rs1, rs2))


def _rearrange_w_in(w):
    offs = [0]
    for n in IN_SIZES:
        offs.append(offs[-1] + n)
    (a_q, a_k, a_v, i_q, i_k, i_w, b_q, b_k, b_v, c_z, c_xbc, c_dt) = [
        w[:, offs[i]:offs[i + 1]] for i in range(len(IN_SIZES))]
    used = COL_MISC + IDX_DIM + IDX_HEADS + C_HEADS
    pad = jnp.zeros((w.shape[0], P_COLS - used), w.dtype)
    return jnp.concatenate([c_xbc, i_q, c_z, a_q, b_q, b_k, b_v, a_k, a_v, i_k, i_w, c_dt, pad], axis=1).astype(BF16)


def _lane_row(v, off):
    return jnp.zeros((1, LANES), F32).at[0, off:off + v.shape[0]].set(v)


def _gain_row(g):
    return jnp.tile(g, LANES // HEAD_DIM)[None, :]


def kernel(x_prompt, x_sample, cache_a_k, cache_a_v, cache_a_idx_k, cache_b_k, cache_b_v, state_c_ssm, state_c_conv, norm1_g, w_in, a_q_norm_g, a_k_norm_g, idx_k_norm_g, b_q_norm_g, b_k_norm_g, c_conv_w, c_conv_b, c_dt_bias, c_a_log, c_d, c_norm_g, w_out, norm2_g, w_gate, w_up, w_down):
    bp, sp, d = x_prompt.shape
    bs, ss, _ = x_sample.shape
    depth = w_in.shape[0]
    past = cache_a_k.shape[2]
    tp, ts = bp * sp, bs * ss
    ls = past + ss
    lsp = -(-ls // 256) * 256
    tq_p = 128
    cs_p = 128 if sp % 128 == 0 else sp

    pos = jnp.concatenate([jnp.tile(jnp.arange(sp, dtype=I32), bp),
                           jnp.tile(past + jnp.arange(ss, dtype=I32), bs)])
    rc, rs1, rs2 = _rope_rows(pos)
    x = jnp.concatenate([x_prompt.reshape(tp, d), x_sample.reshape(ts, d)], axis=0)
    zero_conv = jnp.zeros((bp, CONV_WIDTH - 1, C_CONV_DIM), F32)
    zero_ssm = jnp.zeros((bp, C_HEADS, C_HEAD_DIM, D_STATE), F32)

    def with_past(cache, new, width):
        parts = [cache.reshape(bs, past, width).astype(BF16), new[tp:].reshape(bs, ss, width)]
        if lsp > ls:
            parts.append(jnp.zeros((bs, lsp - ls, width), BF16))
        return jnp.concatenate(parts, axis=1)

    p_states, s_states = [], []
    for l in range(depth):
        p = _proj(x, norm1_g[l][None, :], _rearrange_w_in(w_in[l]))
        gik = jnp.concatenate([idx_k_norm_g[l], jnp.ones((LANES - IDX_DIM,), F32)])[None, :]
        (aq, ak, ak16, av16, iq, ik, ik16, iw, bq, bk, bk16, bv16) = _post(
            p, rc, rs1, rs2, _gain_row(a_q_norm_g[l]), _gain_row(a_k_norm_g[l]), gik,
            _gain_row(b_q_norm_g[l]), _gain_row(b_k_norm_g[l]))

        ya_p = _dsa_call(aq, iq, iw, ik16[:tp].reshape(bp, sp, IDX_DIM), ak16[:tp].reshape(bp, sp, LANES),
                         av16[:tp].reshape(bp, sp, LANES), batch=bp, seq_q=sp, tq=tq_p, row0=0, n_keys=sp, q0=0)
        ya_s = _dsa_call(aq, iq, iw, with_past(cache_a_idx_k[l], ik16, IDX_DIM),
                         with_past(cache_a_k[l], ak16, LANES), with_past(cache_a_v[l], av16, LANES),
                         batch=bs, seq_q=ss, tq=ss, row0=tp, n_keys=ls, q0=past)
        yb_p = _sb_call(bq, bk16[:tp].reshape(bp, sp, B_WIDTH), bv16[:tp].reshape(bp, sp, B_WIDTH),
                        batch=bp, seq_q=sp, tq=tq_p, row0=0, q0=0)
        yb_s = _sb_call(bq, with_past(cache_b_k[l], bk16, B_WIDTH), with_past(cache_b_v[l], bv16, B_WIDTH),
                        batch=bs, seq_q=ss, tq=ss, row0=tp, q0=past)
        cw, cb = c_conv_w[l], c_conv_b[l][None, :]
        dtb, alog = _lane_row(c_dt_bias[l], MISC_DT), _lane_row(c_a_log[l], MISC_DT)
        dsk, ng = jnp.repeat(c_d[l], C_HEAD_DIM)[None, :], c_norm_g[l][None, :]
        yc_p, ssm_p = _ssd_call(p, zero_conv, zero_ssm, cw, cb, dtb, alog, dsk, ng,
                                batch=bp, seq=sp, cs=cs_p, row0=0)
        yc_s, ssm_s = _ssd_call(p, state_c_conv[l], state_c_ssm[l], cw, cb, dtb, alog, dsk, ng,
                                batch=bs, seq=ss, cs=ss, row0=tp)

        x = _outproj(x, jnp.concatenate([ya_p, ya_s]), jnp.concatenate([yb_p, yb_s]),
                     jnp.concatenate([yc_p, yc_s]), w_out[l].astype(BF16))
        x = _ffn(x, norm2_g[l][None, :], w_gate[l].astype(BF16), w_up[l].astype(BF16), w_down[l].astype(BF16))

        av = p[:, COL_AV:COL_AV + LANES]
        bv = p[:, COL_BV:COL_BV + B_WIDTH]
        xbc = p[:, COL_XBC:COL_XBC + C_CONV_DIM]

        def state(lo, hi, b, s, ssm):
            return (ak[lo:hi].reshape(b, s, A_KV_HEADS, HEAD_DIM), av[lo:hi].reshape(b, s, A_KV_HEADS, HEAD_DIM),
                    ik[lo:hi].reshape(b, s, IDX_DIM), bk[lo:hi].reshape(b, s, B_HEADS, HEAD_DIM),
                    bv[lo:hi].reshape(b, s, B_HEADS, HEAD_DIM), ssm,
                    xbc[lo:hi].reshape(b, s, C_CONV_DIM)[:, s - (CONV_WIDTH - 1):])

        p_states.append(state(0, tp, bp, sp, ssm_p))
        s_states.append(state(tp, tp + ts, bs, ss, ssm_s))

    p_out = [jnp.stack(zs) for zs in zip(*p_states)]
    s_out = [jnp.stack(zs) for zs in zip(*s_states)]
    return (x[:tp].reshape(bp, sp, d), x[tp:].reshape(bs, ss, d), *p_out, *s_out)
```

```python
import functools

import jax
import jax.numpy as jnp
from jax import lax
from jax.experimental import pallas as pl
from jax.experimental.pallas import tpu as pltpu

F32 = jnp.float32
BF16 = jnp.bfloat16
I32 = jnp.int32

D_MODEL = 2048
EPS = 1e-6
CHUNK = 64
HEAD_DIM = 64
ROPE_DIM = HEAD_DIM // 4
ROPE_THETA = 500000.0
A_HEADS = 8
A_KV_HEADS = 2
A_WIDTH = A_HEADS * HEAD_DIM
IDX_HEADS = 16
IDX_DIM = 64
TOPK_MAX = 256
B_HEADS = 8
B_WIDTH = B_HEADS * HEAD_DIM
C_D_INNER = D_MODEL // 2
C_HEAD_DIM = 64
C_HEADS = C_D_INNER // C_HEAD_DIM
C_GROUPS = 4
D_STATE = 128
CONV_WIDTH = 4
C_CONV_DIM = C_D_INNER + 2 * C_GROUPS * D_STATE
IN_SIZES = (A_WIDTH, A_KV_HEADS * HEAD_DIM, A_KV_HEADS * HEAD_DIM, IDX_HEADS * IDX_DIM, IDX_DIM, IDX_HEADS,
            B_WIDTH, B_WIDTH, B_WIDTH, C_D_INNER, C_CONV_DIM, C_HEADS)

LANES = 128
VMEM_LIMIT = 48 * 1024 * 1024

COL_XBC, COL_IQ, COL_Z, COL_AQ, COL_BQ, COL_BK, COL_BV, COL_AK, COL_AV, COL_MISC = (
    0, 2048, 3072, 4096, 4608, 5120, 5632, 6144, 6272, 6400)
P_COLS = 6656
MISC_IK, MISC_IW, MISC_DT = 0, 64, 80

NEG_BIG = -1e30
INT_MIN = -2 ** 31
KEY_POS_INF = 0x7F800000
KEY_NEG_INF = INT_MIN + 0x7FFFFF
SB_DEAD = -104.0


def _params(sem):
    return pltpu.CompilerParams(dimension_semantics=sem, vmem_limit_bytes=VMEM_LIMIT)


def _tile(n, pref):
    t = min(n, pref)
    while n % t:
        t -= 8
    return t


def _dot_nt(a, b):
    return lax.dot_general(a, b, (((1,), (1,)), ((), ())), preferred_element_type=F32)


def _dot_tn(a, b):
    return lax.dot_general(a, b, (((0,), (0,)), ((), ())), preferred_element_type=F32)


def _dot(a, b):
    return jnp.dot(a, b, preferred_element_type=F32)


def _split3(x):
    h1 = x.astype(BF16)
    r1 = x - h1.astype(F32)
    h2 = r1.astype(BF16)
    h3 = (r1 - h2.astype(F32)).astype(BF16)
    return h1, h2, h3


def _softplus(x):
    return jnp.maximum(x, 0.0) + jnp.log1p(jnp.exp(-jnp.abs(x)))


def _silu(x):
    return x * (1.0 / (1.0 + jnp.exp(-x)))


def _proj_kernel(x_ref, g_ref, w_ref, o_ref, h_ref):
    @pl.when(pl.program_id(1) == 0)
    def _():
        x = x_ref[...]
        ms = jnp.mean(x * x, axis=-1, keepdims=True)
        h_ref[...] = (x * lax.rsqrt(ms + EPS) * g_ref[...]).astype(BF16)

    o_ref[...] = _dot(h_ref[...], w_ref[...])


def _proj(x, g, w):
    t, d = x.shape
    n = w.shape[1]
    tm = _tile(t, 512)
    tn = 512
    return pl.pallas_call(
        _proj_kernel,
        grid=(t // tm, n // tn),
        in_specs=[pl.BlockSpec((tm, d), lambda i, j: (i, 0)),
                  pl.BlockSpec((1, d), lambda i, j: (0, 0)),
                  pl.BlockSpec((d, tn), lambda i, j: (0, j))],
        out_specs=pl.BlockSpec((tm, tn), lambda i, j: (i, j)),
        out_shape=jax.ShapeDtypeStruct((t, n), F32),
        scratch_shapes=[pltpu.VMEM((tm, d), BF16)],
        compiler_params=_params(("parallel", "arbitrary")),
        name="in_proj",
    )(x, g, w)


def _post_kernel(aq_r, bq_r, bk_r, bv_r, iq_r, ak_r, misc_r, rc_r, rs1_r, rs2_r,
                 gaq_r, gak_r, gik_r, gbq_r, gbk_r,
                 aq_o, ak_o, iq_o, ik_o, iw_o, bq_o, bk_o, bk16_o, bv16_o):
    r = lax.broadcasted_iota(I32, (LANES, LANES), 0) // HEAD_DIM
    c = lax.broadcasted_iota(I32, (LANES, LANES), 1) // HEAD_DIM
    segm = jnp.where(r == c, 1.0, 0.0).astype(BF16)
    rc, rs1, rs2 = rc_r[...], rs1_r[...], rs2_r[...]

    def headnorm(xc, g_row):
        h1, h2, h3 = _split3(xc * xc)
        ss = _dot(h1, segm) + _dot(h2, segm) + _dot(h3, segm)
        return xc * lax.rsqrt(ss * (1.0 / HEAD_DIM) + EPS) * g_row

    def rope(xc):
        return xc * rc + pltpu.roll(xc, 8, 1) * rs1 + pltpu.roll(xc, LANES - 8, 1) * rs2

    for k in range(A_WIDTH // LANES):
        sl = slice(k * LANES, (k + 1) * LANES)
        aq_o[:, sl] = (rope(headnorm(aq_r[:, sl], gaq_r[...])) * (HEAD_DIM ** -0.5)).astype(BF16)
        bq_o[:, sl] = (headnorm(bq_r[:, sl], gbq_r[...]) * (HEAD_DIM ** -0.5)).astype(BF16)
        bk = headnorm(bk_r[:, sl], gbk_r[...])
        bk_o[:, sl] = bk
        bk16_o[:, sl] = bk.astype(BF16)
        bv16_o[:, sl] = bv_r[:, sl].astype(BF16)
    for k in range(IDX_HEADS * IDX_DIM // LANES):
        sl = slice(k * LANES, (k + 1) * LANES)
        iq_o[:, sl] = rope(iq_r[:, sl]).astype(BF16)
    ak_o[...] = rope(headnorm(ak_r[...], gak_r[...]))
    misc = misc_r[...]
    ik_o[...] = rope(headnorm(misc, gik_r[...]))[:, MISC_IK:MISC_IK + IDX_DIM]
    iw_o[...] = misc[:, MISC_IW:MISC_IW + IDX_HEADS] * (IDX_HEADS ** -0.5 * IDX_DIM ** -0.5)


def _post(p, rc, rs1, rs2, gaq, gak, gik, gbq, gbk):
    t = p.shape[0]
    tm = _tile(t, 256)

    def col(width, off):
        return pl.BlockSpec((tm, width), lambda i: (i, off // width))

    def row(width):
        return pl.BlockSpec((1, width), lambda i: (0, 0))

    def out(width):
        return pl.BlockSpec((tm, width), lambda i: (i, 0))

    widths_dtypes = [(A_WIDTH, BF16), (LANES, F32), (IDX_HEADS * IDX_DIM, BF16), (IDX_DIM, F32),
                     (IDX_HEADS, F32), (B_WIDTH, BF16), (B_WIDTH, F32), (B_WIDTH, BF16), (B_WIDTH, BF16)]
    return pl.pallas_call(
        _post_kernel,
        grid=(t // tm,),
        in_specs=[col(A_WIDTH, COL_AQ), col(B_WIDTH, COL_BQ), col(B_WIDTH, COL_BK), col(B_WIDTH, COL_BV),
                  col(IDX_HEADS * IDX_DIM, COL_IQ), col(LANES, COL_AK), col(LANES, COL_MISC),
                  out(LANES), out(LANES), out(LANES),
                  row(LANES), row(LANES), row(LANES), row(LANES), row(LANES)],
        out_specs=[out(w) for w, _ in widths_dtypes],
        out_shape=[jax.ShapeDtypeStruct((t, w), dt) for w, dt in widths_dtypes],
        compiler_params=_params(("parallel",)),
        name="head_post",
    )(p, p, p, p, p, p, p, rc, rs1, rs2, gaq, gak, gik, gbq, gbk)


def _dsa_kernel(*refs, tq, sn, lc, topk, kblk, has_cache):
    if has_cache:
        (aq_ref, iq_ref, iw_ref, ikn_ref, akn_ref, avn_ref, ikc_ref, akc_ref, avc_ref, o_ref,
         key_ref, bias_ref, ik16_ref, ak16_ref, vt_ref, iqh_ref, aqh_ref, m_ref, acc_ref) = refs
    else:
        (aq_ref, iq_ref, iw_ref, ikn_ref, akn_ref, avn_ref, o_ref,
         key_ref, bias_ref, ik16_ref, ak16_ref, vt_ref, iqh_ref, aqh_ref, m_ref, acc_ref) = refs
    qi = pl.program_id(1)
    n_keys = lc + sn
    lp = ik16_ref.shape[0]
    rep = A_HEADS // A_KV_HEADS

    @pl.when(qi == 0)
    def _():
        def put(rows, ik, ak, av):
            n = ik.shape[0]
            ik16_ref[rows:rows + n, :] = ik.astype(BF16)
            ak16_ref[rows:rows + n, :] = ak.astype(BF16)
            avt = av.T.astype(BF16)
            for c in range(n // kblk):
                for g in range(A_KV_HEADS):
                    vt_ref[g, (rows // kblk) + c, 0:HEAD_DIM, :] = avt[g * HEAD_DIM:(g + 1) * HEAD_DIM,
                                                                      c * kblk:(c + 1) * kblk]

        step = 2 * kblk
        if has_cache:
            for r0 in range(0, lc, step):
                put(r0, ikc_ref[0, r0:r0 + step, :], akc_ref[0, r0:r0 + step, :], avc_ref[0, r0:r0 + step, :])
        for r0 in range(0, sn - sn % step, step):
            put(lc + r0, ikn_ref[r0:r0 + step, :], akn_ref[r0:r0 + step, :], avn_ref[r0:r0 + step, :])
        rem = sn % step
        if lp > lc + sn - rem:
            n = lp - (lc + sn - rem)

            def padded(ref):
                w = ref.shape[-1]
                if rem == 0:
                    return jnp.zeros((n, w), F32)
                return jnp.concatenate([ref[sn - rem:sn, :], jnp.zeros((n - rem, w), F32)], axis=0)

            put(lc + sn - rem, padded(ikn_ref), padded(akn_ref), padded(avn_ref))
        for g in range(A_KV_HEADS):
            vt_ref[g, :, HEAD_DIM:LANES, :] = jnp.ones((lp // kblk, LANES - HEAD_DIM, kblk), BF16)

    qstart = lc + qi * tq
    kmax = jnp.minimum(n_keys, ((qstart + tq - 1) // CHUNK + 1) * CHUNK)
    nkb = (kmax + kblk - 1) // kblk
    qpos = qstart + lax.broadcasted_iota(I32, (1, LANES), 1)
    krow = lax.broadcasted_iota(I32, (kblk, 1), 0)

    def head_major(dst, src, nheads):
        for h in range(nheads):
            x = src[:, h * HEAD_DIM:(h + 1) * HEAD_DIM]
            if tq < LANES:
                x = jnp.concatenate([x, jnp.zeros((LANES - tq, HEAD_DIM), x.dtype)], axis=0)
            dst[h * LANES:(h + 1) * LANES, :] = x

    head_major(iqh_ref, iq_ref, IDX_HEADS)
    head_major(aqh_ref, aq_ref, A_HEADS)
    iw = jnp.concatenate([iw_ref[...], jnp.zeros((tq, LANES - IDX_HEADS), F32)], axis=1)
    if tq < LANES:
        iw = jnp.concatenate([iw, jnp.zeros((LANES - tq, LANES), F32)], axis=0)
    iw_t = iw.T

    def score_body(kb, carry):
        start = pl.multiple_of(kb * kblk, kblk)
        ikb = ik16_ref[pl.ds(start, kblk), :]
        acc = jnp.zeros((kblk, LANES), F32)
        for pr in range(IDX_HEADS // 2):
            logit = _dot_nt(ikb, iqh_ref[pr * 2 * LANES:(pr + 1) * 2 * LANES, :])
            for e in range(2):
                h = 2 * pr + e
                acc = acc + iw_t[h:h + 1, :] * jnp.maximum(logit[:, e * LANES:(e + 1) * LANES], 0.0)
        kpos = start + krow
        adm = jnp.logical_and(kpos // CHUNK <= qpos // CHUNK, kpos < n_keys)
        bits = pltpu.bitcast(jnp.where(adm, acc, -jnp.inf), I32)
        bits = jnp.where(bits == INT_MIN, 0, bits)
        key_ref[kb] = bits ^ ((bits >> 31) & 0x7FFFFFFF)
        return carry

    lax.fori_loop(0, nkb, score_body, 0)

    def count(pred):
        def body(kb, acc):
            hit = jnp.where(pred(key_ref[kb], kb), 1.0, 0.0)
            return acc + jnp.sum(hit.reshape(kblk // 64, 64, LANES), axis=0)
        acc = lax.fori_loop(0, nkb, body, jnp.zeros((64, LANES), F32))
        return jnp.sum(acc, axis=0, keepdims=True)

    kf = float(topk)

    def bit_body(i, thr):
        cand = thr + (jnp.int32(1) << (31 - i))
        cnt = count(lambda key, kb: key >= cand)
        return jnp.where(cnt >= kf, cand, thr)

    thr = lax.fori_loop(0, 32, bit_body, jnp.full((1, LANES), INT_MIN, I32))
    n_gt = count(lambda key, kb: key > thr)
    n_ge = count(lambda key, kb: key >= thr)
    need = kf - n_gt

    def tie_search():
        def jbody(i, jl):
            cand = jl + (jnp.int32(1) << (14 - i))
            cnt = count(lambda key, kb: jnp.logical_and(key == thr, kb * kblk + krow < cand))
            return jnp.where(cnt <= need, cand, jl)
        return lax.fori_loop(0, 15, jbody, jnp.zeros((1, LANES), I32))

    jlim = lax.cond(jnp.max(n_ge) > kf, tie_search, lambda: jnp.full((1, LANES), 2 ** 30, I32))

    def bias_body(kb, carry):
        key = key_ref[kb]
        kpos = kb * kblk + krow
        sel = jnp.logical_or(key > thr, jnp.logical_and(key == thr, kpos < jlim))
        sel = jnp.logical_and(sel, jnp.logical_and(key > KEY_NEG_INF, key < KEY_POS_INF))
        bias_ref[kb] = jnp.where(sel, 0.0, NEG_BIG)
        return carry

    lax.fori_loop(0, nkb, bias_body, 0)

    m_ref[...] = jnp.full(m_ref.shape, -3e38, F32)
    acc_ref[...] = jnp.zeros(acc_ref.shape, F32)

    def att_body(kb, carry):
        start = pl.multiple_of(kb * kblk, kblk)
        bias = bias_ref[kb]
        m_all = m_ref[...]
        kk = ak16_ref[pl.ds(start, kblk), :]
        s_ts = [_dot_nt(kk[:, (h0 // rep) * HEAD_DIM:(h0 // rep + 1) * HEAD_DIM],
                        aqh_ref[h0 * LANES:(h0 + 2) * LANES, :]) for h0 in range(0, A_HEADS, 2)]
        ps, alphas, m_rows = [], [], []
        for h in range(A_HEADS):
            x = s_ts[h // 2][:, (h % 2) * LANES:(h % 2 + 1) * LANES] + bias
            part = jnp.max(x.reshape(kblk // 64, 64, LANES), axis=0)
            bmax = jnp.max(part, axis=0, keepdims=True)
            m_old = m_all[h:h + 1, :]
            m_new = jnp.maximum(m_old, bmax)
            alphas.append(jnp.exp(m_old - m_new))
            ps.append(jnp.exp(x - m_new).astype(BF16))
            m_rows.append(m_new)
        m_ref[...] = jnp.concatenate(m_rows, axis=0)
        upds = [_dot(vt_ref[h0 // rep, kb], jnp.concatenate(ps[h0:h0 + 2], axis=1)) for h0 in range(0, A_HEADS, 2)]
        for h in range(A_HEADS):
            acc_ref[h] = acc_ref[h] * alphas[h] + upds[h // 2][:, (h % 2) * LANES:(h % 2 + 1) * LANES]
        return carry

    lax.fori_loop(0, nkb, att_body, 0)

    outs = []
    for h in range(A_HEADS):
        a = acc_ref[h]
        outs.append(a[0:HEAD_DIM, :] / a[HEAD_DIM:LANES, :])
    o_t = jnp.concatenate(outs, axis=0)
    o_ref[...] = o_t.T[0:tq, :].astype(o_ref.dtype)


def _dsa_call(aq, iq, iw, ik, ak, p, caches, *, batch, seq_q, tq, row0, kblk=256):
    has_cache = caches is not None
    lc = caches[0].shape[1] if has_cache else 0
    sn = seq_q
    n_keys = lc + sn
    lp = -(-n_keys // kblk) * kblk
    assert lc % (2 * kblk) == 0 and row0 % sn == 0 and sn % tq == 0 and tq % 16 == 0
    nqb = seq_q // tq
    rb0 = row0 // tq
    nb0 = row0 // sn
    topk = min(TOPK_MAX, n_keys // 4)
    nkb_max = lp // kblk

    def qspec(width):
        return pl.BlockSpec((tq, width), lambda b, i: (rb0 + b * nqb + i, 0))

    def nspec(width, colblk=0):
        return pl.BlockSpec((sn, width), lambda b, i: (nb0 + b, colblk))

    def cspec(width):
        return pl.BlockSpec((1, lc, width), lambda b, i: (b, 0, 0))

    in_specs = [qspec(A_WIDTH), qspec(IDX_HEADS * IDX_DIM), qspec(IDX_HEADS),
                nspec(IDX_DIM), nspec(LANES), nspec(LANES, COL_AV // LANES)]
    args = [aq, iq, iw, ik, ak, p]
    if has_cache:
        in_specs += [cspec(IDX_DIM), cspec(LANES), cspec(LANES)]
        args += list(caches)
    kern = functools.partial(_dsa_kernel, tq=tq, sn=sn, lc=lc, topk=topk, kblk=kblk, has_cache=has_cache)
    return pl.pallas_call(
        kern,
        grid=(batch, nqb),
        in_specs=in_specs,
        out_specs=pl.BlockSpec((tq, A_WIDTH), lambda b, i: (b * nqb + i, 0)),
        out_shape=jax.ShapeDtypeStruct((batch * seq_q, A_WIDTH), BF16),
        scratch_shapes=[pltpu.VMEM((nkb_max, kblk, LANES), I32),
                        pltpu.VMEM((nkb_max, kblk, LANES), F32),
                        pltpu.VMEM((lp, IDX_DIM), BF16),
                        pltpu.VMEM((lp, LANES), BF16),
                        pltpu.VMEM((A_KV_HEADS, nkb_max, LANES, kblk), BF16),
                        pltpu.VMEM((IDX_HEADS * LANES, IDX_DIM), BF16),
                        pltpu.VMEM((A_HEADS * LANES, HEAD_DIM), BF16),
                        pltpu.VMEM((A_HEADS, LANES), F32),
                        pltpu.VMEM((A_HEADS, LANES, LANES), F32)],
        compiler_params=_params(("parallel", "arbitrary")),
        name="dsa_attention",
    )(*args)


def _sb_kernel(*refs, tq, sn, lc, kblk, has_cache):
    if has_cache:
        q_ref, kn_ref, vn_ref, kc_ref, vc_ref, o_ref, carry_ref, acc_ref = refs
    else:
        q_ref, kn_ref, vn_ref, o_ref, carry_ref, acc_ref = refs
    qlocal = pl.program_id(1) * tq
    qpos = qlocal + lax.broadcasted_iota(I32, (tq, 1), 0)
    kiota = lax.broadcasted_iota(I32, (1, kblk), 1)
    r = lax.broadcasted_iota(I32, (kblk, kblk), 0)
    c = lax.broadcasted_iota(I32, (kblk, kblk), 1)
    upper = jnp.where(r > c, 1.0, 0.0).astype(BF16)
    carry_ref[...] = jnp.zeros(carry_ref.shape, F32)
    acc_ref[...] = jnp.zeros(acc_ref.shape, F32)

    def visit(k_of, v_of, causal):
        for h in range(B_HEADS):
            hs = slice(h * HEAD_DIM, (h + 1) * HEAD_DIM)
            z = _dot_nt(q_ref[:, hs], k_of(hs))
            sp = _softplus(z)
            lk = -sp if causal is None else jnp.where(causal, -sp, 0.0)
            l1, l2, l3 = _split3(lk)
            carry = carry_ref[:, h:h + 1]
            later = carry + (_dot(l1, upper) + _dot(l2, upper) + _dot(l3, upper))
            a = jnp.exp((z - sp) + later)
            if causal is not None:
                a = jnp.where(causal, a, 0.0)
            acc_ref[:, hs] += _dot(a.astype(BF16), v_of(hs))
            carry_ref[:, h:h + 1] = carry + jnp.sum(lk, axis=1, keepdims=True)
        return jnp.max(carry_ref[:, 0:B_HEADS]) > SB_DEAD

    def alive_cond(state):
        j, alive = state
        return jnp.logical_and(j >= 0, alive)

    if sn >= kblk:
        def new_body(state):
            j, _ = state
            start = pl.multiple_of(j * kblk, kblk)
            alive = visit(lambda hs: kn_ref[pl.ds(start, kblk), hs], lambda hs: vn_ref[pl.ds(start, kblk), hs],
                          start + kiota < qpos)
            return j - 1, alive

        _, alive = lax.while_loop(alive_cond, new_body, ((qlocal + tq - 1) // kblk, True))
    else:
        def padk(ref):
            return lambda hs: jnp.concatenate([ref[:, hs], jnp.zeros((kblk - sn, HEAD_DIM), BF16)], axis=0)

        alive = visit(padk(kn_ref), padk(vn_ref), kiota < qpos)

    if has_cache:
        def cache_body(state):
            j, _ = state
            start = pl.multiple_of(j * kblk, kblk)
            alive = visit(lambda hs: kc_ref[0, pl.ds(start, kblk), hs].astype(BF16),
                          lambda hs: vc_ref[0, pl.ds(start, kblk), hs].astype(BF16), None)
            return j - 1, alive

        lax.while_loop(alive_cond, cache_body, (lc // kblk - 1, alive))

    o_ref[...] = acc_ref[...].astype(o_ref.dtype)


def _sb_call(bq, bk16, bv16, caches, *, batch, seq_q, tq, row0, kblk=128):
    has_cache = caches is not None
    lc = caches[0].shape[1] if has_cache else 0
    sn = seq_q
    assert lc % kblk == 0 and row0 % sn == 0 and sn % tq == 0 and (sn % kblk == 0 or (sn < kblk and tq == sn))
    nqb = seq_q // tq
    rb0 = row0 // tq
    nb0 = row0 // sn
    in_specs = [pl.BlockSpec((tq, B_WIDTH), lambda b, i: (rb0 + b * nqb + i, 0)),
                pl.BlockSpec((sn, B_WIDTH), lambda b, i: (nb0 + b, 0)),
                pl.BlockSpec((sn, B_WIDTH), lambda b, i: (nb0 + b, 0))]
    args = [bq, bk16, bv16]
    if has_cache:
        in_specs += [pl.BlockSpec((1, lc, B_WIDTH), lambda b, i: (b, 0, 0))] * 2
        args += list(caches)
    kern = functools.partial(_sb_kernel, tq=tq, sn=sn, lc=lc, kblk=kblk, has_cache=has_cache)
    return pl.pallas_call(
        kern,
        grid=(batch, nqb),
        in_specs=in_specs,
        out_specs=pl.BlockSpec((tq, B_WIDTH), lambda b, i: (b * nqb + i, 0)),
        out_shape=jax.ShapeDtypeStruct((batch * seq_q, B_WIDTH), BF16),
        scratch_shapes=[pltpu.VMEM((tq, LANES), F32), pltpu.VMEM((tq, B_WIDTH), F32)],
        compiler_params=_params(("parallel", "arbitrary")),
        name="stick_breaking",
    )(*args)


def _ssd_kernel(z_ref, xbc_ref, misc_ref, cprev_ref, sprev_ref, cw_ref, cb_ref, dtb_ref, alog_ref, dsk_ref,
                ng_ref, y_ref, snew_ref, xpad_ref, st_ref, *, cs):
    ci = pl.program_id(1)
    nci = pl.num_programs(1)
    hpg = C_HEADS // C_GROUPS
    gw = hpg * C_HEAD_DIM

    @pl.when(ci == 0)
    def _():
        xpad_ref[5:8, :] = cprev_ref[0]
        for g in range(C_GROUPS):
            st_ref[g] = sprev_ref[0, g * hpg:(g + 1) * hpg].reshape(gw, D_STATE).T

    @pl.when(ci > 0)
    def _():
        xpad_ref[5:8, :] = xpad_ref[cs + 5:cs + 8, :]

    xpad_ref[8:8 + cs, :] = xbc_ref[...]
    xc = cb_ref[...] + xpad_ref[5:5 + cs, :] * cw_ref[0:1, :]
    for w in range(1, CONV_WIDTH):
        xc = xc + xpad_ref[5 + w:5 + w + cs, :] * cw_ref[w:w + 1, :]
    xc = _silu(xc)

    dt = _softplus(misc_ref[...] + dtb_ref[...])
    a = dt * (-jnp.exp(alog_ref[...]))
    r = lax.broadcasted_iota(I32, (cs, cs), 0)
    c = lax.broadcasted_iota(I32, (cs, cs), 1)
    tril = r >= c
    tril16 = jnp.where(tril, 1.0, 0.0).astype(BF16)
    a1, a2, a3 = _split3(a)
    a_cum = _dot(tril16, a1) + _dot(tril16, a2) + _dot(tril16, a3)
    pad = LANES - cs
    a_sq = a_cum if pad == 0 else jnp.concatenate([a_cum, jnp.zeros((pad, LANES), F32)], axis=0)
    a_cum_t = a_sq.T
    a_last = a_cum[cs - 1:cs, :]
    exp_a = jnp.exp(a_cum)
    dte = jnp.exp(a_last - a_cum)
    exp_last = jnp.exp(a_last)

    for g in range(C_GROUPS):
        bm = xc[:, C_D_INNER + g * D_STATE:C_D_INNER + (g + 1) * D_STATE].astype(BF16)
        cm = xc[:, C_D_INNER + (C_GROUPS + g) * D_STATE:C_D_INNER + (C_GROUPS + g + 1) * D_STATE].astype(BF16)
        cb = _dot_nt(cm, bm)
        st = st_ref[g]
        y_off = _dot(cm, st.astype(BF16))
        y_heads, x_heads, ea_heads, dec_heads = [], [], [], []
        for rr in range(hpg):
            h = g * hpg + rr
            ln = MISC_DT + h
            xs_h = xc[:, h * C_HEAD_DIM:(h + 1) * C_HEAD_DIM]
            xd = xs_h * dt[:, ln:ln + 1]
            seg = a_cum[:, ln:ln + 1] - a_cum_t[ln:ln + 1, 0:cs]
            decay = jnp.where(tril, jnp.exp(jnp.where(tril, seg, 0.0)), 0.0)
            y_heads.append(_dot((cb * decay).astype(BF16), xd.astype(BF16)))
            x_heads.append(xd * dte[:, ln:ln + 1])
            ea_heads.append(jnp.broadcast_to(exp_a[:, ln:ln + 1], (cs, C_HEAD_DIM)))
            dec_heads.append(jnp.broadcast_to(exp_last[:, ln:ln + 1], (1, C_HEAD_DIM)))
        y = jnp.concatenate(y_heads, axis=1) + y_off * jnp.concatenate(ea_heads, axis=1)
        st_new = st * jnp.concatenate(dec_heads, axis=1) + _dot_tn(bm, jnp.concatenate(x_heads, axis=1).astype(BF16))
        st_ref[g] = st_new

        gs = slice(g * gw, (g + 1) * gw)
        y = y + xc[:, gs] * dsk_ref[:, gs]
        y = y * _silu(z_ref[:, gs])
        y = y * lax.rsqrt(jnp.mean(y * y, axis=1, keepdims=True) + EPS) * ng_ref[:, gs]
        y_ref[:, gs] = y.astype(y_ref.dtype)

        @pl.when(ci == nci - 1)
        def _():
            snew_ref[0, g * hpg:(g + 1) * hpg] = st_new.T.reshape(hpg, C_HEAD_DIM, D_STATE)


def _ssd_call(p, cprev, sprev, cw, cb, dtb, alog, dsk, ng, *, batch, seq, cs, row0):
    nci = seq // cs
    rb0 = row0 // cs

    def pcol(width, off):
        return pl.BlockSpec((cs, width), lambda b, i: (rb0 + b * nci + i, off // width))

    def row(width, rows=1):
        return pl.BlockSpec((rows, width), lambda b, i: (0, 0))

    kern = functools.partial(_ssd_kernel, cs=cs)
    return pl.pallas_call(
        kern,
        grid=(batch, nci),
        in_specs=[pcol(C_D_INNER, COL_Z), pcol(C_CONV_DIM, COL_XBC), pcol(LANES, COL_MISC),
                  pl.BlockSpec((1, CONV_WIDTH - 1, C_CONV_DIM), lambda b, i: (b, 0, 0)),
                  pl.BlockSpec((1, C_HEADS, C_HEAD_DIM, D_STATE), lambda b, i: (b, 0, 0, 0)),
                  row(C_CONV_DIM, CONV_WIDTH), row(C_CONV_DIM), row(LANES), row(LANES),
                  row(C_D_INNER), row(C_D_INNER)],
        out_specs=[pl.BlockSpec((cs, C_D_INNER), lambda b, i: (b * nci + i, 0)),
                   pl.BlockSpec((1, C_HEADS, C_HEAD_DIM, D_STATE), lambda b, i: (b, 0, 0, 0))],
        out_shape=[jax.ShapeDtypeStruct((batch * seq, C_D_INNER), BF16),
                   jax.ShapeDtypeStruct((batch, C_HEADS, C_HEAD_DIM, D_STATE), F32)],
        scratch_shapes=[pltpu.VMEM((cs + 8, C_CONV_DIM), F32),
                        pltpu.VMEM((C_GROUPS, D_STATE, C_HEADS // C_GROUPS * C_HEAD_DIM), F32)],
        compiler_params=_params(("parallel", "arbitrary")),
        name="ssd_mixer",
    )(p, p, p, cprev, sprev, cw, cb, dtb, alog, dsk, ng)


def _outproj_kernel(x_ref, ya_ref, yb_ref, yc_ref, wa_ref, wb_ref, wc_ref, o_ref):
    o_ref[...] = (x_ref[...] + _dot(ya_ref[...], wa_ref[...]) + _dot(yb_ref[...], wb_ref[...])
                  + _dot(yc_ref[...], wc_ref[...]))


def _outproj(x, ya, yb, yc, w):
    t, d = x.shape
    tm = _tile(t, 256)

    def tok(width):
        return pl.BlockSpec((tm, width), lambda i: (i, 0))

    return pl.pallas_call(
        _outproj_kernel,
        grid=(t // tm,),
        in_specs=[tok(d), tok(A_WIDTH), tok(B_WIDTH), tok(C_D_INNER),
                  pl.BlockSpec((A_WIDTH, d), lambda i: (0, 0)),
                  pl.BlockSpec((B_WIDTH, d), lambda i: (1, 0)),
                  pl.BlockSpec((C_D_INNER, d), lambda i: (1, 0))],
        out_specs=tok(d),
        out_shape=jax.ShapeDtypeStruct((t, d), F32),
        compiler_params=_params(("parallel",)),
        name="out_proj",
    )(x, ya, yb, yc, w, w, w)


def _ffn_kernel(x_ref, g_ref, wg_ref, wu_ref, wd_ref, o_ref, h_ref):
    @pl.when(pl.program_id(1) == 0)
    def _():
        x = x_ref[...]
        ms = jnp.mean(x * x, axis=-1, keepdims=True)
        h_ref[...] = (x * lax.rsqrt(ms + EPS) * g_ref[...]).astype(BF16)
        o_ref[...] = x

    h = h_ref[...]
    act = _silu(_dot(h, wg_ref[...])) * _dot(h, wu_ref[...])
    o_ref[...] += _dot(act.astype(BF16), wd_ref[...])


def _ffn(x, g, wg, wu, wd):
    t, d = x.shape
    f = wg.shape[1]
    tm = _tile(t, 512)
    tf = 512
    return pl.pallas_call(
        _ffn_kernel,
        grid=(t // tm, f // tf),
        in_specs=[pl.BlockSpec((tm, d), lambda i, j: (i, 0)),
                  pl.BlockSpec((1, d), lambda i, j: (0, 0)),
                  pl.BlockSpec((d, tf), lambda i, j: (0, j)),
                  pl.BlockSpec((d, tf), lambda i, j: (0, j)),
                  pl.BlockSpec((tf, d), lambda i, j: (j, 0))],
        out_specs=pl.BlockSpec((tm, d), lambda i, j: (i, 0)),
        out_shape=jax.ShapeDtypeStruct((t, d), F32),
        scratch_shapes=[pltpu.VMEM((tm, d), BF16)],
        compiler_params=_params(("parallel", "arbitrary")),
        name="swiglu",
    )(x, g, wg, wu, wd)


def _rope_rows(pos):
    half = ROPE_DIM // 2
    inv = 1.0 / (ROPE_THETA ** (jnp.arange(0, ROPE_DIM, 2, dtype=F32) / ROPE_DIM))
    ang = pos.astype(F32)[:, None] * inv[None, :]
    cos, sin = jnp.cos(ang), jnp.sin(ang)
    n = pos.shape[0]
    one = jnp.ones((n, HEAD_DIM - ROPE_DIM), F32)
    zero = jnp.zeros((n, HEAD_DIM - ROPE_DIM), F32)
    zh = jnp.zeros((n, half), F32)
    rc = jnp.concatenate([cos, cos, one], axis=1)
    rs1 = jnp.concatenate([zh, sin, zero], axis=1)
    rs2 = jnp.concatenate([-sin, zh, zero], axis=1)
    return tuple(jnp.tile(a, (1, LANES // HEAD_DIM)) for a in (rc, rs1, rs2))


def _rearrange_w_in(w):
    offs = [0]
    for n in IN_SIZES:
        offs.append(offs[-1] + n)
    (a_q, a_k, a_v, i_q, i_k, i_w, b_q, b_k, b_v, c_z, c_xbc, c_dt) = [
        w[:, offs[i]:offs[i + 1]] for i in range(len(IN_SIZES))]
    used = COL_MISC + IDX_DIM + IDX_HEADS + C_HEADS
    pad = jnp.zeros((w.shape[0], P_COLS - used), w.dtype)
    return jnp.concatenate([c_xbc, i_q, c_z, a_q, b_q, b_k, b_v, a_k, a_v, i_k, i_w, c_dt, pad], axis=1).astype(BF16)


def _lane_row(v, off):
    return jnp.zeros((1, LANES), F32).at[0, off:off + v.shape[0]].set(v)


def _gain_row(g):
    return jnp.tile(g, LANES // HEAD_DIM)[None, :]


def kernel(x_prompt, x_sample, cache_a_k, cache_a_v, cache_a_idx_k, cache_b_k, cache_b_v, state_c_ssm, state_c_conv, norm1_g, w_in, a_q_norm_g, a_k_norm_g, idx_k_norm_g, b_q_norm_g, b_k_norm_g, c_conv_w, c_conv_b, c_dt_bias, c_a_log, c_d, c_norm_g, w_out, norm2_g, w_gate, w_up, w_down):
    bp, sp, d = x_prompt.shape
    bs, ss, _ = x_sample.shape
    depth = w_in.shape[0]
    past = cache_a_k.shape[2]
    tp, ts = bp * sp, bs * ss
    tq_p = 128
    cs_p = 128 if sp % 128 == 0 else sp
    assert ss >= CONV_WIDTH - 1 and sp >= CONV_WIDTH - 1

    pos = jnp.concatenate([jnp.tile(jnp.arange(sp, dtype=I32), bp),
                           jnp.tile(past + jnp.arange(ss, dtype=I32), bs)])
    rc, rs1, rs2 = _rope_rows(pos)
    x = jnp.concatenate([x_prompt.reshape(tp, d), x_sample.reshape(ts, d)], axis=0)
    zero_conv = jnp.zeros((bp, CONV_WIDTH - 1, C_CONV_DIM), F32)
    zero_ssm = jnp.zeros((bp, C_HEADS, C_HEAD_DIM, D_STATE), F32)

    p_states, s_states = [], []
    for l in range(depth):
        p = _proj(x, norm1_g[l][None, :], _rearrange_w_in(w_in[l]))
        gik = jnp.concatenate([idx_k_norm_g[l], jnp.ones((LANES - IDX_DIM,), F32)])[None, :]
        (aq, ak, iq, ik, iw, bq, bk, bk16, bv16) = _post(
            p, rc, rs1, rs2, _gain_row(a_q_norm_g[l]), _gain_row(a_k_norm_g[l]), gik,
            _gain_row(b_q_norm_g[l]), _gain_row(b_k_norm_g[l]))

        a_caches = (cache_a_idx_k[l], cache_a_k[l].reshape(bs, past, LANES), cache_a_v[l].reshape(bs, past, LANES))
        ya_p = _dsa_call(aq, iq, iw, ik, ak, p, None, batch=bp, seq_q=sp, tq=tq_p, row0=0)
        ya_s = _dsa_call(aq, iq, iw, ik, ak, p, a_caches, batch=bs, seq_q=ss, tq=ss, row0=tp)
        b_caches = (cache_b_k[l].reshape(bs, past, B_WIDTH), cache_b_v[l].reshape(bs, past, B_WIDTH))
        yb_p = _sb_call(bq, bk16, bv16, None, batch=bp, seq_q=sp, tq=tq_p, row0=0)
        yb_s = _sb_call(bq, bk16, bv16, b_caches, batch=bs, seq_q=ss, tq=ss, row0=tp)
        cw, cb = c_conv_w[l], c_conv_b[l][None, :]
        dtb, alog = _lane_row(c_dt_bias[l], MISC_DT), _lane_row(c_a_log[l], MISC_DT)
        dsk, ng = jnp.repeat(c_d[l], C_HEAD_DIM)[None, :], c_norm_g[l][None, :]
        yc_p, ssm_p = _ssd_call(p, zero_conv, zero_ssm, cw, cb, dtb, alog, dsk, ng,
                                batch=bp, seq=sp, cs=cs_p, row0=0)
        yc_s, ssm_s = _ssd_call(p, state_c_conv[l], state_c_ssm[l], cw, cb, dtb, alog, dsk, ng,
                                batch=bs, seq=ss, cs=ss, row0=tp)

        x = _outproj(x, jnp.concatenate([ya_p, ya_s]), jnp.concatenate([yb_p, yb_s]),
                     jnp.concatenate([yc_p, yc_s]), w_out[l].astype(BF16))
        x = _ffn(x, norm2_g[l][None, :], w_gate[l].astype(BF16), w_up[l].astype(BF16), w_down[l].astype(BF16))

        av = p[:, COL_AV:COL_AV + LANES]
        bv = p[:, COL_BV:COL_BV + B_WIDTH]
        xbc = p[:, COL_XBC:COL_XBC + C_CONV_DIM]

        def state(lo, hi, b, s, ssm):
            return (ak[lo:hi].reshape(b, s, A_KV_HEADS, HEAD_DIM), av[lo:hi].reshape(b, s, A_KV_HEADS, HEAD_DIM),
                    ik[lo:hi].reshape(b, s, IDX_DIM), bk[lo:hi].reshape(b, s, B_HEADS, HEAD_DIM),
                    bv[lo:hi].reshape(b, s, B_HEADS, HEAD_DIM), ssm,
                    xbc[lo:hi].reshape(b, s, C_CONV_DIM)[:, s - (CONV_WIDTH - 1):])

        p_states.append(state(0, tp, bp, sp, ssm_p))
        s_states.append(state(tp, tp + ts, bs, ss, ssm_s))

    p_out = [jnp.stack(zs) for zs in zip(*p_states)]
    s_out = [jnp.stack(zs) for zs in zip(*s_states)]
    return (x[:tp].reshape(bp, sp, d), x[tp:].reshape(bs, ss, d), *p_out, *s_out)
```

```python
import functools

import jax
import jax.numpy as jnp
from jax import lax
from jax.experimental import pallas as pl
from jax.experimental.pallas import tpu as pltpu

F32 = jnp.float32
BF16 = jnp.bfloat16
I32 = jnp.int32

D_MODEL = 2048
EPS = 1e-6
CHUNK = 64
HEAD_DIM = 64
ROPE_DIM = HEAD_DIM // 4
ROPE_THETA = 500000.0
A_HEADS = 8
A_KV_HEADS = 2
A_WIDTH = A_HEADS * HEAD_DIM
IDX_HEADS = 16
IDX_DIM = 64
TOPK_MAX = 256
B_HEADS = 8
B_WIDTH = B_HEADS * HEAD_DIM
C_D_INNER = D_MODEL // 2
C_HEAD_DIM = 64
C_HEADS = C_D_INNER // C_HEAD_DIM
C_GROUPS = 4
D_STATE = 128
CONV_WIDTH = 4
C_CONV_DIM = C_D_INNER + 2 * C_GROUPS * D_STATE
IN_SIZES = (A_WIDTH, A_KV_HEADS * HEAD_DIM, A_KV_HEADS * HEAD_DIM, IDX_HEADS * IDX_DIM, IDX_DIM, IDX_HEADS,
            B_WIDTH, B_WIDTH, B_WIDTH, C_D_INNER, C_CONV_DIM, C_HEADS)

LANES = 128
VMEM_LIMIT = 48 * 1024 * 1024

COL_XBC, COL_IQ, COL_Z, COL_AQ, COL_BQ, COL_BK, COL_BV, COL_AK, COL_AV, COL_MISC = (
    0, 2048, 3072, 4096, 4608, 5120, 5632, 6144, 6272, 6400)
P_COLS = 6656
MISC_IK, MISC_IW, MISC_DT = 0, 64, 80

NEG_BIG = -1e30
INT_MIN = -2 ** 31
KEY_POS_INF = 0x7F800000
KEY_NEG_INF = INT_MIN + 0x7FFFFF
SB_DEAD = -104.0


def _params(sem):
    return pltpu.CompilerParams(dimension_semantics=sem, vmem_limit_bytes=VMEM_LIMIT)


def _tile(n, pref):
    t = min(n, pref)
    while n % t:
        t -= 8
    return t


def _dot_nt(a, b):
    return lax.dot_general(a, b, (((1,), (1,)), ((), ())), preferred_element_type=F32)


def _dot_tn(a, b):
    return lax.dot_general(a, b, (((0,), (0,)), ((), ())), preferred_element_type=F32)


def _dot(a, b):
    return jnp.dot(a, b, preferred_element_type=F32)


def _split3(x):
    h1 = x.astype(BF16)
    r1 = x - h1.astype(F32)
    h2 = r1.astype(BF16)
    h3 = (r1 - h2.astype(F32)).astype(BF16)
    return h1, h2, h3


def _softplus(x):
    return jnp.maximum(x, 0.0) + jnp.log1p(jnp.exp(-jnp.abs(x)))


def _silu(x):
    return x * (1.0 / (1.0 + jnp.exp(-x)))


def _proj_kernel(x_ref, g_ref, w_ref, o_ref, h_ref):
    @pl.when(pl.program_id(1) == 0)
    def _():
        x = x_ref[...]
        ms = jnp.mean(x * x, axis=-1, keepdims=True)
        h_ref[...] = (x * lax.rsqrt(ms + EPS) * g_ref[...]).astype(BF16)

    o_ref[...] = _dot(h_ref[...], w_ref[...])


def _proj(x, g, w):
    t, d = x.shape
    n = w.shape[1]
    tm = _tile(t, 1024)
    tn = 512
    return pl.pallas_call(
        _proj_kernel,
        grid=(t // tm, n // tn),
        in_specs=[pl.BlockSpec((tm, d), lambda i, j: (i, 0)),
                  pl.BlockSpec((1, d), lambda i, j: (0, 0)),
                  pl.BlockSpec((d, tn), lambda i, j: (0, j))],
        out_specs=pl.BlockSpec((tm, tn), lambda i, j: (i, j)),
        out_shape=jax.ShapeDtypeStruct((t, n), F32),
        scratch_shapes=[pltpu.VMEM((tm, d), BF16)],
        compiler_params=_params(("parallel", "arbitrary")),
        name="in_proj",
    )(x, g, w)


def _post_kernel(aq_r, bq_r, bk_r, bv_r, iq_r, ak_r, misc_r, rc_r, rs1_r, rs2_r,
                 gaq_r, gak_r, gik_r, gbq_r, gbk_r,
                 aq_o, ak_o, iq_o, ik_o, iw_o, bq_o, bk_o, bk16_o, bv16_o):
    r = lax.broadcasted_iota(I32, (LANES, LANES), 0) // HEAD_DIM
    c = lax.broadcasted_iota(I32, (LANES, LANES), 1) // HEAD_DIM
    segm = jnp.where(r == c, 1.0, 0.0).astype(BF16)
    rc, rs1, rs2 = rc_r[...], rs1_r[...], rs2_r[...]

    def headnorm(xc, g_row):
        h1, h2, h3 = _split3(xc * xc)
        ss = _dot(h1, segm) + _dot(h2, segm) + _dot(h3, segm)
        return xc * lax.rsqrt(ss * (1.0 / HEAD_DIM) + EPS) * g_row

    def rope(xc):
        return xc * rc + pltpu.roll(xc, 8, 1) * rs1 + pltpu.roll(xc, LANES - 8, 1) * rs2

    for k in range(A_WIDTH // LANES):
        sl = slice(k * LANES, (k + 1) * LANES)
        aq_o[:, sl] = (rope(headnorm(aq_r[:, sl], gaq_r[...])) * (HEAD_DIM ** -0.5)).astype(BF16)
        bq_o[:, sl] = (headnorm(bq_r[:, sl], gbq_r[...]) * (HEAD_DIM ** -0.5)).astype(BF16)
        bk = headnorm(bk_r[:, sl], gbk_r[...])
        bk_o[:, sl] = bk
        bk16_o[:, sl] = bk.astype(BF16)
        bv16_o[:, sl] = bv_r[:, sl].astype(BF16)
    for k in range(IDX_HEADS * IDX_DIM // LANES):
        sl = slice(k * LANES, (k + 1) * LANES)
        iq_o[:, sl] = rope(iq_r[:, sl]).astype(BF16)
    ak_o[...] = rope(headnorm(ak_r[...], gak_r[...]))
    misc = misc_r[...]
    ik_o[...] = rope(headnorm(misc, gik_r[...]))[:, MISC_IK:MISC_IK + IDX_DIM]
    iw_o[...] = misc[:, MISC_IW:MISC_IW + IDX_HEADS] * (IDX_HEADS ** -0.5 * IDX_DIM ** -0.5)


def _post(p, rc, rs1, rs2, gaq, gak, gik, gbq, gbk):
    t = p.shape[0]
    tm = _tile(t, 256)

    def col(width, off):
        return pl.BlockSpec((tm, width), lambda i: (i, off // width))

    def row(width):
        return pl.BlockSpec((1, width), lambda i: (0, 0))

    def out(width):
        return pl.BlockSpec((tm, width), lambda i: (i, 0))

    widths_dtypes = [(A_WIDTH, BF16), (LANES, F32), (IDX_HEADS * IDX_DIM, BF16), (IDX_DIM, F32),
                     (IDX_HEADS, F32), (B_WIDTH, BF16), (B_WIDTH, F32), (B_WIDTH, BF16), (B_WIDTH, BF16)]
    return pl.pallas_call(
        _post_kernel,
        grid=(t // tm,),
        in_specs=[col(A_WIDTH, COL_AQ), col(B_WIDTH, COL_BQ), col(B_WIDTH, COL_BK), col(B_WIDTH, COL_BV),
                  col(IDX_HEADS * IDX_DIM, COL_IQ), col(LANES, COL_AK), col(LANES, COL_MISC),
                  out(LANES), out(LANES), out(LANES),
                  row(LANES), row(LANES), row(LANES), row(LANES), row(LANES)],
        out_specs=[out(w) for w, _ in widths_dtypes],
        out_shape=[jax.ShapeDtypeStruct((t, w), dt) for w, dt in widths_dtypes],
        compiler_params=_params(("parallel",)),
        name="head_post",
    )(p, p, p, p, p, p, p, rc, rs1, rs2, gaq, gak, gik, gbq, gbk)


def _dsa_kernel(*refs, tq, sn, lc, topk, kblk, has_cache):
    if has_cache:
        (aq_ref, iq_ref, iw_ref, ikn_ref, akn_ref, avn_ref, ikc_ref, akc_ref, avc_ref, o_ref,
         key_ref, bias_ref, ik16_ref, ak16_ref, vt_ref, iqh_ref, aqh_ref, m_ref, acc_ref) = refs
    else:
        (aq_ref, iq_ref, iw_ref, ikn_ref, akn_ref, avn_ref, o_ref,
         key_ref, bias_ref, ik16_ref, ak16_ref, vt_ref, iqh_ref, aqh_ref, m_ref, acc_ref) = refs
    qi = pl.program_id(1)
    n_keys = lc + sn
    lp = ik16_ref.shape[0]
    rep = A_HEADS // A_KV_HEADS

    @pl.when(qi == 0)
    def _():
        def put(rows, ik, ak, av):
            n = ik.shape[0]
            ik16_ref[rows:rows + n, :] = ik.astype(BF16)
            ak16_ref[rows:rows + n, :] = ak.astype(BF16)
            avt = av.T.astype(BF16)
            for c in range(n // kblk):
                for g in range(A_KV_HEADS):
                    vt_ref[g, (rows // kblk) + c, 0:HEAD_DIM, :] = avt[g * HEAD_DIM:(g + 1) * HEAD_DIM,
                                                                      c * kblk:(c + 1) * kblk]

        step = 2 * kblk
        if has_cache:
            for r0 in range(0, lc, step):
                put(r0, ikc_ref[0, r0:r0 + step, :], akc_ref[0, r0:r0 + step, :], avc_ref[0, r0:r0 + step, :])
        for r0 in range(0, sn - sn % step, step):
            put(lc + r0, ikn_ref[r0:r0 + step, :], akn_ref[r0:r0 + step, :], avn_ref[r0:r0 + step, :])
        rem = sn % step
        if lp > lc + sn - rem:
            n = lp - (lc + sn - rem)

            def padded(ref):
                w = ref.shape[-1]
                if rem == 0:
                    return jnp.zeros((n, w), F32)
                return jnp.concatenate([ref[sn - rem:sn, :], jnp.zeros((n - rem, w), F32)], axis=0)

            put(lc + sn - rem, padded(ikn_ref), padded(akn_ref), padded(avn_ref))
        for g in range(A_KV_HEADS):
            vt_ref[g, :, HEAD_DIM:LANES, :] = jnp.ones((lp // kblk, LANES - HEAD_DIM, kblk), BF16)

    qstart = lc + qi * tq
    kmax = jnp.minimum(n_keys, ((qstart + tq - 1) // CHUNK + 1) * CHUNK)
    nkb = (kmax + kblk - 1) // kblk
    qpos = qstart + lax.broadcasted_iota(I32, (1, LANES), 1)
    krow = lax.broadcasted_iota(I32, (kblk, 1), 0)

    def head_major(dst, src, nheads):
        for h in range(nheads):
            x = src[:, h * HEAD_DIM:(h + 1) * HEAD_DIM]
            if tq < LANES:
                x = jnp.concatenate([x, jnp.zeros((LANES - tq, HEAD_DIM), x.dtype)], axis=0)
            dst[h * LANES:(h + 1) * LANES, :] = x

    head_major(iqh_ref, iq_ref, IDX_HEADS)
    head_major(aqh_ref, aq_ref, A_HEADS)
    iw = jnp.concatenate([iw_ref[...], jnp.zeros((tq, LANES - IDX_HEADS), F32)], axis=1)
    if tq < LANES:
        iw = jnp.concatenate([iw, jnp.zeros((LANES - tq, LANES), F32)], axis=0)
    iw_t = iw.T

    def score_body(kb, carry):
        start = pl.multiple_of(kb * kblk, kblk)
        ikb = ik16_ref[pl.ds(start, kblk), :]
        acc = jnp.zeros((kblk, LANES), F32)
        for pr in range(IDX_HEADS // 2):
            logit = _dot_nt(ikb, iqh_ref[pr * 2 * LANES:(pr + 1) * 2 * LANES, :])
            for e in range(2):
                h = 2 * pr + e
                acc = acc + iw_t[h:h + 1, :] * jnp.maximum(logit[:, e * LANES:(e + 1) * LANES], 0.0)
        kpos = start + krow
        adm = jnp.logical_and(kpos // CHUNK <= qpos // CHUNK, kpos < n_keys)
        bits = pltpu.bitcast(jnp.where(adm, acc, -jnp.inf), I32)
        bits = jnp.where(bits == INT_MIN, 0, bits)
        key_ref[kb] = bits ^ ((bits >> 31) & 0x7FFFFFFF)
        return carry

    lax.fori_loop(0, nkb, score_body, 0)

    def count(pred):
        def body(kb, acc):
            hit = jnp.where(pred(key_ref[kb], kb), 1.0, 0.0)
            return acc + jnp.sum(hit.reshape(kblk // 64, 64, LANES), axis=0)
        acc = lax.fori_loop(0, nkb, body, jnp.zeros((64, LANES), F32))
        return jnp.sum(acc, axis=0, keepdims=True)

    kf = float(topk)

    def bit_body(i, thr):
        cand = thr + (jnp.int32(1) << (31 - i))
        cnt = count(lambda key, kb: key >= cand)
        return jnp.where(cnt >= kf, cand, thr)

    thr = lax.fori_loop(0, 32, bit_body, jnp.full((1, LANES), INT_MIN, I32))
    n_gt = count(lambda key, kb: key > thr)
    n_ge = count(lambda key, kb: key >= thr)
    need = kf - n_gt

    def tie_search():
        def jbody(i, jl):
            cand = jl + (jnp.int32(1) << (14 - i))
            cnt = count(lambda key, kb: jnp.logical_and(key == thr, kb * kblk + krow < cand))
            return jnp.where(cnt <= need, cand, jl)
        return lax.fori_loop(0, 15, jbody, jnp.zeros((1, LANES), I32))

    jlim = lax.cond(jnp.max(n_ge) > kf, tie_search, lambda: jnp.full((1, LANES), 2 ** 30, I32))

    def bias_body(kb, carry):
        key = key_ref[kb]
        kpos = kb * kblk + krow
        sel = jnp.logical_or(key > thr, jnp.logical_and(key == thr, kpos < jlim))
        sel = jnp.logical_and(sel, jnp.logical_and(key > KEY_NEG_INF, key < KEY_POS_INF))
        bias_ref[kb] = jnp.where(sel, 0.0, NEG_BIG)
        return carry

    lax.fori_loop(0, nkb, bias_body, 0)

    m_ref[...] = jnp.full(m_ref.shape, -3e38, F32)
    acc_ref[...] = jnp.zeros(acc_ref.shape, F32)

    def att_body(kb, carry):
        start = pl.multiple_of(kb * kblk, kblk)
        bias = bias_ref[kb]
        m_all = m_ref[...]
        kk = ak16_ref[pl.ds(start, kblk), :]
        s_ts = [_dot_nt(kk[:, (h0 // rep) * HEAD_DIM:(h0 // rep + 1) * HEAD_DIM],
                        aqh_ref[h0 * LANES:(h0 + 2) * LANES, :]) for h0 in range(0, A_HEADS, 2)]
        ps, alphas, m_rows = [], [], []
        for h in range(A_HEADS):
            x = s_ts[h // 2][:, (h % 2) * LANES:(h % 2 + 1) * LANES] + bias
            part = jnp.max(x.reshape(kblk // 64, 64, LANES), axis=0)
            bmax = jnp.max(part, axis=0, keepdims=True)
            m_old = m_all[h:h + 1, :]
            m_new = jnp.maximum(m_old, bmax)
            alphas.append(jnp.exp(m_old - m_new))
            ps.append(jnp.exp(x - m_new).astype(BF16))
            m_rows.append(m_new)
        m_ref[...] = jnp.concatenate(m_rows, axis=0)
        upds = [_dot(vt_ref[h0 // rep, kb], jnp.concatenate(ps[h0:h0 + 2], axis=1)) for h0 in range(0, A_HEADS, 2)]
        for h in range(A_HEADS):
            acc_ref[h] = acc_ref[h] * alphas[h] + upds[h // 2][:, (h % 2) * LANES:(h % 2 + 1) * LANES]
        return carry

    lax.fori_loop(0, nkb, att_body, 0)

    outs = []
    for h in range(A_HEADS):
        a = acc_ref[h]
        outs.append(a[0:HEAD_DIM, :] / a[HEAD_DIM:LANES, :])
    o_t = jnp.concatenate(outs, axis=0)
    o_ref[...] = o_t.T[0:tq, :].astype(o_ref.dtype)


def _dsa_call(aq, iq, iw, ik, ak, p, caches, *, batch, seq_q, tq, row0, kblk=256):
    has_cache = caches is not None
    lc = caches[0].shape[1] if has_cache else 0
    sn = seq_q
    n_keys = lc + sn
    lp = -(-n_keys // kblk) * kblk
    assert lc % (2 * kblk) == 0 and row0 % sn == 0 and sn % tq == 0 and tq % 16 == 0
    nqb = seq_q // tq
    rb0 = row0 // tq
    nb0 = row0 // sn
    topk = min(TOPK_MAX, n_keys // 4)
    nkb_max = lp // kblk

    def qspec(width):
        return pl.BlockSpec((tq, width), lambda b, i: (rb0 + b * nqb + i, 0))

    def nspec(width, colblk=0):
        return pl.BlockSpec((sn, width), lambda b, i: (nb0 + b, colblk))

    def cspec(width):
        return pl.BlockSpec((1, lc, width), lambda b, i: (b, 0, 0))

    in_specs = [qspec(A_WIDTH), qspec(IDX_HEADS * IDX_DIM), qspec(IDX_HEADS),
                nspec(IDX_DIM), nspec(LANES), nspec(LANES, COL_AV // LANES)]
    args = [aq, iq, iw, ik, ak, p]
    if has_cache:
        in_specs += [cspec(IDX_DIM), cspec(LANES), cspec(LANES)]
        args += list(caches)
    kern = functools.partial(_dsa_kernel, tq=tq, sn=sn, lc=lc, topk=topk, kblk=kblk, has_cache=has_cache)
    return pl.pallas_call(
        kern,
        grid=(batch, nqb),
        in_specs=in_specs,
        out_specs=pl.BlockSpec((tq, A_WIDTH), lambda b, i: (b * nqb + i, 0)),
        out_shape=jax.ShapeDtypeStruct((batch * seq_q, A_WIDTH), BF16),
        scratch_shapes=[pltpu.VMEM((nkb_max, kblk, LANES), I32),
                        pltpu.VMEM((nkb_max, kblk, LANES), F32),
                        pltpu.VMEM((lp, IDX_DIM), BF16),
                        pltpu.VMEM((lp, LANES), BF16),
                        pltpu.VMEM((A_KV_HEADS, nkb_max, LANES, kblk), BF16),
                        pltpu.VMEM((IDX_HEADS * LANES, IDX_DIM), BF16),
                        pltpu.VMEM((A_HEADS * LANES, HEAD_DIM), BF16),
                        pltpu.VMEM((A_HEADS, LANES), F32),
                        pltpu.VMEM((A_HEADS, LANES, LANES), F32)],
        compiler_params=_params(("parallel", "arbitrary")),
        name="dsa_attention",
    )(*args)


def _sb_kernel(*refs, tq, sn, lc, kblk, has_cache):
    if has_cache:
        q_ref, kn_ref, vn_ref, kc_hbm, vc_hbm, o_ref, qh_ref, carry_ref, acc_ref, kbuf, vbuf, sem = refs
    else:
        q_ref, kn_ref, vn_ref, o_ref, qh_ref, carry_ref, acc_ref = refs
    qlocal = pl.program_id(1) * tq
    qpos = qlocal + lax.broadcasted_iota(I32, (tq, 1), 0)
    kiota = lax.broadcasted_iota(I32, (1, kblk), 1)
    r = lax.broadcasted_iota(I32, (kblk, kblk), 0)
    c = lax.broadcasted_iota(I32, (kblk, kblk), 1)
    upper = jnp.where(r > c, 1.0, 0.0).astype(BF16)
    heads = range(B_HEADS)
    for h in heads:
        qh_ref[h] = q_ref[:, h * HEAD_DIM:(h + 1) * HEAD_DIM]
    carry_ref[...] = jnp.zeros(carry_ref.shape, F32)
    acc_ref[...] = jnp.zeros(acc_ref.shape, F32)

    def visit(k_of, v_of, causal):
        zs = [_dot_nt(qh_ref[h], k_of(h)) for h in heads]
        sps = [_softplus(z) for z in zs]
        lks = [-sp if causal is None else jnp.where(causal, -sp, 0.0) for sp in sps]
        carries = [carry_ref[h] for h in heads]
        laters = []
        for h in heads:
            l1, l2, l3 = _split3(lks[h])
            laters.append(carries[h] + (_dot(l1, upper) + _dot(l2, upper) + _dot(l3, upper)))
        weights = [jnp.exp((zs[h] - sps[h]) + laters[h]) for h in heads]
        if causal is not None:
            weights = [jnp.where(causal, a, 0.0) for a in weights]
        upds = [_dot(weights[h].astype(BF16), v_of(h)) for h in heads]
        new = [carries[h] + jnp.sum(lks[h], axis=1, keepdims=True) for h in heads]
        top = new[0]
        for h in heads:
            acc_ref[h] += upds[h]
            carry_ref[h] = new[h]
            top = jnp.maximum(top, new[h])
        return jnp.max(top) > SB_DEAD

    def alive_cond(state):
        j, alive = state
        return jnp.logical_and(j >= 0, alive)

    def hcols(h):
        return slice(h * HEAD_DIM, (h + 1) * HEAD_DIM)

    if sn >= kblk:
        def new_body(state):
            j, _ = state
            start = pl.multiple_of(j * kblk, kblk)
            alive = visit(lambda h: kn_ref[pl.ds(start, kblk), hcols(h)],
                          lambda h: vn_ref[pl.ds(start, kblk), hcols(h)], start + kiota < qpos)
            return j - 1, alive

        _, alive = lax.while_loop(alive_cond, new_body, ((qlocal + tq - 1) // kblk, True))
    else:
        def padk(ref):
            return lambda h: jnp.concatenate([ref[:, hcols(h)], jnp.zeros((kblk - sn, HEAD_DIM), BF16)], axis=0)

        alive = visit(padk(kn_ref), padk(vn_ref), kiota < qpos)

    if has_cache:
        b = pl.program_id(0)
        rows = B_HEADS * kblk

        def copies(j, slot):
            src = pl.ds(pl.multiple_of(j * rows, rows), rows)
            return (pltpu.make_async_copy(kc_hbm.at[b, src, :], kbuf.at[slot], sem.at[0, slot]),
                    pltpu.make_async_copy(vc_hbm.at[b, src, :], vbuf.at[slot], sem.at[1, slot]))

        def start(j, slot):
            for cp in copies(j, slot):
                cp.start()

        def wait(j, slot):
            for cp in copies(j, slot):
                cp.wait()

        j0 = lc // kblk - 1

        @pl.when(alive)
        def _():
            start(j0, j0 % 2)

        def cache_body(state):
            j, _ = state
            slot = j % 2
            wait(j, slot)

            @pl.when(j > 0)
            def _():
                start(j - 1, 1 - slot)

            alive = visit(lambda h: kbuf[slot, pl.ds(h, kblk, stride=B_HEADS), :].astype(BF16),
                          lambda h: vbuf[slot, pl.ds(h, kblk, stride=B_HEADS), :].astype(BF16), None)
            return j - 1, alive

        j_end, _ = lax.while_loop(alive_cond, cache_body, (j0, alive))

        @pl.when(jnp.logical_and(alive, j_end >= 0))
        def _():
            wait(j_end, j_end % 2)

    for h in heads:
        o_ref[:, hcols(h)] = acc_ref[h].astype(o_ref.dtype)


def _sb_call(bq, bk16, bv16, caches, *, batch, seq_q, tq, row0, kblk=128):
    has_cache = caches is not None
    lc = caches[0].shape[1] // B_HEADS if has_cache else 0
    sn = seq_q
    assert kblk == LANES and lc % kblk == 0 and row0 % sn == 0 and sn % tq == 0
    assert sn % kblk == 0 or (sn < kblk and tq == sn)
    nqb = seq_q // tq
    rb0 = row0 // tq
    nb0 = row0 // sn
    in_specs = [pl.BlockSpec((tq, B_WIDTH), lambda b, i: (rb0 + b * nqb + i, 0)),
                pl.BlockSpec((sn, B_WIDTH), lambda b, i: (nb0 + b, 0)),
                pl.BlockSpec((sn, B_WIDTH), lambda b, i: (nb0 + b, 0))]
    args = [bq, bk16, bv16]
    scratch = [pltpu.VMEM((B_HEADS, tq, HEAD_DIM), BF16), pltpu.VMEM((B_HEADS, tq, LANES), F32),
               pltpu.VMEM((B_HEADS, tq, HEAD_DIM), F32)]
    if has_cache:
        in_specs += [pl.BlockSpec(memory_space=pl.ANY)] * 2
        args += list(caches)
        scratch += [pltpu.VMEM((2, B_HEADS * kblk, HEAD_DIM), F32), pltpu.VMEM((2, B_HEADS * kblk, HEAD_DIM), F32),
                    pltpu.SemaphoreType.DMA((2, 2))]
    kern = functools.partial(_sb_kernel, tq=tq, sn=sn, lc=lc, kblk=kblk, has_cache=has_cache)
    return pl.pallas_call(
        kern,
        grid=(batch, nqb),
        in_specs=in_specs,
        out_specs=pl.BlockSpec((tq, B_WIDTH), lambda b, i: (b * nqb + i, 0)),
        out_shape=jax.ShapeDtypeStruct((batch * seq_q, B_WIDTH), BF16),
        scratch_shapes=scratch,
        compiler_params=_params(("parallel", "arbitrary")),
        name="stick_breaking",
    )(*args)


def _ssd_kernel(z_ref, xbc_ref, misc_ref, cprev_ref, sprev_ref, cw_ref, cb_ref, dtb_ref, alog_ref, dsk_ref,
                ng_ref, y_ref, snew_ref, xpad_ref, st_ref, *, cs):
    ci = pl.program_id(1)
    nci = pl.num_programs(1)
    hpg = C_HEADS // C_GROUPS
    gw = hpg * C_HEAD_DIM

    @pl.when(ci == 0)
    def _():
        xpad_ref[5:8, :] = cprev_ref[0]
        for g in range(C_GROUPS):
            st_ref[g] = sprev_ref[0, g * hpg:(g + 1) * hpg].reshape(gw, D_STATE).T

    @pl.when(ci > 0)
    def _():
        xpad_ref[5:8, :] = xpad_ref[cs + 5:cs + 8, :]

    xpad_ref[8:8 + cs, :] = xbc_ref[...]
    xc = cb_ref[...] + xpad_ref[5:5 + cs, :] * cw_ref[0:1, :]
    for w in range(1, CONV_WIDTH):
        xc = xc + xpad_ref[5 + w:5 + w + cs, :] * cw_ref[w:w + 1, :]
    xc = _silu(xc)

    dt = _softplus(misc_ref[...] + dtb_ref[...])
    a = dt * (-jnp.exp(alog_ref[...]))
    r = lax.broadcasted_iota(I32, (cs, cs), 0)
    c = lax.broadcasted_iota(I32, (cs, cs), 1)
    tril = r >= c
    tril16 = jnp.where(tril, 1.0, 0.0).astype(BF16)
    a1, a2, a3 = _split3(a)
    a_cum = _dot(tril16, a1) + _dot(tril16, a2) + _dot(tril16, a3)
    pad = LANES - cs
    a_sq = a_cum if pad == 0 else jnp.concatenate([a_cum, jnp.zeros((pad, LANES), F32)], axis=0)
    a_cum_t = a_sq.T
    a_last = a_cum[cs - 1:cs, :]
    exp_a = jnp.exp(a_cum)
    dte = jnp.exp(a_last - a_cum)
    exp_last = jnp.exp(a_last)

    for g in range(C_GROUPS):
        bm = xc[:, C_D_INNER + g * D_STATE:C_D_INNER + (g + 1) * D_STATE].astype(BF16)
        cm = xc[:, C_D_INNER + (C_GROUPS + g) * D_STATE:C_D_INNER + (C_GROUPS + g + 1) * D_STATE].astype(BF16)
        cb = _dot_nt(cm, bm)
        st = st_ref[g]
        y_off = _dot(cm, st.astype(BF16))
        y_heads, x_heads, ea_heads, dec_heads = [], [], [], []
        for rr in range(hpg):
            h = g * hpg + rr
            ln = MISC_DT + h
            xs_h = xc[:, h * C_HEAD_DIM:(h + 1) * C_HEAD_DIM]
            xd = xs_h * dt[:, ln:ln + 1]
            seg = a_cum[:, ln:ln + 1] - a_cum_t[ln:ln + 1, 0:cs]
            decay = jnp.where(tril, jnp.exp(jnp.where(tril, seg, 0.0)), 0.0)
            y_heads.append(_dot((cb * decay).astype(BF16), xd.astype(BF16)))
            x_heads.append(xd * dte[:, ln:ln + 1])
            ea_heads.append(jnp.broadcast_to(exp_a[:, ln:ln + 1], (cs, C_HEAD_DIM)))
            dec_heads.append(jnp.broadcast_to(exp_last[:, ln:ln + 1], (1, C_HEAD_DIM)))
        y = jnp.concatenate(y_heads, axis=1) + y_off * jnp.concatenate(ea_heads, axis=1)
        st_new = st * jnp.concatenate(dec_heads, axis=1) + _dot_tn(bm, jnp.concatenate(x_heads, axis=1).astype(BF16))
        st_ref[g] = st_new

        gs = slice(g * gw, (g + 1) * gw)
        y = y + xc[:, gs] * dsk_ref[:, gs]
        y = y * _silu(z_ref[:, gs])
        y = y * lax.rsqrt(jnp.mean(y * y, axis=1, keepdims=True) + EPS) * ng_ref[:, gs]
        y_ref[:, gs] = y.astype(y_ref.dtype)

        @pl.when(ci == nci - 1)
        def _():
            snew_ref[0, g * hpg:(g + 1) * hpg] = st_new.T.reshape(hpg, C_HEAD_DIM, D_STATE)


def _ssd_call(p, cprev, sprev, cw, cb, dtb, alog, dsk, ng, *, batch, seq, cs, row0):
    nci = seq // cs
    rb0 = row0 // cs

    def pcol(width, off):
        return pl.BlockSpec((cs, width), lambda b, i: (rb0 + b * nci + i, off // width))

    def row(width, rows=1):
        return pl.BlockSpec((rows, width), lambda b, i: (0, 0))

    kern = functools.partial(_ssd_kernel, cs=cs)
    return pl.pallas_call(
        kern,
        grid=(batch, nci),
        in_specs=[pcol(C_D_INNER, COL_Z), pcol(C_CONV_DIM, COL_XBC), pcol(LANES, COL_MISC),
                  pl.BlockSpec((1, CONV_WIDTH - 1, C_CONV_DIM), lambda b, i: (b, 0, 0)),
                  pl.BlockSpec((1, C_HEADS, C_HEAD_DIM, D_STATE), lambda b, i: (b, 0, 0, 0)),
                  row(C_CONV_DIM, CONV_WIDTH), row(C_CONV_DIM), row(LANES), row(LANES),
                  row(C_D_INNER), row(C_D_INNER)],
        out_specs=[pl.BlockSpec((cs, C_D_INNER), lambda b, i: (b * nci + i, 0)),
                   pl.BlockSpec((1, C_HEADS, C_HEAD_DIM, D_STATE), lambda b, i: (b, 0, 0, 0))],
        out_shape=[jax.ShapeDtypeStruct((batch * seq, C_D_INNER), BF16),
                   jax.ShapeDtypeStruct((batch, C_HEADS, C_HEAD_DIM, D_STATE), F32)],
        scratch_shapes=[pltpu.VMEM((cs + 8, C_CONV_DIM), F32),
                        pltpu.VMEM((C_GROUPS, D_STATE, C_HEADS // C_GROUPS * C_HEAD_DIM), F32)],
        compiler_params=_params(("parallel", "arbitrary")),
        name="ssd_mixer",
    )(p, p, p, cprev, sprev, cw, cb, dtb, alog, dsk, ng)


def _outproj_kernel(x_ref, ya_ref, yb_ref, yc_ref, wa_ref, wb_ref, wc_ref, o_ref):
    o_ref[...] = (x_ref[...] + _dot(ya_ref[...], wa_ref[...]) + _dot(yb_ref[...], wb_ref[...])
                  + _dot(yc_ref[...], wc_ref[...]))


def _outproj(x, ya, yb, yc, w):
    t, d = x.shape
    tm = _tile(t, 256)

    def tok(width):
        return pl.BlockSpec((tm, width), lambda i: (i, 0))

    return pl.pallas_call(
        _outproj_kernel,
        grid=(t // tm,),
        in_specs=[tok(d), tok(A_WIDTH), tok(B_WIDTH), tok(C_D_INNER),
                  pl.BlockSpec((A_WIDTH, d), lambda i: (0, 0)),
                  pl.BlockSpec((B_WIDTH, d), lambda i: (1, 0)),
                  pl.BlockSpec((C_D_INNER, d), lambda i: (1, 0))],
        out_specs=tok(d),
        out_shape=jax.ShapeDtypeStruct((t, d), F32),
        compiler_params=_params(("parallel",)),
        name="out_proj",
    )(x, ya, yb, yc, w, w, w)


def _ffn_kernel(x_ref, g_ref, wg_ref, wu_ref, wd_ref, o_ref, h_ref):
    @pl.when(pl.program_id(1) == 0)
    def _():
        x = x_ref[...]
        ms = jnp.mean(x * x, axis=-1, keepdims=True)
        h_ref[...] = (x * lax.rsqrt(ms + EPS) * g_ref[...]).astype(BF16)
        o_ref[...] = x

    h = h_ref[...]
    act = _silu(_dot(h, wg_ref[...])) * _dot(h, wu_ref[...])
    o_ref[...] += _dot(act.astype(BF16), wd_ref[...])


def _ffn(x, g, wg, wu, wd):
    t, d = x.shape
    f = wg.shape[1]
    tm = _tile(t, 1024)
    tf = 512
    once = pl.Buffered(1)
    return pl.pallas_call(
        _ffn_kernel,
        grid=(t // tm, f // tf),
        in_specs=[pl.BlockSpec((tm, d), lambda i, j: (i, 0), pipeline_mode=once),
                  pl.BlockSpec((1, d), lambda i, j: (0, 0)),
                  pl.BlockSpec((d, tf), lambda i, j: (0, j)),
                  pl.BlockSpec((d, tf), lambda i, j: (0, j)),
                  pl.BlockSpec((tf, d), lambda i, j: (j, 0))],
        out_specs=pl.BlockSpec((tm, d), lambda i, j: (i, 0), pipeline_mode=once),
        out_shape=jax.ShapeDtypeStruct((t, d), F32),
        scratch_shapes=[pltpu.VMEM((tm, d), BF16)],
        compiler_params=_params(("parallel", "arbitrary")),
        name="swiglu",
    )(x, g, wg, wu, wd)


def _rope_rows(pos):
    half = ROPE_DIM // 2
    inv = 1.0 / (ROPE_THETA ** (jnp.arange(0, ROPE_DIM, 2, dtype=F32) / ROPE_DIM))
    ang = pos.astype(F32)[:, None] * inv[None, :]
    cos, sin = jnp.cos(ang), jnp.sin(ang)
    n = pos.shape[0]
    one = jnp.ones((n, HEAD_DIM - ROPE_DIM), F32)
    zero = jnp.zeros((n, HEAD_DIM - ROPE_DIM), F32)
    zh = jnp.zeros((n, half), F32)
    rc = jnp.concatenate([cos, cos, one], axis=1)
    rs1 = jnp.concatenate([zh, sin, zero], axis=1)
    rs2 = jnp.concatenate([-sin, zh, zero], axis=1)
    return tuple(jnp.tile(a, (1, LANES // HEAD_DIM)) for a in (rc, rs1, rs2))


def _w_in_segments():
    offs = [0]
    for n in IN_SIZES:
        offs.append(offs[-1] + n)
    dst = (COL_AQ, COL_AK, COL_AV, COL_IQ, COL_MISC + MISC_IK, COL_MISC + MISC_IW, COL_BQ, COL_BK, COL_BV,
           COL_Z, COL_XBC, COL_MISC + MISC_DT)
    return [(offs[i], dst[i], IN_SIZES[i]) for i in range(len(IN_SIZES))]


def _regroup_kernel(w_ref, o_ref):
    for src, dst, width in _w_in_segments():
        o_ref[:, dst:dst + width] = w_ref[:, src:src + width].astype(BF16)
    used = COL_MISC + MISC_DT + C_HEADS
    o_ref[:, used:] = jnp.zeros((o_ref.shape[0], P_COLS - used), BF16)


def _rearrange_w_in(w):
    d, n = w.shape
    tr = _tile(d, 256)
    return pl.pallas_call(
        _regroup_kernel,
        grid=(d // tr,),
        in_specs=[pl.BlockSpec((tr, n), lambda i: (i, 0))],
        out_specs=pl.BlockSpec((tr, P_COLS), lambda i: (i, 0)),
        out_shape=jax.ShapeDtypeStruct((d, P_COLS), BF16),
        compiler_params=_params(("parallel",)),
        name="regroup_w_in",
    )(w)


def _lane_row(v, off):
    return jnp.zeros((1, LANES), F32).at[0, off:off + v.shape[0]].set(v)


def _gain_row(g):
    return jnp.tile(g, LANES // HEAD_DIM)[None, :]


def kernel(x_prompt, x_sample, cache_a_k, cache_a_v, cache_a_idx_k, cache_b_k, cache_b_v, state_c_ssm, state_c_conv, norm1_g, w_in, a_q_norm_g, a_k_norm_g, idx_k_norm_g, b_q_norm_g, b_k_norm_g, c_conv_w, c_conv_b, c_dt_bias, c_a_log, c_d, c_norm_g, w_out, norm2_g, w_gate, w_up, w_down):
    bp, sp, d = x_prompt.shape
    bs, ss, _ = x_sample.shape
    depth = w_in.shape[0]
    past = cache_a_k.shape[2]
    tp, ts = bp * sp, bs * ss
    tq_p = 128
    cs_p = 128 if sp % 128 == 0 else sp
    assert ss >= CONV_WIDTH - 1 and sp >= CONV_WIDTH - 1

    pos = jnp.concatenate([jnp.tile(jnp.arange(sp, dtype=I32), bp),
                           jnp.tile(past + jnp.arange(ss, dtype=I32), bs)])
    rc, rs1, rs2 = _rope_rows(pos)
    x = jnp.concatenate([x_prompt.reshape(tp, d), x_sample.reshape(ts, d)], axis=0)
    zero_conv = jnp.zeros((bp, CONV_WIDTH - 1, C_CONV_DIM), F32)
    zero_ssm = jnp.zeros((bp, C_HEADS, C_HEAD_DIM, D_STATE), F32)

    p_states, s_states = [], []
    for l in range(depth):
        p = _proj(x, norm1_g[l][None, :], _rearrange_w_in(w_in[l]))
        gik = jnp.concatenate([idx_k_norm_g[l], jnp.ones((LANES - IDX_DIM,), F32)])[None, :]
        (aq, ak, iq, ik, iw, bq, bk, bk16, bv16) = _post(
            p, rc, rs1, rs2, _gain_row(a_q_norm_g[l]), _gain_row(a_k_norm_g[l]), gik,
            _gain_row(b_q_norm_g[l]), _gain_row(b_k_norm_g[l]))

        a_caches = (cache_a_idx_k[l], cache_a_k[l].reshape(bs, past, LANES), cache_a_v[l].reshape(bs, past, LANES))
        ya_p = _dsa_call(aq, iq, iw, ik, ak, p, None, batch=bp, seq_q=sp, tq=tq_p, row0=0)
        ya_s = _dsa_call(aq, iq, iw, ik, ak, p, a_caches, batch=bs, seq_q=ss, tq=ss, row0=tp)
        b_caches = (cache_b_k[l].reshape(bs, past * B_HEADS, HEAD_DIM),
                    cache_b_v[l].reshape(bs, past * B_HEADS, HEAD_DIM))
        yb_p = _sb_call(bq, bk16, bv16, None, batch=bp, seq_q=sp, tq=tq_p, row0=0)
        yb_s = _sb_call(bq, bk16, bv16, b_caches, batch=bs, seq_q=ss, tq=ss, row0=tp)
        cw, cb = c_conv_w[l], c_conv_b[l][None, :]
        dtb, alog = _lane_row(c_dt_bias[l], MISC_DT), _lane_row(c_a_log[l], MISC_DT)
        dsk, ng = jnp.repeat(c_d[l], C_HEAD_DIM)[None, :], c_norm_g[l][None, :]
        yc_p, ssm_p = _ssd_call(p, zero_conv, zero_ssm, cw, cb, dtb, alog, dsk, ng,
                                batch=bp, seq=sp, cs=cs_p, row0=0)
        yc_s, ssm_s = _ssd_call(p, state_c_conv[l], state_c_ssm[l], cw, cb, dtb, alog, dsk, ng,
                                batch=bs, seq=ss, cs=ss, row0=tp)

        x = _outproj(x, jnp.concatenate([ya_p, ya_s]), jnp.concatenate([yb_p, yb_s]),
                     jnp.concatenate([yc_p, yc_s]), w_out[l].astype(BF16))
        x = _ffn(x, norm2_g[l][None, :], w_gate[l].astype(BF16), w_up[l].astype(BF16), w_down[l].astype(BF16))

        def state(lo, hi, b, s, ssm):
            av = p[lo:hi, COL_AV:COL_AV + LANES]
            bv = p[lo:hi, COL_BV:COL_BV + B_WIDTH]
            conv = jnp.stack([p[lo + (i + 1) * s - (CONV_WIDTH - 1):lo + (i + 1) * s, COL_XBC:COL_XBC + C_CONV_DIM]
                              for i in range(b)])
            return (ak[lo:hi].reshape(b, s, A_KV_HEADS, HEAD_DIM), av.reshape(b, s, A_KV_HEADS, HEAD_DIM),
                    ik[lo:hi].reshape(b, s, IDX_DIM), bk[lo:hi].reshape(b, s, B_HEADS, HEAD_DIM),
                    bv.reshape(b, s, B_HEADS, HEAD_DIM), ssm, conv)

        p_states.append(state(0, tp, bp, sp, ssm_p))
        s_states.append(state(tp, tp + ts, bs, ss, ssm_s))

    p_out = [jnp.stack(zs) for zs in zip(*p_states)]
    s_out = [jnp.stack(zs) for zs in zip(*s_states)]
    return (x[:tp].reshape(bp, sp, d), x[tp:].reshape(bs, ss, d), *p_out, *s_out)
```

```python
import functools
import math

import jax
import jax.numpy as jnp
from jax import lax
from jax.experimental import pallas as pl
from jax.experimental.pallas import tpu as pltpu

F32 = jnp.float32
BF16 = jnp.bfloat16
I32 = jnp.int32

D_MODEL = 2048
EPS = 1e-6
CHUNK = 64
HEAD_DIM = 64
ROPE_DIM = HEAD_DIM // 4
ROPE_THETA = 500000.0
A_HEADS = 8
A_KV_HEADS = 2
A_WIDTH = A_HEADS * HEAD_DIM
IDX_HEADS = 16
IDX_DIM = 64
TOPK_MAX = 256
B_HEADS = 8
B_WIDTH = B_HEADS * HEAD_DIM
C_D_INNER = D_MODEL // 2
C_HEAD_DIM = 64
C_HEADS = C_D_INNER // C_HEAD_DIM
C_GROUPS = 4
D_STATE = 128
CONV_WIDTH = 4
C_CONV_DIM = C_D_INNER + 2 * C_GROUPS * D_STATE
IN_SIZES = (A_WIDTH, A_KV_HEADS * HEAD_DIM, A_KV_HEADS * HEAD_DIM, IDX_HEADS * IDX_DIM, IDX_DIM, IDX_HEADS,
            B_WIDTH, B_WIDTH, B_WIDTH, C_D_INNER, C_CONV_DIM, C_HEADS)

LANES = 128
VMEM_LIMIT = 48 * 1024 * 1024

COL_XBC, COL_IQ, COL_Z, COL_AQ, COL_BQ, COL_BK, COL_BV, COL_AK, COL_AV, COL_MISC = (
    0, 2048, 3072, 4096, 4608, 5120, 5632, 6144, 6272, 6400)
P_COLS = 6656
MISC_IK, MISC_IW, MISC_DT = 0, 64, 80

NEG_BIG = -1e30
INT_MIN = -2 ** 31
KEY_POS_INF = 0x7F800000
KEY_NEG_INF = INT_MIN + 0x7FFFFF
SB_DEAD = -104.0


def _params(sem):
    return pltpu.CompilerParams(dimension_semantics=sem, vmem_limit_bytes=VMEM_LIMIT)


def _tile(n, pref):
    t = min(n, pref)
    while n % t:
        t -= 8
    return t


def _dot_nt(a, b):
    return lax.dot_general(a, b, (((1,), (1,)), ((), ())), preferred_element_type=F32)


def _dot_tn(a, b):
    return lax.dot_general(a, b, (((0,), (0,)), ((), ())), preferred_element_type=F32)


def _dot(a, b):
    return jnp.dot(a, b, preferred_element_type=F32)


def _split3(x):
    h1 = x.astype(BF16)
    r1 = x - h1.astype(F32)
    h2 = r1.astype(BF16)
    h3 = (r1 - h2.astype(F32)).astype(BF16)
    return h1, h2, h3


def _softplus(x):
    return jnp.maximum(x, 0.0) + jnp.log1p(jnp.exp(-jnp.abs(x)))


def _silu(x):
    return x * (1.0 / (1.0 + jnp.exp(-x)))


def _proj_kernel(x_ref, g_ref, w_ref, o_ref, h_ref):
    @pl.when(pl.program_id(1) == 0)
    def _():
        x = x_ref[...]
        ms = jnp.mean(x * x, axis=-1, keepdims=True)
        h_ref[...] = (x * lax.rsqrt(ms + EPS) * g_ref[...]).astype(BF16)

    o_ref[...] = _dot(h_ref[...], w_ref[...])


def _proj(x, g, w):
    t, d = x.shape
    n = w.shape[1]
    tm = _tile(t, 1024)
    tn = 512
    return pl.pallas_call(
        _proj_kernel,
        grid=(t // tm, n // tn),
        in_specs=[pl.BlockSpec((tm, d), lambda i, j: (i, 0)),
                  pl.BlockSpec((1, d), lambda i, j: (0, 0)),
                  pl.BlockSpec((d, tn), lambda i, j: (0, j))],
        out_specs=pl.BlockSpec((tm, tn), lambda i, j: (i, j)),
        out_shape=jax.ShapeDtypeStruct((t, n), F32),
        scratch_shapes=[pltpu.VMEM((tm, d), BF16)],
        compiler_params=_params(("parallel", "arbitrary")),
        name="in_proj",
    )(x, g, w)


def _post_kernel(aq_r, bq_r, bk_r, bv_r, iq_r, ak_r, misc_r, rc_r, rs1_r, rs2_r,
                 gaq_r, gak_r, gik_r, gbq_r, gbk_r,
                 aq_o, ak_o, iq_o, ik_o, iw_o, bq_o, bk_o, bk16_o, bv16_o):
    r = lax.broadcasted_iota(I32, (LANES, LANES), 0) // HEAD_DIM
    c = lax.broadcasted_iota(I32, (LANES, LANES), 1) // HEAD_DIM
    segm = jnp.where(r == c, 1.0, 0.0).astype(BF16)
    rc, rs1, rs2 = rc_r[...], rs1_r[...], rs2_r[...]

    def headnorm(xc, g_row):
        h1, h2, h3 = _split3(xc * xc)
        ss = _dot(h1, segm) + _dot(h2, segm) + _dot(h3, segm)
        return xc * lax.rsqrt(ss * (1.0 / HEAD_DIM) + EPS) * g_row

    def rope(xc):
        return xc * rc + pltpu.roll(xc, 8, 1) * rs1 + pltpu.roll(xc, LANES - 8, 1) * rs2

    for k in range(A_WIDTH // LANES):
        sl = slice(k * LANES, (k + 1) * LANES)
        aq_o[:, sl] = (rope(headnorm(aq_r[:, sl], gaq_r[...])) * (HEAD_DIM ** -0.5)).astype(BF16)
        bq_o[:, sl] = (headnorm(bq_r[:, sl], gbq_r[...]) * (HEAD_DIM ** -0.5)).astype(BF16)
        bk = headnorm(bk_r[:, sl], gbk_r[...])
        bk_o[:, sl] = bk
        bk16_o[:, sl] = bk.astype(BF16)
        bv16_o[:, sl] = bv_r[:, sl].astype(BF16)
    for k in range(IDX_HEADS * IDX_DIM // LANES):
        sl = slice(k * LANES, (k + 1) * LANES)
        iq_o[:, sl] = rope(iq_r[:, sl]).astype(BF16)
    ak_o[...] = rope(headnorm(ak_r[...], gak_r[...]))
    misc = misc_r[...]
    ik_o[...] = rope(headnorm(misc, gik_r[...]))[:, MISC_IK:MISC_IK + IDX_DIM]
    iw_o[...] = misc[:, MISC_IW:MISC_IW + IDX_HEADS] * (IDX_HEADS ** -0.5 * IDX_DIM ** -0.5)


def _post(p, rc, rs1, rs2, gaq, gak, gik, gbq, gbk):
    t = p.shape[0]
    tm = _tile(t, 256)

    def col(width, off):
        return pl.BlockSpec((tm, width), lambda i: (i, off // width))

    def row(width):
        return pl.BlockSpec((1, width), lambda i: (0, 0))

    def out(width):
        return pl.BlockSpec((tm, width), lambda i: (i, 0))

    widths_dtypes = [(A_WIDTH, BF16), (LANES, F32), (IDX_HEADS * IDX_DIM, BF16), (IDX_DIM, F32),
                     (IDX_HEADS, F32), (B_WIDTH, BF16), (B_WIDTH, F32), (B_WIDTH, BF16), (B_WIDTH, BF16)]
    return pl.pallas_call(
        _post_kernel,
        grid=(t // tm,),
        in_specs=[col(A_WIDTH, COL_AQ), col(B_WIDTH, COL_BQ), col(B_WIDTH, COL_BK), col(B_WIDTH, COL_BV),
                  col(IDX_HEADS * IDX_DIM, COL_IQ), col(LANES, COL_AK), col(LANES, COL_MISC),
                  out(LANES), out(LANES), out(LANES),
                  row(LANES), row(LANES), row(LANES), row(LANES), row(LANES)],
        out_specs=[out(w) for w, _ in widths_dtypes],
        out_shape=[jax.ShapeDtypeStruct((t, w), dt) for w, dt in widths_dtypes],
        compiler_params=_params(("parallel",)),
        name="head_post",
    )(p, p, p, p, p, p, p, rc, rs1, rs2, gaq, gak, gik, gbq, gbk)


def _dsa_kernel(*refs, tq, sn, lc, topk, kblk, has_cache):
    if has_cache:
        (aq_ref, iq_ref, iw_ref, ikn_ref, akn_ref, avn_ref, ikc_ref, akc_ref, avc_ref, o_ref,
         key_ref, bias_ref, ik16_ref, ak16_ref, vt_ref, iqh_ref, aqh_ref, m_ref, acc_ref) = refs
    else:
        (aq_ref, iq_ref, iw_ref, ikn_ref, akn_ref, avn_ref, o_ref,
         key_ref, bias_ref, ik16_ref, ak16_ref, vt_ref, iqh_ref, aqh_ref, m_ref, acc_ref) = refs
    qi = pl.program_id(1)
    n_keys = lc + sn
    lp = ik16_ref.shape[0]
    rep = A_HEADS // A_KV_HEADS

    @pl.when(qi == 0)
    def _():
        def put(rows, ik, ak, av):
            n = ik.shape[0]
            ik16_ref[rows:rows + n, :] = ik.astype(BF16)
            ak16_ref[rows:rows + n, :] = ak.astype(BF16)
            avt = av.T.astype(BF16)
            for c in range(n // kblk):
                for g in range(A_KV_HEADS):
                    vt_ref[g, (rows // kblk) + c, 0:HEAD_DIM, :] = avt[g * HEAD_DIM:(g + 1) * HEAD_DIM,
                                                                      c * kblk:(c + 1) * kblk]

        step = 2 * kblk
        if has_cache:
            for r0 in range(0, lc, step):
                put(r0, ikc_ref[0, r0:r0 + step, :], akc_ref[0, r0:r0 + step, :], avc_ref[0, r0:r0 + step, :])
        for r0 in range(0, sn - sn % step, step):
            put(lc + r0, ikn_ref[r0:r0 + step, :], akn_ref[r0:r0 + step, :], avn_ref[r0:r0 + step, :])
        rem = sn % step
        if lp > lc + sn - rem:
            n = lp - (lc + sn - rem)

            def padded(ref):
                w = ref.shape[-1]
                if rem == 0:
                    return jnp.zeros((n, w), F32)
                return jnp.concatenate([ref[sn - rem:sn, :], jnp.zeros((n - rem, w), F32)], axis=0)

            put(lc + sn - rem, padded(ikn_ref), padded(akn_ref), padded(avn_ref))
        for g in range(A_KV_HEADS):
            vt_ref[g, :, HEAD_DIM:LANES, :] = jnp.ones((lp // kblk, LANES - HEAD_DIM, kblk), BF16)

    qstart = lc + qi * tq
    kmax = jnp.minimum(n_keys, ((qstart + tq - 1) // CHUNK + 1) * CHUNK)
    nkb = (kmax + kblk - 1) // kblk
    qpos = qstart + lax.broadcasted_iota(I32, (1, LANES), 1)
    krow = lax.broadcasted_iota(I32, (kblk, 1), 0)

    def head_major(dst, src, nheads):
        for h in range(nheads):
            x = src[:, h * HEAD_DIM:(h + 1) * HEAD_DIM]
            if tq < LANES:
                x = jnp.concatenate([x, jnp.zeros((LANES - tq, HEAD_DIM), x.dtype)], axis=0)
            dst[h * LANES:(h + 1) * LANES, :] = x

    head_major(iqh_ref, iq_ref, IDX_HEADS)
    head_major(aqh_ref, aq_ref, A_HEADS)
    iw = jnp.concatenate([iw_ref[...], jnp.zeros((tq, LANES - IDX_HEADS), F32)], axis=1)
    if tq < LANES:
        iw = jnp.concatenate([iw, jnp.zeros((LANES - tq, LANES), F32)], axis=0)
    iw_t = iw.T

    def score_body(kb, carry):
        start = pl.multiple_of(kb * kblk, kblk)
        ikb = ik16_ref[pl.ds(start, kblk), :]
        acc = jnp.zeros((kblk, LANES), F32)
        for pr in range(IDX_HEADS // 2):
            logit = _dot_nt(ikb, iqh_ref[pr * 2 * LANES:(pr + 1) * 2 * LANES, :])
            for e in range(2):
                h = 2 * pr + e
                acc = acc + iw_t[h:h + 1, :] * jnp.maximum(logit[:, e * LANES:(e + 1) * LANES], 0.0)
        kpos = start + krow
        adm = jnp.logical_and(kpos // CHUNK <= qpos // CHUNK, kpos < n_keys)
        bits = pltpu.bitcast(jnp.where(adm, acc, -jnp.inf), I32)
        bits = jnp.where(bits == INT_MIN, 0, bits)
        key_ref[kb] = bits ^ ((bits >> 31) & 0x7FFFFFFF)
        return carry

    lax.fori_loop(0, nkb, score_body, 0)

    def count(pred):
        def body(kb, acc):
            hit = jnp.where(pred(key_ref[kb], kb), 1.0, 0.0)
            return acc + jnp.sum(hit.reshape(kblk // 64, 64, LANES), axis=0)
        acc = lax.fori_loop(0, nkb, body, jnp.zeros((64, LANES), F32))
        return jnp.sum(acc, axis=0, keepdims=True)

    kf = float(topk)

    def bit_body(i, thr):
        cand = thr + (jnp.int32(1) << (31 - i))
        cnt = count(lambda key, kb: key >= cand)
        return jnp.where(cnt >= kf, cand, thr)

    thr = lax.fori_loop(0, 32, bit_body, jnp.full((1, LANES), INT_MIN, I32))
    n_gt = count(lambda key, kb: key > thr)
    n_ge = count(lambda key, kb: key >= thr)
    need = kf - n_gt

    def tie_search():
        def jbody(i, jl):
            cand = jl + (jnp.int32(1) << (14 - i))
            cnt = count(lambda key, kb: jnp.logical_and(key == thr, kb * kblk + krow < cand))
            return jnp.where(cnt <= need, cand, jl)
        return lax.fori_loop(0, 15, jbody, jnp.zeros((1, LANES), I32))

    jlim = lax.cond(jnp.max(n_ge) > kf, tie_search, lambda: jnp.full((1, LANES), 2 ** 30, I32))

    def bias_body(kb, carry):
        key = key_ref[kb]
        kpos = kb * kblk + krow
        sel = jnp.logical_or(key > thr, jnp.logical_and(key == thr, kpos < jlim))
        sel = jnp.logical_and(sel, jnp.logical_and(key > KEY_NEG_INF, key < KEY_POS_INF))
        bias_ref[kb] = jnp.where(sel, 0.0, NEG_BIG)
        return carry

    lax.fori_loop(0, nkb, bias_body, 0)

    m_ref[...] = jnp.full(m_ref.shape, -3e38, F32)
    acc_ref[...] = jnp.zeros(acc_ref.shape, F32)

    def att_body(kb, carry):
        start = pl.multiple_of(kb * kblk, kblk)
        bias = bias_ref[kb]
        m_all = m_ref[...]
        kk = ak16_ref[pl.ds(start, kblk), :]
        s_ts = [_dot_nt(kk[:, (h0 // rep) * HEAD_DIM:(h0 // rep + 1) * HEAD_DIM],
                        aqh_ref[h0 * LANES:(h0 + 2) * LANES, :]) for h0 in range(0, A_HEADS, 2)]
        ps, alphas, m_rows = [], [], []
        for h in range(A_HEADS):
            x = s_ts[h // 2][:, (h % 2) * LANES:(h % 2 + 1) * LANES] + bias
            part = jnp.max(x.reshape(kblk // 64, 64, LANES), axis=0)
            bmax = jnp.max(part, axis=0, keepdims=True)
            m_old = m_all[h:h + 1, :]
            m_new = jnp.maximum(m_old, bmax)
            alphas.append(jnp.exp(m_old - m_new))
            ps.append(jnp.exp(x - m_new).astype(BF16))
            m_rows.append(m_new)
        m_ref[...] = jnp.concatenate(m_rows, axis=0)
        upds = [_dot(vt_ref[h0 // rep, kb], jnp.concatenate(ps[h0:h0 + 2], axis=1)) for h0 in range(0, A_HEADS, 2)]
        for h in range(A_HEADS):
            acc_ref[h] = acc_ref[h] * alphas[h] + upds[h // 2][:, (h % 2) * LANES:(h % 2 + 1) * LANES]
        return carry

    lax.fori_loop(0, nkb, att_body, 0)

    outs = []
    for h in range(A_HEADS):
        a = acc_ref[h]
        outs.append(a[0:HEAD_DIM, :] / a[HEAD_DIM:LANES, :])
    o_t = jnp.concatenate(outs, axis=0)
    o_ref[...] = o_t.T[0:tq, :].astype(o_ref.dtype)


def _dsa_call(aq, iq, iw, ik, ak, p, caches, *, batch, seq_q, tq, row0, kblk=256):
    has_cache = caches is not None
    lc = caches[0].shape[1] if has_cache else 0
    sn = seq_q
    n_keys = lc + sn
    lp = -(-n_keys // kblk) * kblk
    assert lc % (2 * kblk) == 0 and row0 % sn == 0 and sn % tq == 0 and tq % 16 == 0
    nqb = seq_q // tq
    rb0 = row0 // tq
    nb0 = row0 // sn
    topk = min(TOPK_MAX, n_keys // 4)
    nkb_max = lp // kblk

    def qspec(width):
        return pl.BlockSpec((tq, width), lambda b, i: (rb0 + b * nqb + i, 0))

    def nspec(width, colblk=0):
        return pl.BlockSpec((sn, width), lambda b, i: (nb0 + b, colblk))

    def cspec(width):
        return pl.BlockSpec((1, lc, width), lambda b, i: (b, 0, 0))

    in_specs = [qspec(A_WIDTH), qspec(IDX_HEADS * IDX_DIM), qspec(IDX_HEADS),
                nspec(IDX_DIM), nspec(LANES), nspec(LANES, COL_AV // LANES)]
    args = [aq, iq, iw, ik, ak, p]
    if has_cache:
        in_specs += [cspec(IDX_DIM), cspec(LANES), cspec(LANES)]
        args += list(caches)
    kern = functools.partial(_dsa_kernel, tq=tq, sn=sn, lc=lc, topk=topk, kblk=kblk, has_cache=has_cache)
    return pl.pallas_call(
        kern,
        grid=(batch, nqb),
        in_specs=in_specs,
        out_specs=pl.BlockSpec((tq, A_WIDTH), lambda b, i: (b * nqb + i, 0)),
        out_shape=jax.ShapeDtypeStruct((batch * seq_q, A_WIDTH), BF16),
        scratch_shapes=[pltpu.VMEM((nkb_max, kblk, LANES), I32),
                        pltpu.VMEM((nkb_max, kblk, LANES), F32),
                        pltpu.VMEM((lp, IDX_DIM), BF16),
                        pltpu.VMEM((lp, LANES), BF16),
                        pltpu.VMEM((A_KV_HEADS, nkb_max, LANES, kblk), BF16),
                        pltpu.VMEM((IDX_HEADS * LANES, IDX_DIM), BF16),
                        pltpu.VMEM((A_HEADS * LANES, HEAD_DIM), BF16),
                        pltpu.VMEM((A_HEADS, LANES), F32),
                        pltpu.VMEM((A_HEADS, LANES, LANES), F32)],
        compiler_params=_params(("parallel", "arbitrary")),
        name="dsa_attention",
    )(*args)


CACHE_CHUNK = 512


def _dsap_kernel(*refs, tq, sn, lc, topk, kblk, has_cache, cache_row0):
    nb = LANES // tq
    if has_cache:
        (aq_ref, iq_ref, iw_ref, ikn_ref, akn_ref, avn_ref, ikc_hbm, akc_hbm, avc_hbm, o_ref,
         key_ref, bias_ref, ik16_ref, ak16_ref, vt_ref, iqh_ref, aqh_ref, m_ref, acc_ref,
         ikbuf, akbuf, avbuf, sem) = refs
    else:
        (aq_ref, iq_ref, iw_ref, ikn_ref, akn_ref, avn_ref, o_ref,
         key_ref, bias_ref, ik16_ref, ak16_ref, vt_ref, iqh_ref, aqh_ref, m_ref, acc_ref) = refs
    grp = pl.program_id(0)
    qi = pl.program_id(1)
    n_keys = lc + sn
    lp = ik16_ref.shape[0]
    rep = A_HEADS // A_KV_HEADS
    ch = CACHE_CHUNK

    @pl.when(qi == 0)
    def _():
        def put(row, j, ik, k0, k1, v2):
            n = ik.shape[0]
            cols = slice(j * IDX_DIM, (j + 1) * IDX_DIM)
            ik16_ref[pl.ds(row, n), cols] = ik.astype(BF16)
            ak16_ref[0, pl.ds(row, n), cols] = k0.astype(BF16)
            ak16_ref[1, pl.ds(row, n), cols] = k1.astype(BF16)
            vt = v2.T.astype(BF16)
            for e in range(n // kblk):
                for g in range(A_KV_HEADS):
                    vt_ref[g, row // kblk + e, 0:HEAD_DIM, j * kblk:(j + 1) * kblk] = (
                        vt[g * HEAD_DIM:(g + 1) * HEAD_DIM, e * kblk:(e + 1) * kblk])

        if has_cache:
            nc = lc // ch

            def copies(j, c, slot):
                b = cache_row0 + grp * nb + j
                return (pltpu.make_async_copy(ikc_hbm.at[b, pl.ds(c * ch, ch), :], ikbuf.at[slot], sem.at[0, slot]),
                        pltpu.make_async_copy(akc_hbm.at[b, pl.ds(c * 2 * ch, 2 * ch), :], akbuf.at[slot],
                                              sem.at[1, slot]),
                        pltpu.make_async_copy(avc_hbm.at[b, pl.ds(c * 2 * ch, 2 * ch), :], avbuf.at[slot],
                                              sem.at[2, slot]))

            def start(j, c, slot):
                for cp in copies(j, c, slot):
                    cp.start()

            start(0, 0, 0)
            for j in range(nb):
                def chunk_body(c, carry, j=j):
                    slot = c % 2
                    for cp in copies(j, c, slot):
                        cp.wait()

                    @pl.when(c + 1 < nc)
                    def _():
                        start(j, c + 1, 1 - slot)

                    if j + 1 < nb:
                        @pl.when(c + 1 == nc)
                        def _():
                            start(j + 1, 0, 0)

                    k0 = akbuf[slot, pl.ds(0, ch, stride=A_KV_HEADS), :]
                    k1 = akbuf[slot, pl.ds(1, ch, stride=A_KV_HEADS), :]
                    v2 = jnp.concatenate([avbuf[slot, pl.ds(0, ch, stride=A_KV_HEADS), :],
                                          avbuf[slot, pl.ds(1, ch, stride=A_KV_HEADS), :]], axis=1)
                    put(pl.multiple_of(c * ch, ch), j, ikbuf[slot], k0, k1, v2)
                    return carry

                lax.fori_loop(0, nc, chunk_body, 0)

        for j in range(nb):
            if sn % ch == 0:
                def new_body(c, carry, j=j):
                    src = pl.ds(pl.multiple_of(j * sn + c * ch, ch), ch)
                    ak = akn_ref[src, :]
                    put(pl.multiple_of(lc + c * ch, ch), j, ikn_ref[src, :], ak[:, 0:HEAD_DIM], ak[:, HEAD_DIM:LANES],
                        avn_ref[src, :])
                    return carry

                lax.fori_loop(0, sn // ch, new_body, 0)
            else:
                def padded(ref):
                    return jnp.concatenate([ref[j * sn:(j + 1) * sn, :],
                                            jnp.zeros((kblk - sn, ref.shape[-1]), F32)], axis=0)

                ak = padded(akn_ref)
                put(lc, j, padded(ikn_ref), ak[:, 0:HEAD_DIM], ak[:, HEAD_DIM:LANES], padded(avn_ref))
        for g in range(A_KV_HEADS):
            vt_ref[g, :, HEAD_DIM:LANES, :] = jnp.ones((lp // kblk, LANES - HEAD_DIM, nb * kblk), BF16)

    qstart = lc + qi * tq
    kmax = jnp.minimum(n_keys, ((qstart + tq - 1) // CHUNK + 1) * CHUNK)
    nkb = (kmax + kblk - 1) // kblk
    lane = lax.broadcasted_iota(I32, (1, LANES), 1)
    qpos = qstart + lane % tq
    krow = lax.broadcasted_iota(I32, (kblk, 1), 0)
    row_group = lax.broadcasted_iota(I32, (LANES, 1), 0) // tq

    def head_major(dst, src, nheads):
        for h in range(nheads):
            x = src[:, h * HEAD_DIM:(h + 1) * HEAD_DIM]
            for j in range(nb):
                xj = x if nb == 1 else jnp.where(row_group == j, x, jnp.zeros_like(x))
                dst[h * LANES:(h + 1) * LANES, j * HEAD_DIM:(j + 1) * HEAD_DIM] = xj

    head_major(iqh_ref, iq_ref, IDX_HEADS)
    head_major(aqh_ref, aq_ref, A_HEADS)
    iw_t = jnp.concatenate([iw_ref[...], jnp.zeros((LANES, LANES - IDX_HEADS), F32)], axis=1).T

    def score_body(kb, carry):
        start = pl.multiple_of(kb * kblk, kblk)
        ikb = ik16_ref[pl.ds(start, kblk), :]
        acc = jnp.zeros((kblk, LANES), F32)
        for pr in range(IDX_HEADS // 2):
            logit = _dot_nt(ikb, iqh_ref[pr * 2 * LANES:(pr + 1) * 2 * LANES, :])
            for e in range(2):
                h = 2 * pr + e
                acc = acc + iw_t[h:h + 1, :] * jnp.maximum(logit[:, e * LANES:(e + 1) * LANES], 0.0)
        kpos = start + krow
        adm = jnp.logical_and(kpos // CHUNK <= qpos // CHUNK, kpos < n_keys)
        bits = pltpu.bitcast(jnp.where(adm, acc, -jnp.inf), I32)
        bits = jnp.where(bits == INT_MIN, 0, bits)
        key_ref[kb] = bits ^ ((bits >> 31) & 0x7FFFFFFF)
        return carry

    lax.fori_loop(0, nkb, score_body, 0)

    def count(pred):
        def body(kb, acc):
            hit = jnp.where(pred(key_ref[kb], kb), 1.0, 0.0)
            return acc + jnp.sum(hit.reshape(kblk // 64, 64, LANES), axis=0)
        acc = lax.fori_loop(0, nkb, body, jnp.zeros((64, LANES), F32))
        return jnp.sum(acc, axis=0, keepdims=True)

    kf = float(topk)

    def bit_body(i, thr):
        cand = thr + (jnp.int32(1) << (31 - i))
        cnt = count(lambda key, kb: key >= cand)
        return jnp.where(cnt >= kf, cand, thr)

    thr = lax.fori_loop(0, 32, bit_body, jnp.full((1, LANES), INT_MIN, I32))
    n_gt = count(lambda key, kb: key > thr)
    n_ge = count(lambda key, kb: key >= thr)
    need = kf - n_gt

    def tie_search():
        def jbody(i, jl):
            cand = jl + (jnp.int32(1) << (14 - i))
            cnt = count(lambda key, kb: jnp.logical_and(key == thr, kb * kblk + krow < cand))
            return jnp.where(cnt <= need, cand, jl)
        return lax.fori_loop(0, 15, jbody, jnp.zeros((1, LANES), I32))

    jlim = lax.cond(jnp.max(n_ge) > kf, tie_search, lambda: jnp.full((1, LANES), 2 ** 30, I32))

    def bias_body(kb, carry):
        key = key_ref[kb]
        kpos = kb * kblk + krow
        sel = jnp.logical_or(key > thr, jnp.logical_and(key == thr, kpos < jlim))
        sel = jnp.logical_and(sel, jnp.logical_and(key > KEY_NEG_INF, key < KEY_POS_INF))
        bias_ref[kb] = jnp.where(sel, 0.0, NEG_BIG)
        return carry

    lax.fori_loop(0, nkb, bias_body, 0)

    m_ref[...] = jnp.full(m_ref.shape, -3e38, F32)
    acc_ref[...] = jnp.zeros(acc_ref.shape, F32)
    pair_group = (lax.broadcasted_iota(I32, (1, 2 * LANES), 1) % LANES) // tq

    def att_body(kb, carry):
        start = pl.multiple_of(kb * kblk, kblk)
        bias = bias_ref[kb]
        m_all = m_ref[...]
        kks = [ak16_ref[g, pl.ds(start, kblk), :] for g in range(A_KV_HEADS)]
        s_ts = [_dot_nt(kks[h0 // rep], aqh_ref[h0 * LANES:(h0 + 2) * LANES, :]) for h0 in range(0, A_HEADS, 2)]
        ps, alphas, m_rows = [], [], []
        for h in range(A_HEADS):
            x = s_ts[h // 2][:, (h % 2) * LANES:(h % 2 + 1) * LANES] + bias
            part = jnp.max(x.reshape(kblk // 64, 64, LANES), axis=0)
            bmax = jnp.max(part, axis=0, keepdims=True)
            m_old = m_all[h:h + 1, :]
            m_new = jnp.maximum(m_old, bmax)
            alphas.append(jnp.exp(m_old - m_new))
            ps.append(jnp.exp(x - m_new).astype(BF16))
            m_rows.append(m_new)
        m_ref[...] = jnp.concatenate(m_rows, axis=0)
        upds = []
        for h0 in range(0, A_HEADS, 2):
            pp = jnp.concatenate(ps[h0:h0 + 2], axis=1)
            if nb > 1:
                pp = jnp.concatenate([jnp.where(pair_group == j, pp, jnp.zeros_like(pp)) for j in range(nb)], axis=0)
            upds.append(_dot(vt_ref[h0 // rep, kb], pp))
        for h in range(A_HEADS):
            acc_ref[h] = acc_ref[h] * alphas[h] + upds[h // 2][:, (h % 2) * LANES:(h % 2 + 1) * LANES]
        return carry

    lax.fori_loop(0, nkb, att_body, 0)

    outs = []
    for h in range(A_HEADS):
        a = acc_ref[h]
        outs.append(a[0:HEAD_DIM, :] / a[HEAD_DIM:LANES, :])
    o_ref[...] = jnp.concatenate(outs, axis=0).T.astype(o_ref.dtype)


def _dsap_call(aq, iq, iw, ik, ak, p, caches, *, batch, seq_q, tq, row0, cache_row0=0, kblk=256):
    has_cache = caches is not None
    lc = caches[0].shape[1] if has_cache else 0
    sn = seq_q
    nb = LANES // tq
    n_keys = lc + sn
    lp = -(-n_keys // kblk) * kblk
    assert LANES % tq == 0 and batch % nb == 0 and sn % tq == 0 and row0 % (nb * sn) == 0 and row0 % LANES == 0
    assert lc % (2 * CACHE_CHUNK) == 0 and CACHE_CHUNK % kblk == 0
    assert (sn % CACHE_CHUNK == 0 and lp == n_keys) or (sn < kblk and lp == lc + kblk)
    groups = batch // nb
    nqb = seq_q // tq
    assert nb == 1 or nqb == 1
    rb0 = row0 // LANES
    nb0 = row0 // (nb * sn)
    topk = min(TOPK_MAX, n_keys // 4)
    nkb_max = lp // kblk

    def qspec(width):
        return pl.BlockSpec((LANES, width), lambda g, i: (rb0 + g * nqb + i, 0))

    def nspec(width, colblk=0):
        return pl.BlockSpec((nb * sn, width), lambda g, i: (nb0 + g, colblk))

    in_specs = [qspec(A_WIDTH), qspec(IDX_HEADS * IDX_DIM), qspec(IDX_HEADS),
                nspec(IDX_DIM), nspec(LANES), nspec(LANES, COL_AV // LANES)]
    args = [aq, iq, iw, ik, ak, p]
    scratch = [pltpu.VMEM((nkb_max, kblk, LANES), I32),
               pltpu.VMEM((nkb_max, kblk, LANES), F32),
               pltpu.VMEM((lp, nb * IDX_DIM), BF16),
               pltpu.VMEM((A_KV_HEADS, lp, nb * HEAD_DIM), BF16),
               pltpu.VMEM((A_KV_HEADS, nkb_max, LANES, nb * kblk), BF16),
               pltpu.VMEM((IDX_HEADS * LANES, nb * IDX_DIM), BF16),
               pltpu.VMEM((A_HEADS * LANES, nb * HEAD_DIM), BF16),
               pltpu.VMEM((A_HEADS, LANES), F32),
               pltpu.VMEM((A_HEADS, LANES, LANES), F32)]
    if has_cache:
        in_specs += [pl.BlockSpec(memory_space=pl.ANY)] * 3
        args += list(caches)
        scratch += [pltpu.VMEM((2, CACHE_CHUNK, IDX_DIM), F32),
                    pltpu.VMEM((2, A_KV_HEADS * CACHE_CHUNK, HEAD_DIM), F32),
                    pltpu.VMEM((2, A_KV_HEADS * CACHE_CHUNK, HEAD_DIM), F32),
                    pltpu.SemaphoreType.DMA((3, 2))]
    kern = functools.partial(_dsap_kernel, tq=tq, sn=sn, lc=lc, topk=topk, kblk=kblk, has_cache=has_cache,
                             cache_row0=cache_row0)
    return pl.pallas_call(
        kern,
        grid=(groups, nqb),
        in_specs=in_specs,
        out_specs=pl.BlockSpec((LANES, A_WIDTH), lambda g, i: (g * nqb + i, 0)),
        out_shape=jax.ShapeDtypeStruct((batch * seq_q, A_WIDTH), BF16),
        scratch_shapes=scratch,
        compiler_params=_params(("parallel", "arbitrary")),
        name="dsa_attention",
    )(*args)


def _sb_kernel(*refs, tq, sn, lc, kblk, has_cache, cache_row0):
    if has_cache:
        q_ref, kn_ref, vn_ref, kc_hbm, vc_hbm, o_ref, qh_ref, carry_ref, acc_ref, kbuf, vbuf, sem = refs
    else:
        q_ref, kn_ref, vn_ref, o_ref, qh_ref, carry_ref, acc_ref = refs
    qlocal = pl.program_id(1) * tq
    qpos = qlocal + lax.broadcasted_iota(I32, (tq, 1), 0)
    kiota = lax.broadcasted_iota(I32, (1, kblk), 1)
    r = lax.broadcasted_iota(I32, (kblk, kblk), 0)
    c = lax.broadcasted_iota(I32, (kblk, kblk), 1)
    upper = jnp.where(r > c, 1.0, 0.0).astype(BF16)
    heads = range(B_HEADS)
    for h in heads:
        qh_ref[h] = q_ref[:, h * HEAD_DIM:(h + 1) * HEAD_DIM]
    carry_ref[...] = jnp.zeros(carry_ref.shape, F32)
    acc_ref[...] = jnp.zeros(acc_ref.shape, F32)

    def visit(k_of, v_of, causal):
        zs = [_dot_nt(qh_ref[h], k_of(h)) for h in heads]
        sps = [_softplus(z) for z in zs]
        lks = [-sp if causal is None else jnp.where(causal, -sp, 0.0) for sp in sps]
        carries = [carry_ref[h] for h in heads]
        laters = []
        for h in heads:
            l1, l2, l3 = _split3(lks[h])
            laters.append(carries[h] + (_dot(l1, upper) + _dot(l2, upper) + _dot(l3, upper)))
        weights = [jnp.exp((zs[h] - sps[h]) + laters[h]) for h in heads]
        if causal is not None:
            weights = [jnp.where(causal, a, 0.0) for a in weights]
        upds = [_dot(weights[h].astype(BF16), v_of(h)) for h in heads]
        new = [carries[h] + jnp.sum(lks[h], axis=1, keepdims=True) for h in heads]
        top = new[0]
        for h in heads:
            acc_ref[h] += upds[h]
            carry_ref[h] = new[h]
            top = jnp.maximum(top, new[h])
        return jnp.max(top) > SB_DEAD

    def alive_cond(state):
        j, alive = state
        return jnp.logical_and(j >= 0, alive)

    def hcols(h):
        return slice(h * HEAD_DIM, (h + 1) * HEAD_DIM)

    if sn >= kblk:
        def new_body(state):
            j, _ = state
            start = pl.multiple_of(j * kblk, kblk)
            alive = visit(lambda h: kn_ref[pl.ds(start, kblk), hcols(h)],
                          lambda h: vn_ref[pl.ds(start, kblk), hcols(h)], start + kiota < qpos)
            return j - 1, alive

        _, alive = lax.while_loop(alive_cond, new_body, ((qlocal + tq - 1) // kblk, True))
    else:
        def padk(ref):
            return lambda h: jnp.concatenate([ref[:, hcols(h)], jnp.zeros((kblk - sn, HEAD_DIM), BF16)], axis=0)

        alive = visit(padk(kn_ref), padk(vn_ref), kiota < qpos)

    if has_cache:
        b = cache_row0 + pl.program_id(0)
        rows = B_HEADS * kblk

        def copies(j, slot):
            src = pl.ds(pl.multiple_of(j * rows, rows), rows)
            return (pltpu.make_async_copy(kc_hbm.at[b, src, :], kbuf.at[slot], sem.at[0, slot]),
                    pltpu.make_async_copy(vc_hbm.at[b, src, :], vbuf.at[slot], sem.at[1, slot]))

        def start(j, slot):
            for cp in copies(j, slot):
                cp.start()

        def wait(j, slot):
            for cp in copies(j, slot):
                cp.wait()

        j0 = lc // kblk - 1

        @pl.when(alive)
        def _():
            start(j0, j0 % 2)

        def cache_body(state):
            j, _ = state
            slot = j % 2
            wait(j, slot)

            @pl.when(j > 0)
            def _():
                start(j - 1, 1 - slot)

            alive = visit(lambda h: kbuf[slot, pl.ds(h, kblk, stride=B_HEADS), :].astype(BF16),
                          lambda h: vbuf[slot, pl.ds(h, kblk, stride=B_HEADS), :].astype(BF16), None)
            return j - 1, alive

        j_end, _ = lax.while_loop(alive_cond, cache_body, (j0, alive))

        @pl.when(jnp.logical_and(alive, j_end >= 0))
        def _():
            wait(j_end, j_end % 2)

    for h in heads:
        o_ref[:, hcols(h)] = acc_ref[h].astype(o_ref.dtype)


def _sb_call(bq, bk16, bv16, caches, *, batch, seq_q, tq, row0, cache_row0=0, kblk=128):
    has_cache = caches is not None
    lc = caches[0].shape[1] // B_HEADS if has_cache else 0
    sn = seq_q
    assert kblk == LANES and lc % kblk == 0 and row0 % sn == 0 and sn % tq == 0
    assert sn % kblk == 0 or (sn < kblk and tq == sn)
    nqb = seq_q // tq
    rb0 = row0 // tq
    nb0 = row0 // sn
    in_specs = [pl.BlockSpec((tq, B_WIDTH), lambda b, i: (rb0 + b * nqb + i, 0)),
                pl.BlockSpec((sn, B_WIDTH), lambda b, i: (nb0 + b, 0)),
                pl.BlockSpec((sn, B_WIDTH), lambda b, i: (nb0 + b, 0))]
    args = [bq, bk16, bv16]
    scratch = [pltpu.VMEM((B_HEADS, tq, HEAD_DIM), BF16), pltpu.VMEM((B_HEADS, tq, LANES), F32),
               pltpu.VMEM((B_HEADS, tq, HEAD_DIM), F32)]
    if has_cache:
        in_specs += [pl.BlockSpec(memory_space=pl.ANY)] * 2
        args += list(caches)
        scratch += [pltpu.VMEM((2, B_HEADS * kblk, HEAD_DIM), F32), pltpu.VMEM((2, B_HEADS * kblk, HEAD_DIM), F32),
                    pltpu.SemaphoreType.DMA((2, 2))]
    kern = functools.partial(_sb_kernel, tq=tq, sn=sn, lc=lc, kblk=kblk, has_cache=has_cache,
                             cache_row0=cache_row0)
    return pl.pallas_call(
        kern,
        grid=(batch, nqb),
        in_specs=in_specs,
        out_specs=pl.BlockSpec((tq, B_WIDTH), lambda b, i: (b * nqb + i, 0)),
        out_shape=jax.ShapeDtypeStruct((batch * seq_q, B_WIDTH), BF16),
        scratch_shapes=scratch,
        compiler_params=_params(("parallel", "arbitrary")),
        name="stick_breaking",
    )(*args)


def _ssd_kernel(z_ref, xbc_ref, misc_ref, cprev_ref, sprev_ref, cw_ref, cb_ref, dtb_ref, alog_ref, dsk_ref,
                ng_ref, y_ref, snew_ref, xpad_ref, st_ref, *, cs):
    ci = pl.program_id(1)
    nci = pl.num_programs(1)
    hpg = C_HEADS // C_GROUPS
    gw = hpg * C_HEAD_DIM

    @pl.when(ci == 0)
    def _():
        xpad_ref[5:8, :] = cprev_ref[0]
        for g in range(C_GROUPS):
            st_ref[g] = sprev_ref[0, g * hpg:(g + 1) * hpg].reshape(gw, D_STATE).T

    @pl.when(ci > 0)
    def _():
        xpad_ref[5:8, :] = xpad_ref[cs + 5:cs + 8, :]

    xpad_ref[8:8 + cs, :] = xbc_ref[...]
    xc = cb_ref[...] + xpad_ref[5:5 + cs, :] * cw_ref[0:1, :]
    for w in range(1, CONV_WIDTH):
        xc = xc + xpad_ref[5 + w:5 + w + cs, :] * cw_ref[w:w + 1, :]
    xc = _silu(xc)

    dt = _softplus(misc_ref[...] + dtb_ref[...])
    a = dt * (-jnp.exp(alog_ref[...]))
    r = lax.broadcasted_iota(I32, (cs, cs), 0)
    c = lax.broadcasted_iota(I32, (cs, cs), 1)
    tril = r >= c
    tril16 = jnp.where(tril, 1.0, 0.0).astype(BF16)
    a1, a2, a3 = _split3(a)
    a_cum = _dot(tril16, a1) + _dot(tril16, a2) + _dot(tril16, a3)
    pad = LANES - cs
    a_sq = a_cum if pad == 0 else jnp.concatenate([a_cum, jnp.zeros((pad, LANES), F32)], axis=0)
    a_cum_t = a_sq.T
    a_last = a_cum[cs - 1:cs, :]
    exp_a = jnp.exp(a_cum)
    dte = jnp.exp(a_last - a_cum)
    exp_last = jnp.exp(a_last)

    for g in range(C_GROUPS):
        bm = xc[:, C_D_INNER + g * D_STATE:C_D_INNER + (g + 1) * D_STATE].astype(BF16)
        cm = xc[:, C_D_INNER + (C_GROUPS + g) * D_STATE:C_D_INNER + (C_GROUPS + g + 1) * D_STATE].astype(BF16)
        cb = _dot_nt(cm, bm)
        st = st_ref[g]
        y_off = _dot(cm, st.astype(BF16))
        y_heads, x_heads, ea_heads, dec_heads = [], [], [], []
        for rr in range(hpg):
            h = g * hpg + rr
            ln = MISC_DT + h
            xs_h = xc[:, h * C_HEAD_DIM:(h + 1) * C_HEAD_DIM]
            xd = xs_h * dt[:, ln:ln + 1]
            seg = a_cum[:, ln:ln + 1] - a_cum_t[ln:ln + 1, 0:cs]
            decay = jnp.where(tril, jnp.exp(jnp.where(tril, seg, 0.0)), 0.0)
            y_heads.append(_dot((cb * decay).astype(BF16), xd.astype(BF16)))
            x_heads.append(xd * dte[:, ln:ln + 1])
            ea_heads.append(jnp.broadcast_to(exp_a[:, ln:ln + 1], (cs, C_HEAD_DIM)))
            dec_heads.append(jnp.broadcast_to(exp_last[:, ln:ln + 1], (1, C_HEAD_DIM)))
        y = jnp.concatenate(y_heads, axis=1) + y_off * jnp.concatenate(ea_heads, axis=1)
        st_new = st * jnp.concatenate(dec_heads, axis=1) + _dot_tn(bm, jnp.concatenate(x_heads, axis=1).astype(BF16))
        st_ref[g] = st_new

        gs = slice(g * gw, (g + 1) * gw)
        y = y + xc[:, gs] * dsk_ref[:, gs]
        y = y * _silu(z_ref[:, gs])
        y = y * lax.rsqrt(jnp.mean(y * y, axis=1, keepdims=True) + EPS) * ng_ref[:, gs]
        y_ref[:, gs] = y.astype(y_ref.dtype)

        @pl.when(ci == nci - 1)
        def _():
            snew_ref[0, g * hpg:(g + 1) * hpg] = st_new.T.reshape(hpg, C_HEAD_DIM, D_STATE)


def _ssd_call(p, cprev, sprev, cw, cb, dtb, alog, dsk, ng, *, batch, seq, cs, row0):
    nci = seq // cs
    rb0 = row0 // cs

    def pcol(width, off):
        return pl.BlockSpec((cs, width), lambda b, i: (rb0 + b * nci + i, off // width))

    def row(width, rows=1):
        return pl.BlockSpec((rows, width), lambda b, i: (0, 0))

    kern = functools.partial(_ssd_kernel, cs=cs)
    return pl.pallas_call(
        kern,
        grid=(batch, nci),
        in_specs=[pcol(C_D_INNER, COL_Z), pcol(C_CONV_DIM, COL_XBC), pcol(LANES, COL_MISC),
                  pl.BlockSpec((1, CONV_WIDTH - 1, C_CONV_DIM), lambda b, i: (b, 0, 0)),
                  pl.BlockSpec((1, C_HEADS, C_HEAD_DIM, D_STATE), lambda b, i: (b, 0, 0, 0)),
                  row(C_CONV_DIM, CONV_WIDTH), row(C_CONV_DIM), row(LANES), row(LANES),
                  row(C_D_INNER), row(C_D_INNER)],
        out_specs=[pl.BlockSpec((cs, C_D_INNER), lambda b, i: (b * nci + i, 0)),
                   pl.BlockSpec((1, C_HEADS, C_HEAD_DIM, D_STATE), lambda b, i: (b, 0, 0, 0))],
        out_shape=[jax.ShapeDtypeStruct((batch * seq, C_D_INNER), BF16),
                   jax.ShapeDtypeStruct((batch, C_HEADS, C_HEAD_DIM, D_STATE), F32)],
        scratch_shapes=[pltpu.VMEM((cs + 8, C_CONV_DIM), F32),
                        pltpu.VMEM((C_GROUPS, D_STATE, C_HEADS // C_GROUPS * C_HEAD_DIM), F32)],
        compiler_params=_params(("parallel", "arbitrary")),
        name="ssd_mixer",
    )(p, p, p, cprev, sprev, cw, cb, dtb, alog, dsk, ng)


def _outproj_kernel(x_ref, yap_ref, ybp_ref, ycp_ref, yas_ref, ybs_ref, ycs_ref, wa_ref, wb_ref, wc_ref, o_ref, *,
                    prompt_tiles):
    def mix(ya_ref, yb_ref, yc_ref):
        o_ref[...] = (x_ref[...] + _dot(ya_ref[...], wa_ref[...]) + _dot(yb_ref[...], wb_ref[...])
                      + _dot(yc_ref[...], wc_ref[...]))

    is_prompt = pl.program_id(0) < prompt_tiles

    @pl.when(is_prompt)
    def _():
        mix(yap_ref, ybp_ref, ycp_ref)

    @pl.when(jnp.logical_not(is_prompt))
    def _():
        mix(yas_ref, ybs_ref, ycs_ref)


def _outproj(x, y_prompt, y_sample, w):
    t, d = x.shape
    tp, ts = y_prompt[0].shape[0], y_sample[0].shape[0]
    tm = _tile(math.gcd(tp, ts), 256)
    npt = tp // tm

    def tok(width):
        return pl.BlockSpec((tm, width), lambda i: (i, 0))

    def ptok(width):
        return pl.BlockSpec((tm, width), lambda i: (jnp.minimum(i, npt - 1), 0))

    def stok(width):
        return pl.BlockSpec((tm, width), lambda i: (jnp.maximum(i - npt, 0), 0))

    widths = (A_WIDTH, B_WIDTH, C_D_INNER)
    return pl.pallas_call(
        functools.partial(_outproj_kernel, prompt_tiles=npt),
        grid=(t // tm,),
        in_specs=[tok(d)] + [ptok(wd) for wd in widths] + [stok(wd) for wd in widths] + [
            pl.BlockSpec((A_WIDTH, d), lambda i: (0, 0)),
            pl.BlockSpec((B_WIDTH, d), lambda i: (1, 0)),
            pl.BlockSpec((C_D_INNER, d), lambda i: (1, 0))],
        out_specs=tok(d),
        out_shape=jax.ShapeDtypeStruct((t, d), F32),
        compiler_params=_params(("parallel",)),
        name="out_proj",
    )(x, *y_prompt, *y_sample, w, w, w)


def _ffn_kernel(x_ref, g_ref, wg_ref, wu_ref, wd_ref, o_ref, h_ref):
    @pl.when(pl.program_id(1) == 0)
    def _():
        x = x_ref[...]
        ms = jnp.mean(x * x, axis=-1, keepdims=True)
        h_ref[...] = (x * lax.rsqrt(ms + EPS) * g_ref[...]).astype(BF16)
        o_ref[...] = x

    h = h_ref[...]
    act = _silu(_dot(h, wg_ref[...])) * _dot(h, wu_ref[...])
    o_ref[...] += _dot(act.astype(BF16), wd_ref[...])


def _ffn(x, g, wg, wu, wd):
    t, d = x.shape
    f = wg.shape[1]
    tm = _tile(t, 1024)
    tf = 512
    once = pl.Buffered(1)
    return pl.pallas_call(
        _ffn_kernel,
        grid=(t // tm, f // tf),
        in_specs=[pl.BlockSpec((tm, d), lambda i, j: (i, 0), pipeline_mode=once),
                  pl.BlockSpec((1, d), lambda i, j: (0, 0)),
                  pl.BlockSpec((d, tf), lambda i, j: (0, j)),
                  pl.BlockSpec((d, tf), lambda i, j: (0, j)),
                  pl.BlockSpec((tf, d), lambda i, j: (j, 0))],
        out_specs=pl.BlockSpec((tm, d), lambda i, j: (i, 0), pipeline_mode=once),
        out_shape=jax.ShapeDtypeStruct((t, d), F32),
        scratch_shapes=[pltpu.VMEM((tm, d), BF16)],
        compiler_params=_params(("parallel", "arbitrary")),
        name="swiglu",
    )(x, g, wg, wu, wd)


def _rope_rows(pos):
    half = ROPE_DIM // 2
    inv = 1.0 / (ROPE_THETA ** (jnp.arange(0, ROPE_DIM, 2, dtype=F32) / ROPE_DIM))
    ang = pos.astype(F32)[:, None] * inv[None, :]
    cos, sin = jnp.cos(ang), jnp.sin(ang)
    n = pos.shape[0]
    one = jnp.ones((n, HEAD_DIM - ROPE_DIM), F32)
    zero = jnp.zeros((n, HEAD_DIM - ROPE_DIM), F32)
    zh = jnp.zeros((n, half), F32)
    rc = jnp.concatenate([cos, cos, one], axis=1)
    rs1 = jnp.concatenate([zh, sin, zero], axis=1)
    rs2 = jnp.concatenate([-sin, zh, zero], axis=1)
    return tuple(jnp.tile(a, (1, LANES // HEAD_DIM)) for a in (rc, rs1, rs2))


def _w_in_segments():
    offs = [0]
    for n in IN_SIZES:
        offs.append(offs[-1] + n)
    dst = (COL_AQ, COL_AK, COL_AV, COL_IQ, COL_MISC + MISC_IK, COL_MISC + MISC_IW, COL_BQ, COL_BK, COL_BV,
           COL_Z, COL_XBC, COL_MISC + MISC_DT)
    return [(offs[i], dst[i], IN_SIZES[i]) for i in range(len(IN_SIZES))]


def _regroup_kernel(w_ref, o_ref):
    for src, dst, width in _w_in_segments():
        o_ref[:, dst:dst + width] = w_ref[0, :, src:src + width].astype(BF16)
    used = COL_MISC + MISC_DT + C_HEADS
    o_ref[:, used:] = jnp.zeros((o_ref.shape[0], P_COLS - used), BF16)


def _rearrange_w_in(w, layer):
    _, d, n = w.shape
    tr = _tile(d, 256)
    return pl.pallas_call(
        _regroup_kernel,
        grid=(d // tr,),
        in_specs=[pl.BlockSpec((1, tr, n), lambda i: (layer, i, 0))],
        out_specs=pl.BlockSpec((tr, P_COLS), lambda i: (i, 0)),
        out_shape=jax.ShapeDtypeStruct((d, P_COLS), BF16),
        compiler_params=_params(("parallel",)),
        name="regroup_w_in",
    )(w)


def _lane_row(v, off):
    return jnp.zeros((1, LANES), F32).at[0, off:off + v.shape[0]].set(v)


def _gain_row(g):
    return jnp.tile(g, LANES // HEAD_DIM)[None, :]


def kernel(x_prompt, x_sample, cache_a_k, cache_a_v, cache_a_idx_k, cache_b_k, cache_b_v, state_c_ssm, state_c_conv, norm1_g, w_in, a_q_norm_g, a_k_norm_g, idx_k_norm_g, b_q_norm_g, b_k_norm_g, c_conv_w, c_conv_b, c_dt_bias, c_a_log, c_d, c_norm_g, w_out, norm2_g, w_gate, w_up, w_down):
    bp, sp, d = x_prompt.shape
    bs, ss, _ = x_sample.shape
    depth = w_in.shape[0]
    past = cache_a_k.shape[2]
    tp, ts = bp * sp, bs * ss
    tq_p = 128
    cs_p = 128 if sp % 128 == 0 else sp
    assert ss >= CONV_WIDTH - 1 and sp >= CONV_WIDTH - 1

    pos = jnp.concatenate([jnp.tile(jnp.arange(sp, dtype=I32), bp),
                           jnp.tile(past + jnp.arange(ss, dtype=I32), bs)])
    rc, rs1, rs2 = _rope_rows(pos)
    x = jnp.concatenate([x_prompt.reshape(tp, d), x_sample.reshape(ts, d)], axis=0)
    zero_conv = jnp.zeros((bp, CONV_WIDTH - 1, C_CONV_DIM), F32)
    zero_ssm = jnp.zeros((bp, C_HEADS, C_HEAD_DIM, D_STATE), F32)

    p_states, s_states = [], []
    for l in range(depth):
        p = _proj(x, norm1_g[l][None, :], _rearrange_w_in(w_in, l))
        gik = jnp.concatenate([idx_k_norm_g[l], jnp.ones((LANES - IDX_DIM,), F32)])[None, :]
        (aq, ak, iq, ik, iw, bq, bk, bk16, bv16) = _post(
            p, rc, rs1, rs2, _gain_row(a_q_norm_g[l]), _gain_row(a_k_norm_g[l]), gik,
            _gain_row(b_q_norm_g[l]), _gain_row(b_k_norm_g[l]))

        a_caches = (cache_a_idx_k.reshape(depth * bs, past, IDX_DIM),
                    cache_a_k.reshape(depth * bs, past * A_KV_HEADS, HEAD_DIM),
                    cache_a_v.reshape(depth * bs, past * A_KV_HEADS, HEAD_DIM))
        ya_p = _dsap_call(aq, iq, iw, ik, ak, p, None, batch=bp, seq_q=sp, tq=tq_p, row0=0)
        ya_s = _dsap_call(aq, iq, iw, ik, ak, p, a_caches, batch=bs, seq_q=ss, tq=ss, row0=tp, cache_row0=l * bs)
        b_caches = (cache_b_k.reshape(depth * bs, past * B_HEADS, HEAD_DIM),
                    cache_b_v.reshape(depth * bs, past * B_HEADS, HEAD_DIM))
        yb_p = _sb_call(bq, bk16, bv16, None, batch=bp, seq_q=sp, tq=tq_p, row0=0)
        yb_s = _sb_call(bq, bk16, bv16, b_caches, batch=bs, seq_q=ss, tq=ss, row0=tp, cache_row0=l * bs)
        cw, cb = c_conv_w[l], c_conv_b[l][None, :]
        dtb, alog = _lane_row(c_dt_bias[l], MISC_DT), _lane_row(c_a_log[l], MISC_DT)
        dsk, ng = jnp.repeat(c_d[l], C_HEAD_DIM)[None, :], c_norm_g[l][None, :]
        yc_p, ssm_p = _ssd_call(p, zero_conv, zero_ssm, cw, cb, dtb, alog, dsk, ng,
                                batch=bp, seq=sp, cs=cs_p, row0=0)
        yc_s, ssm_s = _ssd_call(p, state_c_conv[l], state_c_ssm[l], cw, cb, dtb, alog, dsk, ng,
                                batch=bs, seq=ss, cs=ss, row0=tp)

        x = _outproj(x, (ya_p, yb_p, yc_p), (ya_s, yb_s, yc_s), w_out[l].astype(BF16))
        x = _ffn(x, norm2_g[l][None, :], w_gate[l].astype(BF16), w_up[l].astype(BF16), w_down[l].astype(BF16))

        def state(lo, hi, b, s, ssm):
            av = p[lo:hi, COL_AV:COL_AV + LANES]
            bv = p[lo:hi, COL_BV:COL_BV + B_WIDTH]
            conv = jnp.stack([p[lo + (i + 1) * s - (CONV_WIDTH - 1):lo + (i + 1) * s, COL_XBC:COL_XBC + C_CONV_DIM]
                              for i in range(b)])
            return (ak[lo:hi].reshape(b, s, A_KV_HEADS, HEAD_DIM), av.reshape(b, s, A_KV_HEADS, HEAD_DIM),
                    ik[lo:hi].reshape(b, s, IDX_DIM), bk[lo:hi].reshape(b, s, B_HEADS, HEAD_DIM),
                    bv.reshape(b, s, B_HEADS, HEAD_DIM), ssm, conv)

        p_states.append(state(0, tp, bp, sp, ssm_p))
        s_states.append(state(tp, tp + ts, bs, ss, ssm_s))

    p_out = [jnp.stack(zs) for zs in zip(*p_states)]
    s_out = [jnp.stack(zs) for zs in zip(*s_states)]
    return (x[:tp].reshape(bp, sp, d), x[tp:].reshape(bs, ss, d), *p_out, *s_out)
```

```python
import functools
import math

import jax
import jax.numpy as jnp
from jax import lax
from jax.experimental import pallas as pl
from jax.experimental.pallas import tpu as pltpu

F32 = jnp.float32
BF16 = jnp.bfloat16
I32 = jnp.int32

D_MODEL = 2048
EPS = 1e-6
CHUNK = 64
HEAD_DIM = 64
ROPE_DIM = HEAD_DIM // 4
ROPE_THETA = 500000.0
A_HEADS = 8
A_KV_HEADS = 2
A_WIDTH = A_HEADS * HEAD_DIM
IDX_HEADS = 16
IDX_DIM = 64
TOPK_MAX = 256
B_HEADS = 8
B_WIDTH = B_HEADS * HEAD_DIM
C_D_INNER = D_MODEL // 2
C_HEAD_DIM = 64
C_HEADS = C_D_INNER // C_HEAD_DIM
C_GROUPS = 4
D_STATE = 128
CONV_WIDTH = 4
C_CONV_DIM = C_D_INNER + 2 * C_GROUPS * D_STATE
IN_SIZES = (A_WIDTH, A_KV_HEADS * HEAD_DIM, A_KV_HEADS * HEAD_DIM, IDX_HEADS * IDX_DIM, IDX_DIM, IDX_HEADS,
            B_WIDTH, B_WIDTH, B_WIDTH, C_D_INNER, C_CONV_DIM, C_HEADS)

LANES = 128
VMEM_LIMIT = 48 * 1024 * 1024

COL_XBC, COL_IQ, COL_Z, COL_AQ, COL_BQ, COL_BK, COL_BV, COL_AK, COL_AV, COL_MISC = (
    0, 2048, 3072, 4096, 4608, 5120, 5632, 6144, 6272, 6400)
P_COLS = 6656
MISC_IK, MISC_IW, MISC_DT = 0, 64, 80

NEG_BIG = -1e30
INT_MIN = -2 ** 31
KEY_POS_INF = 0x7F800000
KEY_NEG_INF = INT_MIN + 0x7FFFFF
SB_DEAD = -104.0


def _params(sem):
    return pltpu.CompilerParams(dimension_semantics=sem, vmem_limit_bytes=VMEM_LIMIT)


def _tile(n, pref):
    t = min(n, pref)
    while n % t:
        t -= 8
    return t


def _dot_nt(a, b):
    return lax.dot_general(a, b, (((1,), (1,)), ((), ())), preferred_element_type=F32)


def _dot_tn(a, b):
    return lax.dot_general(a, b, (((0,), (0,)), ((), ())), preferred_element_type=F32)


def _dot(a, b):
    return jnp.dot(a, b, preferred_element_type=F32)


def _split3(x):
    h1 = x.astype(BF16)
    r1 = x - h1.astype(F32)
    h2 = r1.astype(BF16)
    h3 = (r1 - h2.astype(F32)).astype(BF16)
    return h1, h2, h3


def _softplus(x):
    return jnp.maximum(x, 0.0) + jnp.log1p(jnp.exp(-jnp.abs(x)))


def _silu(x):
    return x * (1.0 / (1.0 + jnp.exp(-x)))


def _proj_kernel(x_ref, g_ref, w_ref, o_ref, h_ref):
    @pl.when(pl.program_id(1) == 0)
    def _():
        x = x_ref[...]
        ms = jnp.mean(x * x, axis=-1, keepdims=True)
        h_ref[...] = (x * lax.rsqrt(ms + EPS) * g_ref[...]).astype(BF16)

    o_ref[...] = _dot(h_ref[...], w_ref[...])


def _proj(x, g, w):
    t, d = x.shape
    n = w.shape[1]
    tm = _tile(t, 1024)
    tn = 512
    return pl.pallas_call(
        _proj_kernel,
        grid=(t // tm, n // tn),
        in_specs=[pl.BlockSpec((tm, d), lambda i, j: (i, 0)),
                  pl.BlockSpec((1, d), lambda i, j: (0, 0)),
                  pl.BlockSpec((d, tn), lambda i, j: (0, j))],
        out_specs=pl.BlockSpec((tm, tn), lambda i, j: (i, j)),
        out_shape=jax.ShapeDtypeStruct((t, n), F32),
        scratch_shapes=[pltpu.VMEM((tm, d), BF16)],
        compiler_params=_params(("parallel", "arbitrary")),
        name="in_proj",
    )(x, g, w)


def _post_kernel(aq_r, bq_r, bk_r, bv_r, iq_r, ak_r, misc_r, rc_r, rs1_r, rs2_r,
                 gaq_r, gak_r, gik_r, gbq_r, gbk_r,
                 aq_o, ak_o, iq_o, ik_o, iw_o, bq_o, bk_o, bk16_o, bv16_o):
    r = lax.broadcasted_iota(I32, (LANES, LANES), 0) // HEAD_DIM
    c = lax.broadcasted_iota(I32, (LANES, LANES), 1) // HEAD_DIM
    segm = jnp.where(r == c, 1.0, 0.0).astype(BF16)
    rc, rs1, rs2 = rc_r[...], rs1_r[...], rs2_r[...]

    def headnorm(xc, g_row):
        h1, h2, h3 = _split3(xc * xc)
        ss = _dot(h1, segm) + _dot(h2, segm) + _dot(h3, segm)
        return xc * lax.rsqrt(ss * (1.0 / HEAD_DIM) + EPS) * g_row

    def rope(xc):
        return xc * rc + pltpu.roll(xc, 8, 1) * rs1 + pltpu.roll(xc, LANES - 8, 1) * rs2

    for k in range(A_WIDTH // LANES):
        sl = slice(k * LANES, (k + 1) * LANES)
        aq_o[:, sl] = (rope(headnorm(aq_r[:, sl], gaq_r[...])) * (HEAD_DIM ** -0.5)).astype(BF16)
        bq_o[:, sl] = (headnorm(bq_r[:, sl], gbq_r[...]) * (HEAD_DIM ** -0.5)).astype(BF16)
        bk = headnorm(bk_r[:, sl], gbk_r[...])
        bk_o[:, sl] = bk
        bk16_o[:, sl] = bk.astype(BF16)
        bv16_o[:, sl] = bv_r[:, sl].astype(BF16)
    for k in range(IDX_HEADS * IDX_DIM // LANES):
        sl = slice(k * LANES, (k + 1) * LANES)
        iq_o[:, sl] = rope(iq_r[:, sl]).astype(BF16)
    ak_o[...] = rope(headnorm(ak_r[...], gak_r[...]))
    misc = misc_r[...]
    ik_o[...] = rope(headnorm(misc, gik_r[...]))[:, MISC_IK:MISC_IK + IDX_DIM]
    iw_o[...] = misc[:, MISC_IW:MISC_IW + IDX_HEADS] * (IDX_HEADS ** -0.5 * IDX_DIM ** -0.5)


def _post(p, rc, rs1, rs2, gaq, gak, gik, gbq, gbk):
    t = p.shape[0]
    tm = _tile(t, 256)

    def col(width, off):
        return pl.BlockSpec((tm, width), lambda i: (i, off // width))

    def row(width):
        return pl.BlockSpec((1, width), lambda i: (0, 0))

    def out(width):
        return pl.BlockSpec((tm, width), lambda i: (i, 0))

    widths_dtypes = [(A_WIDTH, BF16), (LANES, F32), (IDX_HEADS * IDX_DIM, BF16), (IDX_DIM, F32),
                     (IDX_HEADS, F32), (B_WIDTH, BF16), (B_WIDTH, F32), (B_WIDTH, BF16), (B_WIDTH, BF16)]
    return pl.pallas_call(
        _post_kernel,
        grid=(t // tm,),
        in_specs=[col(A_WIDTH, COL_AQ), col(B_WIDTH, COL_BQ), col(B_WIDTH, COL_BK), col(B_WIDTH, COL_BV),
                  col(IDX_HEADS * IDX_DIM, COL_IQ), col(LANES, COL_AK), col(LANES, COL_MISC),
                  out(LANES), out(LANES), out(LANES),
                  row(LANES), row(LANES), row(LANES), row(LANES), row(LANES)],
        out_specs=[out(w) for w, _ in widths_dtypes],
        out_shape=[jax.ShapeDtypeStruct((t, w), dt) for w, dt in widths_dtypes],
        compiler_params=_params(("parallel",)),
        name="head_post",
    )(p, p, p, p, p, p, p, rc, rs1, rs2, gaq, gak, gik, gbq, gbk)


def _dsa_kernel(*refs, tq, sn, lc, topk, kblk, has_cache):
    if has_cache:
        (aq_ref, iq_ref, iw_ref, ikn_ref, akn_ref, avn_ref, ikc_ref, akc_ref, avc_ref, o_ref,
         key_ref, bias_ref, ik16_ref, ak16_ref, vt_ref, iqh_ref, aqh_ref, m_ref, acc_ref) = refs
    else:
        (aq_ref, iq_ref, iw_ref, ikn_ref, akn_ref, avn_ref, o_ref,
         key_ref, bias_ref, ik16_ref, ak16_ref, vt_ref, iqh_ref, aqh_ref, m_ref, acc_ref) = refs
    qi = pl.program_id(1)
    n_keys = lc + sn
    lp = ik16_ref.shape[0]
    rep = A_HEADS // A_KV_HEADS

    @pl.when(qi == 0)
    def _():
        def put(rows, ik, ak, av):
            n = ik.shape[0]
            ik16_ref[rows:rows + n, :] = ik.astype(BF16)
            ak16_ref[rows:rows + n, :] = ak.astype(BF16)
            avt = av.T.astype(BF16)
            for c in range(n // kblk):
                for g in range(A_KV_HEADS):
                    vt_ref[g, (rows // kblk) + c, 0:HEAD_DIM, :] = avt[g * HEAD_DIM:(g + 1) * HEAD_DIM,
                                                                      c * kblk:(c + 1) * kblk]

        step = 2 * kblk
        if has_cache:
            for r0 in range(0, lc, step):
                put(r0, ikc_ref[0, r0:r0 + step, :], akc_ref[0, r0:r0 + step, :], avc_ref[0, r0:r0 + step, :])
        for r0 in range(0, sn - sn % step, step):
            put(lc + r0, ikn_ref[r0:r0 + step, :], akn_ref[r0:r0 + step, :], avn_ref[r0:r0 + step, :])
        rem = sn % step
        if lp > lc + sn - rem:
            n = lp - (lc + sn - rem)

            def padded(ref):
                w = ref.shape[-1]
                if rem == 0:
                    return jnp.zeros((n, w), F32)
                return jnp.concatenate([ref[sn - rem:sn, :], jnp.zeros((n - rem, w), F32)], axis=0)

            put(lc + sn - rem, padded(ikn_ref), padded(akn_ref), padded(avn_ref))
        for g in range(A_KV_HEADS):
            vt_ref[g, :, HEAD_DIM:LANES, :] = jnp.ones((lp // kblk, LANES - HEAD_DIM, kblk), BF16)

    qstart = lc + qi * tq
    kmax = jnp.minimum(n_keys, ((qstart + tq - 1) // CHUNK + 1) * CHUNK)
    nkb = (kmax + kblk - 1) // kblk
    qpos = qstart + lax.broadcasted_iota(I32, (1, LANES), 1)
    krow = lax.broadcasted_iota(I32, (kblk, 1), 0)

    def head_major(dst, src, nheads):
        for h in range(nheads):
            x = src[:, h * HEAD_DIM:(h + 1) * HEAD_DIM]
            if tq < LANES:
                x = jnp.concatenate([x, jnp.zeros((LANES - tq, HEAD_DIM), x.dtype)], axis=0)
            dst[h * LANES:(h + 1) * LANES, :] = x

    head_major(iqh_ref, iq_ref, IDX_HEADS)
    head_major(aqh_ref, aq_ref, A_HEADS)
    iw = jnp.concatenate([iw_ref[...], jnp.zeros((tq, LANES - IDX_HEADS), F32)], axis=1)
    if tq < LANES:
        iw = jnp.concatenate([iw, jnp.zeros((LANES - tq, LANES), F32)], axis=0)
    iw_t = iw.T

    def score_body(kb, carry):
        start = pl.multiple_of(kb * kblk, kblk)
        ikb = ik16_ref[pl.ds(start, kblk), :]
        acc = jnp.zeros((kblk, LANES), F32)
        for pr in range(IDX_HEADS // 2):
            logit = _dot_nt(ikb, iqh_ref[pr * 2 * LANES:(pr + 1) * 2 * LANES, :])
            for e in range(2):
                h = 2 * pr + e
                acc = acc + iw_t[h:h + 1, :] * jnp.maximum(logit[:, e * LANES:(e + 1) * LANES], 0.0)
        kpos = start + krow
        adm = jnp.logical_and(kpos // CHUNK <= qpos // CHUNK, kpos < n_keys)
        bits = pltpu.bitcast(jnp.where(adm, acc, -jnp.inf), I32)
        bits = jnp.where(bits == INT_MIN, 0, bits)
        key_ref[kb] = bits ^ ((bits >> 31) & 0x7FFFFFFF)
        return carry

    lax.fori_loop(0, nkb, score_body, 0)

    def count(pred):
        def body(kb, acc):
            hit = jnp.where(pred(key_ref[kb], kb), 1.0, 0.0)
            return acc + jnp.sum(hit.reshape(kblk // 64, 64, LANES), axis=0)
        acc = lax.fori_loop(0, nkb, body, jnp.zeros((64, LANES), F32))
        return jnp.sum(acc, axis=0, keepdims=True)

    kf = float(topk)

    def bit_body(i, thr):
        cand = thr + (jnp.int32(1) << (31 - i))
        cnt = count(lambda key, kb: key >= cand)
        return jnp.where(cnt >= kf, cand, thr)

    thr = lax.fori_loop(0, 32, bit_body, jnp.full((1, LANES), INT_MIN, I32))
    n_gt = count(lambda key, kb: key > thr)
    n_ge = count(lambda key, kb: key >= thr)
    need = kf - n_gt

    def tie_search():
        def jbody(i, jl):
            cand = jl + (jnp.int32(1) << (14 - i))
            cnt = count(lambda key, kb: jnp.logical_and(key == thr, kb * kblk + krow < cand))
            return jnp.where(cnt <= need, cand, jl)
        return lax.fori_loop(0, 15, jbody, jnp.zeros((1, LANES), I32))

    jlim = lax.cond(jnp.max(n_ge) > kf, tie_search, lambda: jnp.full((1, LANES), 2 ** 30, I32))

    def bias_body(kb, carry):
        key = key_ref[kb]
        kpos = kb * kblk + krow
        sel = jnp.logical_or(key > thr, jnp.logical_and(key == thr, kpos < jlim))
        sel = jnp.logical_and(sel, jnp.logical_and(key > KEY_NEG_INF, key < KEY_POS_INF))
        bias_ref[kb] = jnp.where(sel, 0.0, NEG_BIG)
        return carry

    lax.fori_loop(0, nkb, bias_body, 0)

    m_ref[...] = jnp.full(m_ref.shape, -3e38, F32)
    acc_ref[...] = jnp.zeros(acc_ref.shape, F32)

    def att_body(kb, carry):
        start = pl.multiple_of(kb * kblk, kblk)
        bias = bias_ref[kb]
        m_all = m_ref[...]
        kk = ak16_ref[pl.ds(start, kblk), :]
        s_ts = [_dot_nt(kk[:, (h0 // rep) * HEAD_DIM:(h0 // rep + 1) * HEAD_DIM],
                        aqh_ref[h0 * LANES:(h0 + 2) * LANES, :]) for h0 in range(0, A_HEADS, 2)]
        ps, alphas, m_rows = [], [], []
        for h in range(A_HEADS):
            x = s_ts[h // 2][:, (h % 2) * LANES:(h % 2 + 1) * LANES] + bias
            part = jnp.max(x.reshape(kblk // 64, 64, LANES), axis=0)
            bmax = jnp.max(part, axis=0, keepdims=True)
            m_old = m_all[h:h + 1, :]
            m_new = jnp.maximum(m_old, bmax)
            alphas.append(jnp.exp(m_old - m_new))
            ps.append(jnp.exp(x - m_new).astype(BF16))
            m_rows.append(m_new)
        m_ref[...] = jnp.concatenate(m_rows, axis=0)
        upds = [_dot(vt_ref[h0 // rep, kb], jnp.concatenate(ps[h0:h0 + 2], axis=1)) for h0 in range(0, A_HEADS, 2)]
        for h in range(A_HEADS):
            acc_ref[h] = acc_ref[h] * alphas[h] + upds[h // 2][:, (h % 2) * LANES:(h % 2 + 1) * LANES]
        return carry

    lax.fori_loop(0, nkb, att_body, 0)

    outs = []
    for h in range(A_HEADS):
        a = acc_ref[h]
        outs.append(a[0:HEAD_DIM, :] / a[HEAD_DIM:LANES, :])
    o_t = jnp.concatenate(outs, axis=0)
    o_ref[...] = o_t.T[0:tq, :].astype(o_ref.dtype)


def _dsa_call(aq, iq, iw, ik, ak, p, caches, *, batch, seq_q, tq, row0, kblk=256):
    has_cache = caches is not None
    lc = caches[0].shape[1] if has_cache else 0
    sn = seq_q
    n_keys = lc + sn
    lp = -(-n_keys // kblk) * kblk
    assert lc % (2 * kblk) == 0 and row0 % sn == 0 and sn % tq == 0 and tq % 16 == 0
    nqb = seq_q // tq
    rb0 = row0 // tq
    nb0 = row0 // sn
    topk = min(TOPK_MAX, n_keys // 4)
    nkb_max = lp // kblk

    def qspec(width):
        return pl.BlockSpec((tq, width), lambda b, i: (rb0 + b * nqb + i, 0))

    def nspec(width, colblk=0):
        return pl.BlockSpec((sn, width), lambda b, i: (nb0 + b, colblk))

    def cspec(width):
        return pl.BlockSpec((1, lc, width), lambda b, i: (b, 0, 0))

    in_specs = [qspec(A_WIDTH), qspec(IDX_HEADS * IDX_DIM), qspec(IDX_HEADS),
                nspec(IDX_DIM), nspec(LANES), nspec(LANES, COL_AV // LANES)]
    args = [aq, iq, iw, ik, ak, p]
    if has_cache:
        in_specs += [cspec(IDX_DIM), cspec(LANES), cspec(LANES)]
        args += list(caches)
    kern = functools.partial(_dsa_kernel, tq=tq, sn=sn, lc=lc, topk=topk, kblk=kblk, has_cache=has_cache)
    return pl.pallas_call(
        kern,
        grid=(batch, nqb),
        in_specs=in_specs,
        out_specs=pl.BlockSpec((tq, A_WIDTH), lambda b, i: (b * nqb + i, 0)),
        out_shape=jax.ShapeDtypeStruct((batch * seq_q, A_WIDTH), BF16),
        scratch_shapes=[pltpu.VMEM((nkb_max, kblk, LANES), I32),
                        pltpu.VMEM((nkb_max, kblk, LANES), F32),
                        pltpu.VMEM((lp, IDX_DIM), BF16),
                        pltpu.VMEM((lp, LANES), BF16),
                        pltpu.VMEM((A_KV_HEADS, nkb_max, LANES, kblk), BF16),
                        pltpu.VMEM((IDX_HEADS * LANES, IDX_DIM), BF16),
                        pltpu.VMEM((A_HEADS * LANES, HEAD_DIM), BF16),
                        pltpu.VMEM((A_HEADS, LANES), F32),
                        pltpu.VMEM((A_HEADS, LANES, LANES), F32)],
        compiler_params=_params(("parallel", "arbitrary")),
        name="dsa_attention",
    )(*args)


CACHE_CHUNK = 512


def _dsap_kernel(*refs, tq, sn, lc, topk, kblk, has_cache, layer):
    nb = LANES // tq
    if has_cache:
        (aq_ref, iq_ref, iw_ref, ikn_ref, akn_ref, avn_ref, ikc_hbm, akc_hbm, avc_hbm, o_ref,
         key_ref, bias_ref, ik16_ref, ak16_ref, vt_ref, iqh_ref, aqh_ref, m_ref, acc_ref,
         ikbuf, akbuf, avbuf, sem) = refs
    else:
        (aq_ref, iq_ref, iw_ref, ikn_ref, akn_ref, avn_ref, o_ref,
         key_ref, bias_ref, ik16_ref, ak16_ref, vt_ref, iqh_ref, aqh_ref, m_ref, acc_ref) = refs
    grp = pl.program_id(0)
    qi = pl.program_id(1)
    n_keys = lc + sn
    lp = ik16_ref.shape[0]
    rep = A_HEADS // A_KV_HEADS
    ch = CACHE_CHUNK

    @pl.when(qi == 0)
    def _():
        def put(row, j, ik, k0, k1, v_t):
            n = ik.shape[0]
            cols = slice(j * IDX_DIM, (j + 1) * IDX_DIM)
            ik16_ref[pl.ds(row, n), cols] = ik.astype(BF16)
            ak16_ref[0, pl.ds(row, n), cols] = k0.astype(BF16)
            ak16_ref[1, pl.ds(row, n), cols] = k1.astype(BF16)
            vt = v_t.astype(BF16)
            for e in range(n // kblk):
                for g in range(A_KV_HEADS):
                    vt_ref[g, row // kblk + e, 0:HEAD_DIM, j * kblk:(j + 1) * kblk] = (
                        vt[g * HEAD_DIM:(g + 1) * HEAD_DIM, e * kblk:(e + 1) * kblk])

        if has_cache:
            nc = lc // ch

            def copies(j, c, slot):
                b = grp * nb + j
                src = pl.ds(pl.multiple_of(c * ch, ch), ch)
                return (pltpu.make_async_copy(ikc_hbm.at[layer, b, :, src], ikbuf.at[slot], sem.at[0, slot]),
                        pltpu.make_async_copy(akc_hbm.at[layer, b, :, :, src], akbuf.at[slot], sem.at[1, slot]),
                        pltpu.make_async_copy(avc_hbm.at[layer, b, :, :, src], avbuf.at[slot], sem.at[2, slot]))

            def start(j, c, slot):
                for cp in copies(j, c, slot):
                    cp.start()

            start(0, 0, 0)
            for j in range(nb):
                def chunk_body(c, carry, j=j):
                    slot = c % 2
                    for cp in copies(j, c, slot):
                        cp.wait()

                    @pl.when(c + 1 < nc)
                    def _():
                        start(j, c + 1, 1 - slot)

                    if j + 1 < nb:
                        @pl.when(c + 1 == nc)
                        def _():
                            start(j + 1, 0, 0)

                    kk = akbuf[slot].reshape(LANES, ch).T
                    ik = jnp.concatenate([ikbuf[slot], jnp.zeros((LANES - IDX_DIM, ch), F32)], axis=0).T
                    put(pl.multiple_of(c * ch, ch), j, ik[:, 0:IDX_DIM], kk[:, 0:HEAD_DIM], kk[:, HEAD_DIM:LANES],
                        avbuf[slot].reshape(LANES, ch))
                    return carry

                lax.fori_loop(0, nc, chunk_body, 0)

        for j in range(nb):
            if sn % ch == 0:
                def new_body(c, carry, j=j):
                    src = pl.ds(pl.multiple_of(j * sn + c * ch, ch), ch)
                    ak = akn_ref[src, :]
                    put(pl.multiple_of(lc + c * ch, ch), j, ikn_ref[src, :], ak[:, 0:HEAD_DIM], ak[:, HEAD_DIM:LANES],
                        avn_ref[src, :].T)
                    return carry

                lax.fori_loop(0, sn // ch, new_body, 0)
            else:
                def padded(ref):
                    return jnp.concatenate([ref[j * sn:(j + 1) * sn, :],
                                            jnp.zeros((kblk - sn, ref.shape[-1]), F32)], axis=0)

                ak = padded(akn_ref)
                put(lc, j, padded(ikn_ref), ak[:, 0:HEAD_DIM], ak[:, HEAD_DIM:LANES], padded(avn_ref).T)
        for g in range(A_KV_HEADS):
            vt_ref[g, :, HEAD_DIM:LANES, :] = jnp.ones((lp // kblk, LANES - HEAD_DIM, nb * kblk), BF16)

    qstart = lc + qi * tq
    kmax = jnp.minimum(n_keys, ((qstart + tq - 1) // CHUNK + 1) * CHUNK)
    nkb = (kmax + kblk - 1) // kblk
    lane = lax.broadcasted_iota(I32, (1, LANES), 1)
    qpos = qstart + lane % tq
    krow = lax.broadcasted_iota(I32, (kblk, 1), 0)
    row_group = lax.broadcasted_iota(I32, (LANES, 1), 0) // tq

    def head_major(dst, src, nheads):
        for h in range(nheads):
            x = src[:, h * HEAD_DIM:(h + 1) * HEAD_DIM]
            for j in range(nb):
                xj = x if nb == 1 else jnp.where(row_group == j, x, jnp.zeros_like(x))
                dst[h * LANES:(h + 1) * LANES, j * HEAD_DIM:(j + 1) * HEAD_DIM] = xj

    head_major(iqh_ref, iq_ref, IDX_HEADS)
    head_major(aqh_ref, aq_ref, A_HEADS)
    iw_t = jnp.concatenate([iw_ref[...], jnp.zeros((LANES, LANES - IDX_HEADS), F32)], axis=1).T

    def score_body(kb, carry):
        start = pl.multiple_of(kb * kblk, kblk)
        ikb = ik16_ref[pl.ds(start, kblk), :]
        acc = jnp.zeros((kblk, LANES), F32)
        for pr in range(IDX_HEADS // 2):
            logit = _dot_nt(ikb, iqh_ref[pr * 2 * LANES:(pr + 1) * 2 * LANES, :])
            for e in range(2):
                h = 2 * pr + e
                acc = acc + iw_t[h:h + 1, :] * jnp.maximum(logit[:, e * LANES:(e + 1) * LANES], 0.0)
        kpos = start + krow
        adm = jnp.logical_and(kpos // CHUNK <= qpos // CHUNK, kpos < n_keys)
        bits = pltpu.bitcast(jnp.where(adm, acc, -jnp.inf), I32)
        bits = jnp.where(bits == INT_MIN, 0, bits)
        key_ref[kb] = bits ^ ((bits >> 31) & 0x7FFFFFFF)
        return carry

    lax.fori_loop(0, nkb, score_body, 0)

    def count(pred):
        def body(kb, acc):
            hit = jnp.where(pred(key_ref[kb], kb), 1.0, 0.0)
            return acc + jnp.sum(hit.reshape(kblk // 64, 64, LANES), axis=0)
        acc = lax.fori_loop(0, nkb, body, jnp.zeros((64, LANES), F32))
        return jnp.sum(acc, axis=0, keepdims=True)

    kf = float(topk)

    def bit_body(i, thr):
        cand = thr + (jnp.int32(1) << (31 - i))
        cnt = count(lambda key, kb: key >= cand)
        return jnp.where(cnt >= kf, cand, thr)

    thr = lax.fori_loop(0, 32, bit_body, jnp.full((1, LANES), INT_MIN, I32))
    n_gt = count(lambda key, kb: key > thr)
    n_ge = count(lambda key, kb: key >= thr)
    need = kf - n_gt

    def tie_search():
        def jbody(i, jl):
            cand = jl + (jnp.int32(1) << (14 - i))
            cnt = count(lambda key, kb: jnp.logical_and(key == thr, kb * kblk + krow < cand))
            return jnp.where(cnt <= need, cand, jl)
        return lax.fori_loop(0, 15, jbody, jnp.zeros((1, LANES), I32))

    jlim = lax.cond(jnp.max(n_ge) > kf, tie_search, lambda: jnp.full((1, LANES), 2 ** 30, I32))

    def bias_body(kb, carry):
        key = key_ref[kb]
        kpos = kb * kblk + krow
        sel = jnp.logical_or(key > thr, jnp.logical_and(key == thr, kpos < jlim))
        sel = jnp.logical_and(sel, jnp.logical_and(key > KEY_NEG_INF, key < KEY_POS_INF))
        bias_ref[kb] = jnp.where(sel, 0.0, NEG_BIG)
        return carry

    lax.fori_loop(0, nkb, bias_body, 0)

    m_ref[...] = jnp.full(m_ref.shape, -3e38, F32)
    acc_ref[...] = jnp.zeros(acc_ref.shape, F32)
    pair_group = (lax.broadcasted_iota(I32, (1, 2 * LANES), 1) % LANES) // tq

    def att_body(kb, carry):
        start = pl.multiple_of(kb * kblk, kblk)
        bias = bias_ref[kb]
        m_all = m_ref[...]
        kks = [ak16_ref[g, pl.ds(start, kblk), :] for g in range(A_KV_HEADS)]
        s_ts = [_dot_nt(kks[h0 // rep], aqh_ref[h0 * LANES:(h0 + 2) * LANES, :]) for h0 in range(0, A_HEADS, 2)]
        ps, alphas, m_rows = [], [], []
        for h in range(A_HEADS):
            x = s_ts[h // 2][:, (h % 2) * LANES:(h % 2 + 1) * LANES] + bias
            part = jnp.max(x.reshape(kblk // 64, 64, LANES), axis=0)
            bmax = jnp.max(part, axis=0, keepdims=True)
            m_old = m_all[h:h + 1, :]
            m_new = jnp.maximum(m_old, bmax)
            alphas.append(jnp.exp(m_old - m_new))
            ps.append(jnp.exp(x - m_new).astype(BF16))
            m_rows.append(m_new)
        m_ref[...] = jnp.concatenate(m_rows, axis=0)
        upds = []
        for h0 in range(0, A_HEADS, 2):
            pp = jnp.concatenate(ps[h0:h0 + 2], axis=1)
            if nb > 1:
                pp = jnp.concatenate([jnp.where(pair_group == j, pp, jnp.zeros_like(pp)) for j in range(nb)], axis=0)
            upds.append(_dot(vt_ref[h0 // rep, kb], pp))
        for h in range(A_HEADS):
            acc_ref[h] = acc_ref[h] * alphas[h] + upds[h // 2][:, (h % 2) * LANES:(h % 2 + 1) * LANES]
        return carry

    lax.fori_loop(0, nkb, att_body, 0)

    outs = []
    for h in range(A_HEADS):
        a = acc_ref[h]
        outs.append(a[0:HEAD_DIM, :] / a[HEAD_DIM:LANES, :])
    o_ref[...] = jnp.concatenate(outs, axis=0).T.astype(o_ref.dtype)


def _dsap_call(aq, iq, iw, ik, ak, p, caches, *, batch, seq_q, tq, row0, layer=0, kblk=256):
    has_cache = caches is not None
    lc = caches[0].shape[-1] if has_cache else 0
    sn = seq_q
    nb = LANES // tq
    n_keys = lc + sn
    lp = -(-n_keys // kblk) * kblk
    assert LANES % tq == 0 and batch % nb == 0 and sn % tq == 0 and row0 % (nb * sn) == 0 and row0 % LANES == 0
    assert lc % (2 * CACHE_CHUNK) == 0 and CACHE_CHUNK % kblk == 0
    assert (sn % CACHE_CHUNK == 0 and lp == n_keys) or (sn < kblk and lp == lc + kblk)
    groups = batch // nb
    nqb = seq_q // tq
    assert nb == 1 or nqb == 1
    rb0 = row0 // LANES
    nb0 = row0 // (nb * sn)
    topk = min(TOPK_MAX, n_keys // 4)
    nkb_max = lp // kblk

    def qspec(width):
        return pl.BlockSpec((LANES, width), lambda g, i: (rb0 + g * nqb + i, 0))

    def nspec(width, colblk=0):
        return pl.BlockSpec((nb * sn, width), lambda g, i: (nb0 + g, colblk))

    in_specs = [qspec(A_WIDTH), qspec(IDX_HEADS * IDX_DIM), qspec(IDX_HEADS),
                nspec(IDX_DIM), nspec(LANES), nspec(LANES, COL_AV // LANES)]
    args = [aq, iq, iw, ik, ak, p]
    scratch = [pltpu.VMEM((nkb_max, kblk, LANES), I32),
               pltpu.VMEM((nkb_max, kblk, LANES), F32),
               pltpu.VMEM((lp, nb * IDX_DIM), BF16),
               pltpu.VMEM((A_KV_HEADS, lp, nb * HEAD_DIM), BF16),
               pltpu.VMEM((A_KV_HEADS, nkb_max, LANES, nb * kblk), BF16),
               pltpu.VMEM((IDX_HEADS * LANES, nb * IDX_DIM), BF16),
               pltpu.VMEM((A_HEADS * LANES, nb * HEAD_DIM), BF16),
               pltpu.VMEM((A_HEADS, LANES), F32),
               pltpu.VMEM((A_HEADS, LANES, LANES), F32)]
    if has_cache:
        in_specs += [pl.BlockSpec(memory_space=pl.ANY)] * 3
        args += list(caches)
        scratch += [pltpu.VMEM((2, IDX_DIM, CACHE_CHUNK), F32),
                    pltpu.VMEM((2, A_KV_HEADS, HEAD_DIM, CACHE_CHUNK), F32),
                    pltpu.VMEM((2, A_KV_HEADS, HEAD_DIM, CACHE_CHUNK), F32),
                    pltpu.SemaphoreType.DMA((3, 2))]
    kern = functools.partial(_dsap_kernel, tq=tq, sn=sn, lc=lc, topk=topk, kblk=kblk, has_cache=has_cache,
                             layer=layer)
    return pl.pallas_call(
        kern,
        grid=(groups, nqb),
        in_specs=in_specs,
        out_specs=pl.BlockSpec((LANES, A_WIDTH), lambda g, i: (g * nqb + i, 0)),
        out_shape=jax.ShapeDtypeStruct((batch * seq_q, A_WIDTH), BF16),
        scratch_shapes=scratch,
        compiler_params=_params(("parallel", "arbitrary")),
        name="dsa_attention",
    )(*args)


def _sb_kernel(*refs, tq, sn, lc, kblk, has_cache, layer):
    if has_cache:
        q_ref, kn_ref, vn_ref, kc_hbm, vc_hbm, o_ref, qh_ref, carry_ref, acc_ref, kbuf, vbuf, sem = refs
    else:
        q_ref, kn_ref, vn_ref, o_ref, qh_ref, carry_ref, acc_ref = refs
    qlocal = pl.program_id(1) * tq
    qpos = qlocal + lax.broadcasted_iota(I32, (tq, 1), 0)
    kiota = lax.broadcasted_iota(I32, (1, kblk), 1)
    r = lax.broadcasted_iota(I32, (kblk, kblk), 0)
    c = lax.broadcasted_iota(I32, (kblk, kblk), 1)
    upper = jnp.where(r > c, 1.0, 0.0).astype(BF16)
    heads = range(B_HEADS)
    for h in heads:
        qh_ref[h] = q_ref[:, h * HEAD_DIM:(h + 1) * HEAD_DIM]
    carry_ref[...] = jnp.zeros(carry_ref.shape, F32)
    acc_ref[...] = jnp.zeros(acc_ref.shape, F32)

    def visit(score, apply, causal):
        zs = [score(h) for h in heads]
        sps = [_softplus(z) for z in zs]
        lks = [-sp if causal is None else jnp.where(causal, -sp, 0.0) for sp in sps]
        carries = [carry_ref[h] for h in heads]
        laters = []
        for h in heads:
            l1, l2, l3 = _split3(lks[h])
            laters.append(carries[h] + (_dot(l1, upper) + _dot(l2, upper) + _dot(l3, upper)))
        weights = [jnp.exp((zs[h] - sps[h]) + laters[h]) for h in heads]
        if causal is not None:
            weights = [jnp.where(causal, a, 0.0) for a in weights]
        upds = [apply(h, weights[h].astype(BF16)) for h in heads]
        new = [carries[h] + jnp.sum(lks[h], axis=1, keepdims=True) for h in heads]
        top = new[0]
        for h in heads:
            acc_ref[h] += upds[h]
            carry_ref[h] = new[h]
            top = jnp.maximum(top, new[h])
        return jnp.max(top) > SB_DEAD

    def alive_cond(state):
        j, alive = state
        return jnp.logical_and(j >= 0, alive)

    def hcols(h):
        return slice(h * HEAD_DIM, (h + 1) * HEAD_DIM)

    if sn >= kblk:
        def new_body(state):
            j, _ = state
            start = pl.multiple_of(j * kblk, kblk)
            alive = visit(lambda h: _dot_nt(qh_ref[h], kn_ref[pl.ds(start, kblk), hcols(h)]),
                          lambda h, a: _dot(a, vn_ref[pl.ds(start, kblk), hcols(h)]), start + kiota < qpos)
            return j - 1, alive

        _, alive = lax.while_loop(alive_cond, new_body, ((qlocal + tq - 1) // kblk, True))
    else:
        def padded(ref, h):
            return jnp.concatenate([ref[:, hcols(h)], jnp.zeros((kblk - sn, HEAD_DIM), BF16)], axis=0)

        alive = visit(lambda h: _dot_nt(qh_ref[h], padded(kn_ref, h)), lambda h, a: _dot(a, padded(vn_ref, h)),
                      kiota < qpos)

    if has_cache:
        b = pl.program_id(0)

        def copies(j, slot):
            src = pl.ds(pl.multiple_of(j * kblk, kblk), kblk)
            return (pltpu.make_async_copy(kc_hbm.at[layer, b, :, :, src], kbuf.at[slot], sem.at[0, slot]),
                    pltpu.make_async_copy(vc_hbm.at[layer, b, :, :, src], vbuf.at[slot], sem.at[1, slot]))

        def start(j, slot):
            for cp in copies(j, slot):
                cp.start()

        def wait(j, slot):
            for cp in copies(j, slot):
                cp.wait()

        j0 = lc // kblk - 1

        @pl.when(alive)
        def _():
            start(j0, j0 % 2)

        def cache_body(state):
            j, _ = state
            slot = j % 2
            wait(j, slot)

            @pl.when(j > 0)
            def _():
                start(j - 1, 1 - slot)

            alive = visit(lambda h: _dot(qh_ref[h], kbuf[slot, h].astype(BF16)),
                          lambda h, a: _dot_nt(a, vbuf[slot, h].astype(BF16)), None)
            return j - 1, alive

        j_end, _ = lax.while_loop(alive_cond, cache_body, (j0, alive))

        @pl.when(jnp.logical_and(alive, j_end >= 0))
        def _():
            wait(j_end, j_end % 2)

    for h in heads:
        o_ref[:, hcols(h)] = acc_ref[h].astype(o_ref.dtype)


def _sb_call(bq, bk16, bv16, caches, *, batch, seq_q, tq, row0, layer=0, kblk=128):
    has_cache = caches is not None
    lc = caches[0].shape[-1] if has_cache else 0
    sn = seq_q
    assert kblk == LANES and lc % kblk == 0 and row0 % sn == 0 and sn % tq == 0
    assert sn % kblk == 0 or (sn < kblk and tq == sn)
    nqb = seq_q // tq
    rb0 = row0 // tq
    nb0 = row0 // sn
    in_specs = [pl.BlockSpec((tq, B_WIDTH), lambda b, i: (rb0 + b * nqb + i, 0)),
                pl.BlockSpec((sn, B_WIDTH), lambda b, i: (nb0 + b, 0)),
                pl.BlockSpec((sn, B_WIDTH), lambda b, i: (nb0 + b, 0))]
    args = [bq, bk16, bv16]
    scratch = [pltpu.VMEM((B_HEADS, tq, HEAD_DIM), BF16), pltpu.VMEM((B_HEADS, tq, LANES), F32),
               pltpu.VMEM((B_HEADS, tq, HEAD_DIM), F32)]
    if has_cache:
        in_specs += [pl.BlockSpec(memory_space=pl.ANY)] * 2
        args += list(caches)
        scratch += [pltpu.VMEM((2, B_HEADS, HEAD_DIM, kblk), F32), pltpu.VMEM((2, B_HEADS, HEAD_DIM, kblk), F32),
                    pltpu.SemaphoreType.DMA((2, 2))]
    kern = functools.partial(_sb_kernel, tq=tq, sn=sn, lc=lc, kblk=kblk, has_cache=has_cache, layer=layer)
    return pl.pallas_call(
        kern,
        grid=(batch, nqb),
        in_specs=in_specs,
        out_specs=pl.BlockSpec((tq, B_WIDTH), lambda b, i: (b * nqb + i, 0)),
        out_shape=jax.ShapeDtypeStruct((batch * seq_q, B_WIDTH), BF16),
        scratch_shapes=scratch,
        compiler_params=_params(("parallel", "arbitrary")),
        name="stick_breaking",
    )(*args)


def _ssd_kernel(z_ref, xbc_ref, misc_ref, cprev_ref, sprev_ref, cw_ref, cb_ref, dtb_ref, alog_ref, dsk_ref,
                ng_ref, y_ref, snew_ref, xpad_ref, st_ref, *, cs):
    ci = pl.program_id(1)
    nci = pl.num_programs(1)
    hpg = C_HEADS // C_GROUPS
    gw = hpg * C_HEAD_DIM

    @pl.when(ci == 0)
    def _():
        xpad_ref[5:8, :] = cprev_ref[0]
        for g in range(C_GROUPS):
            st_ref[g] = sprev_ref[0, g * hpg:(g + 1) * hpg].reshape(gw, D_STATE).T

    @pl.when(ci > 0)
    def _():
        xpad_ref[5:8, :] = xpad_ref[cs + 5:cs + 8, :]

    xpad_ref[8:8 + cs, :] = xbc_ref[...]
    xc = cb_ref[...] + xpad_ref[5:5 + cs, :] * cw_ref[0:1, :]
    for w in range(1, CONV_WIDTH):
        xc = xc + xpad_ref[5 + w:5 + w + cs, :] * cw_ref[w:w + 1, :]
    xc = _silu(xc)

    dt = _softplus(misc_ref[...] + dtb_ref[...])
    a = dt * (-jnp.exp(alog_ref[...]))
    r = lax.broadcasted_iota(I32, (cs, cs), 0)
    c = lax.broadcasted_iota(I32, (cs, cs), 1)
    tril = r >= c
    tril16 = jnp.where(tril, 1.0, 0.0).astype(BF16)
    a1, a2, a3 = _split3(a)
    a_cum = _dot(tril16, a1) + _dot(tril16, a2) + _dot(tril16, a3)
    pad = LANES - cs
    a_sq = a_cum if pad == 0 else jnp.concatenate([a_cum, jnp.zeros((pad, LANES), F32)], axis=0)
    a_cum_t = a_sq.T
    a_last = a_cum[cs - 1:cs, :]
    exp_a = jnp.exp(a_cum)
    dte = jnp.exp(a_last - a_cum)
    exp_last = jnp.exp(a_last)

    for g in range(C_GROUPS):
        bm = xc[:, C_D_INNER + g * D_STATE:C_D_INNER + (g + 1) * D_STATE].astype(BF16)
        cm = xc[:, C_D_INNER + (C_GROUPS + g) * D_STATE:C_D_INNER + (C_GROUPS + g + 1) * D_STATE].astype(BF16)
        cb = _dot_nt(cm, bm)
        st = st_ref[g]
        y_off = _dot(cm, st.astype(BF16))
        y_heads, x_heads, ea_heads, dec_heads = [], [], [], []
        for rr in range(hpg):
            h = g * hpg + rr
            ln = MISC_DT + h
            xs_h = xc[:, h * C_HEAD_DIM:(h + 1) * C_HEAD_DIM]
            xd = xs_h * dt[:, ln:ln + 1]
            seg = a_cum[:, ln:ln + 1] - a_cum_t[ln:ln + 1, 0:cs]
            decay = jnp.where(tril, jnp.exp(jnp.where(tril, seg, 0.0)), 0.0)
            y_heads.append(_dot((cb * decay).astype(BF16), xd.astype(BF16)))
            x_heads.append(xd * dte[:, ln:ln + 1])
            ea_heads.append(jnp.broadcast_to(exp_a[:, ln:ln + 1], (cs, C_HEAD_DIM)))
            dec_heads.append(jnp.broadcast_to(exp_last[:, ln:ln + 1], (1, C_HEAD_DIM)))
        y = jnp.concatenate(y_heads, axis=1) + y_off * jnp.concatenate(ea_heads, axis=1)
        st_new = st * jnp.concatenate(dec_heads, axis=1) + _dot_tn(bm, jnp.concatenate(x_heads, axis=1).astype(BF16))
        st_ref[g] = st_new

        gs = slice(g * gw, (g + 1) * gw)
        y = y + xc[:, gs] * dsk_ref[:, gs]
        y = y * _silu(z_ref[:, gs])
        y = y * lax.rsqrt(jnp.mean(y * y, axis=1, keepdims=True) + EPS) * ng_ref[:, gs]
        y_ref[:, gs] = y.astype(y_ref.dtype)

        @pl.when(ci == nci - 1)
        def _():
            snew_ref[0, g * hpg:(g + 1) * hpg] = st_new.T.reshape(hpg, C_HEAD_DIM, D_STATE)


def _ssd_call(p, cprev, sprev, cw, cb, dtb, alog, dsk, ng, *, batch, seq, cs, row0):
    nci = seq // cs
    rb0 = row0 // cs

    def pcol(width, off):
        return pl.BlockSpec((cs, width), lambda b, i: (rb0 + b * nci + i, off // width))

    def row(width, rows=1):
        return pl.BlockSpec((rows, width), lambda b, i: (0, 0))

    kern = functools.partial(_ssd_kernel, cs=cs)
    return pl.pallas_call(
        kern,
        grid=(batch, nci),
        in_specs=[pcol(C_D_INNER, COL_Z), pcol(C_CONV_DIM, COL_XBC), pcol(LANES, COL_MISC),
                  pl.BlockSpec((1, CONV_WIDTH - 1, C_CONV_DIM), lambda b, i: (b, 0, 0)),
                  pl.BlockSpec((1, C_HEADS, C_HEAD_DIM, D_STATE), lambda b, i: (b, 0, 0, 0)),
                  row(C_CONV_DIM, CONV_WIDTH), row(C_CONV_DIM), row(LANES), row(LANES),
                  row(C_D_INNER), row(C_D_INNER)],
        out_specs=[pl.BlockSpec((cs, C_D_INNER), lambda b, i: (b * nci + i, 0)),
                   pl.BlockSpec((1, C_HEADS, C_HEAD_DIM, D_STATE), lambda b, i: (b, 0, 0, 0))],
        out_shape=[jax.ShapeDtypeStruct((batch * seq, C_D_INNER), BF16),
                   jax.ShapeDtypeStruct((batch, C_HEADS, C_HEAD_DIM, D_STATE), F32)],
        scratch_shapes=[pltpu.VMEM((cs + 8, C_CONV_DIM), F32),
                        pltpu.VMEM((C_GROUPS, D_STATE, C_HEADS // C_GROUPS * C_HEAD_DIM), F32)],
        compiler_params=_params(("parallel", "arbitrary")),
        name="ssd_mixer",
    )(p, p, p, cprev, sprev, cw, cb, dtb, alog, dsk, ng)


def _outproj_kernel(x_ref, yap_ref, ybp_ref, ycp_ref, yas_ref, ybs_ref, ycs_ref, wa_ref, wb_ref, wc_ref, o_ref, *,
                    prompt_tiles):
    def mix(ya_ref, yb_ref, yc_ref):
        o_ref[...] = (x_ref[...] + _dot(ya_ref[...], wa_ref[...]) + _dot(yb_ref[...], wb_ref[...])
                      + _dot(yc_ref[...], wc_ref[...]))

    is_prompt = pl.program_id(0) < prompt_tiles

    @pl.when(is_prompt)
    def _():
        mix(yap_ref, ybp_ref, ycp_ref)

    @pl.when(jnp.logical_not(is_prompt))
    def _():
        mix(yas_ref, ybs_ref, ycs_ref)


def _outproj(x, y_prompt, y_sample, w):
    t, d = x.shape
    tp, ts = y_prompt[0].shape[0], y_sample[0].shape[0]
    tm = _tile(math.gcd(tp, ts), 256)
    npt = tp // tm

    def tok(width):
        return pl.BlockSpec((tm, width), lambda i: (i, 0))

    def ptok(width):
        return pl.BlockSpec((tm, width), lambda i: (jnp.minimum(i, npt - 1), 0))

    def stok(width):
        return pl.BlockSpec((tm, width), lambda i: (jnp.maximum(i - npt, 0), 0))

    widths = (A_WIDTH, B_WIDTH, C_D_INNER)
    return pl.pallas_call(
        functools.partial(_outproj_kernel, prompt_tiles=npt),
        grid=(t // tm,),
        in_specs=[tok(d)] + [ptok(wd) for wd in widths] + [stok(wd) for wd in widths] + [
            pl.BlockSpec((A_WIDTH, d), lambda i: (0, 0)),
            pl.BlockSpec((B_WIDTH, d), lambda i: (1, 0)),
            pl.BlockSpec((C_D_INNER, d), lambda i: (1, 0))],
        out_specs=tok(d),
        out_shape=jax.ShapeDtypeStruct((t, d), F32),
        compiler_params=_params(("parallel",)),
        name="out_proj",
    )(x, *y_prompt, *y_sample, w, w, w)


def _ffn_kernel(x_ref, g_ref, wg_ref, wu_ref, wd_ref, o_ref, h_ref):
    @pl.when(pl.program_id(1) == 0)
    def _():
        x = x_ref[...]
        ms = jnp.mean(x * x, axis=-1, keepdims=True)
        h_ref[...] = (x * lax.rsqrt(ms + EPS) * g_ref[...]).astype(BF16)
        o_ref[...] = x

    h = h_ref[...]
    act = _silu(_dot(h, wg_ref[...])) * _dot(h, wu_ref[...])
    o_ref[...] += _dot(act.astype(BF16), wd_ref[...])


def _ffn(x, g, wg, wu, wd):
    t, d = x.shape
    f = wg.shape[1]
    tm = _tile(t, 1024)
    tf = 512
    once = pl.Buffered(1)
    return pl.pallas_call(
        _ffn_kernel,
        grid=(t // tm, f // tf),
        in_specs=[pl.BlockSpec((tm, d), lambda i, j: (i, 0), pipeline_mode=once),
                  pl.BlockSpec((1, d), lambda i, j: (0, 0)),
                  pl.BlockSpec((d, tf), lambda i, j: (0, j)),
                  pl.BlockSpec((d, tf), lambda i, j: (0, j)),
                  pl.BlockSpec((tf, d), lambda i, j: (j, 0))],
        out_specs=pl.BlockSpec((tm, d), lambda i, j: (i, 0), pipeline_mode=once),
        out_shape=jax.ShapeDtypeStruct((t, d), F32),
        scratch_shapes=[pltpu.VMEM((tm, d), BF16)],
        compiler_params=_params(("parallel", "arbitrary")),
        name="swiglu",
    )(x, g, wg, wu, wd)


def _rope_rows(pos):
    half = ROPE_DIM // 2
    inv = 1.0 / (ROPE_THETA ** (jnp.arange(0, ROPE_DIM, 2, dtype=F32) / ROPE_DIM))
    ang = pos.astype(F32)[:, None] * inv[None, :]
    cos, sin = jnp.cos(ang), jnp.sin(ang)
    n = pos.shape[0]
    one = jnp.ones((n, HEAD_DIM - ROPE_DIM), F32)
    zero = jnp.zeros((n, HEAD_DIM - ROPE_DIM), F32)
    zh = jnp.zeros((n, half), F32)
    rc = jnp.concatenate([cos, cos, one], axis=1)
    rs1 = jnp.concatenate([zh, sin, zero], axis=1)
    rs2 = jnp.concatenate([-sin, zh, zero], axis=1)
    return tuple(jnp.tile(a, (1, LANES // HEAD_DIM)) for a in (rc, rs1, rs2))


def _w_in_segments():
    offs = [0]
    for n in IN_SIZES:
        offs.append(offs[-1] + n)
    dst = (COL_AQ, COL_AK, COL_AV, COL_IQ, COL_MISC + MISC_IK, COL_MISC + MISC_IW, COL_BQ, COL_BK, COL_BV,
           COL_Z, COL_XBC, COL_MISC + MISC_DT)
    return [(offs[i], dst[i], IN_SIZES[i]) for i in range(len(IN_SIZES))]


def _regroup_kernel(w_ref, o_ref):
    for src, dst, width in _w_in_segments():
        o_ref[:, dst:dst + width] = w_ref[0, :, src:src + width].astype(BF16)
    used = COL_MISC + MISC_DT + C_HEADS
    o_ref[:, used:] = jnp.zeros((o_ref.shape[0], P_COLS - used), BF16)


def _rearrange_w_in(w, layer):
    _, d, n = w.shape
    tr = _tile(d, 256)
    return pl.pallas_call(
        _regroup_kernel,
        grid=(d // tr,),
        in_specs=[pl.BlockSpec((1, tr, n), lambda i: (layer, i, 0))],
        out_specs=pl.BlockSpec((tr, P_COLS), lambda i: (i, 0)),
        out_shape=jax.ShapeDtypeStruct((d, P_COLS), BF16),
        compiler_params=_params(("parallel",)),
        name="regroup_w_in",
    )(w)


def _lane_row(v, off):
    return jnp.zeros((1, LANES), F32).at[0, off:off + v.shape[0]].set(v)


def _gain_row(g):
    return jnp.tile(g, LANES // HEAD_DIM)[None, :]


def kernel(x_prompt, x_sample, cache_a_k, cache_a_v, cache_a_idx_k, cache_b_k, cache_b_v, state_c_ssm, state_c_conv, norm1_g, w_in, a_q_norm_g, a_k_norm_g, idx_k_norm_g, b_q_norm_g, b_k_norm_g, c_conv_w, c_conv_b, c_dt_bias, c_a_log, c_d, c_norm_g, w_out, norm2_g, w_gate, w_up, w_down):
    bp, sp, d = x_prompt.shape
    bs, ss, _ = x_sample.shape
    depth = w_in.shape[0]
    past = cache_a_k.shape[2]
    tp, ts = bp * sp, bs * ss
    tq_p = 128
    cs_p = 128 if sp % 128 == 0 else sp
    assert ss >= CONV_WIDTH - 1 and sp >= CONV_WIDTH - 1

    pos = jnp.concatenate([jnp.tile(jnp.arange(sp, dtype=I32), bp),
                           jnp.tile(past + jnp.arange(ss, dtype=I32), bs)])
    rc, rs1, rs2 = _rope_rows(pos)
    x = jnp.concatenate([x_prompt.reshape(tp, d), x_sample.reshape(ts, d)], axis=0)
    zero_conv = jnp.zeros((bp, CONV_WIDTH - 1, C_CONV_DIM), F32)
    zero_ssm = jnp.zeros((bp, C_HEADS, C_HEAD_DIM, D_STATE), F32)

    p_states, s_states = [], []
    for l in range(depth):
        p = _proj(x, norm1_g[l][None, :], _rearrange_w_in(w_in, l))
        gik = jnp.concatenate([idx_k_norm_g[l], jnp.ones((LANES - IDX_DIM,), F32)])[None, :]
        (aq, ak, iq, ik, iw, bq, bk, bk16, bv16) = _post(
            p, rc, rs1, rs2, _gain_row(a_q_norm_g[l]), _gain_row(a_k_norm_g[l]), gik,
            _gain_row(b_q_norm_g[l]), _gain_row(b_k_norm_g[l]))

        a_caches = (jnp.transpose(cache_a_idx_k, (0, 1, 3, 2)), jnp.transpose(cache_a_k, (0, 1, 3, 4, 2)),
                    jnp.transpose(cache_a_v, (0, 1, 3, 4, 2)))
        ya_p = _dsap_call(aq, iq, iw, ik, ak, p, None, batch=bp, seq_q=sp, tq=tq_p, row0=0)
        ya_s = _dsap_call(aq, iq, iw, ik, ak, p, a_caches, batch=bs, seq_q=ss, tq=ss, row0=tp, layer=l)
        b_caches = (jnp.transpose(cache_b_k, (0, 1, 3, 4, 2)), jnp.transpose(cache_b_v, (0, 1, 3, 4, 2)))
        yb_p = _sb_call(bq, bk16, bv16, None, batch=bp, seq_q=sp, tq=tq_p, row0=0)
        yb_s = _sb_call(bq, bk16, bv16, b_caches, batch=bs, seq_q=ss, tq=ss, row0=tp, layer=l)
        cw, cb = c_conv_w[l], c_conv_b[l][None, :]
        dtb, alog = _lane_row(c_dt_bias[l], MISC_DT), _lane_row(c_a_log[l], MISC_DT)
        dsk, ng = jnp.repeat(c_d[l], C_HEAD_DIM)[None, :], c_norm_g[l][None, :]
        yc_p, ssm_p = _ssd_call(p, zero_conv, zero_ssm, cw, cb, dtb, alog, dsk, ng,
                                batch=bp, seq=sp, cs=cs_p, row0=0)
        yc_s, ssm_s = _ssd_call(p, state_c_conv[l], state_c_ssm[l], cw, cb, dtb, alog, dsk, ng,
                                batch=bs, seq=ss, cs=ss, row0=tp)

        x = _outproj(x, (ya_p, yb_p, yc_p), (ya_s, yb_s, yc_s), w_out[l].astype(BF16))
        x = _ffn(x, norm2_g[l][None, :], w_gate[l].astype(BF16), w_up[l].astype(BF16), w_down[l].astype(BF16))

        def state(lo, hi, b, s, ssm):
            av = p[lo:hi, COL_AV:COL_AV + LANES]
            bv = p[lo:hi, COL_BV:COL_BV + B_WIDTH]
            conv = jnp.stack([p[lo + (i + 1) * s - (CONV_WIDTH - 1):lo + (i + 1) * s, COL_XBC:COL_XBC + C_CONV_DIM]
                              for i in range(b)])
            return (ak[lo:hi].reshape(b, s, A_KV_HEADS, HEAD_DIM), av.reshape(b, s, A_KV_HEADS, HEAD_DIM),
                    ik[lo:hi].reshape(b, s, IDX_DIM), bk[lo:hi].reshape(b, s, B_HEADS, HEAD_DIM),
                    bv.reshape(b, s, B_HEADS, HEAD_DIM), ssm, conv)

        p_states.append(state(0, tp, bp, sp, ssm_p))
        s_states.append(state(tp, tp + ts, bs, ss, ssm_s))

    p_out = [jnp.stack(zs) for zs in zip(*p_states)]
    s_out = [jnp.stack(zs) for zs in zip(*s_states)]
    return (x[:tp].reshape(bp, sp, d), x[tp:].reshape(bs, ss, d), *p_out, *s_out)
```

```python
import functools
import math

import jax
import jax.numpy as jnp
from jax import lax
from jax.experimental import pallas as pl
from jax.experimental.pallas import tpu as pltpu

F32 = jnp.float32
BF16 = jnp.bfloat16
I32 = jnp.int32

D_MODEL = 2048
EPS = 1e-6
CHUNK = 64
HEAD_DIM = 64
ROPE_DIM = HEAD_DIM // 4
ROPE_THETA = 500000.0
A_HEADS = 8
A_KV_HEADS = 2
A_WIDTH = A_HEADS * HEAD_DIM
IDX_HEADS = 16
IDX_DIM = 64
TOPK_MAX = 256
B_HEADS = 8
B_WIDTH = B_HEADS * HEAD_DIM
C_D_INNER = D_MODEL // 2
C_HEAD_DIM = 64
C_HEADS = C_D_INNER // C_HEAD_DIM
C_GROUPS = 4
D_STATE = 128
CONV_WIDTH = 4
C_CONV_DIM = C_D_INNER + 2 * C_GROUPS * D_STATE
IN_SIZES = (A_WIDTH, A_KV_HEADS * HEAD_DIM, A_KV_HEADS * HEAD_DIM, IDX_HEADS * IDX_DIM, IDX_DIM, IDX_HEADS,
            B_WIDTH, B_WIDTH, B_WIDTH, C_D_INNER, C_CONV_DIM, C_HEADS)

LANES = 128
VMEM_LIMIT = 48 * 1024 * 1024

COL_XBC, COL_IQ, COL_Z, COL_AQ, COL_BQ, COL_BK, COL_BV, COL_AK, COL_AV, COL_MISC = (
    0, 2048, 3072, 4096, 4608, 5120, 5632, 6144, 6272, 6400)
P_COLS = 6656
MISC_IK, MISC_IW, MISC_DT = 0, 64, 80

NEG_BIG = -1e30
INT_MIN = -2 ** 31
KEY_POS_INF = 0x7F800000
KEY_NEG_INF = INT_MIN + 0x7FFFFF
SB_DEAD = -104.0


def _params(sem):
    return pltpu.CompilerParams(dimension_semantics=sem, vmem_limit_bytes=VMEM_LIMIT)


def _tile(n, pref):
    t = min(n, pref)
    while n % t:
        t -= 8
    return t


def _dot_nt(a, b):
    return lax.dot_general(a, b, (((1,), (1,)), ((), ())), preferred_element_type=F32)


def _dot_tn(a, b):
    return lax.dot_general(a, b, (((0,), (0,)), ((), ())), preferred_element_type=F32)


def _dot(a, b):
    return jnp.dot(a, b, preferred_element_type=F32)


def _split3(x):
    h1 = x.astype(BF16)
    r1 = x - h1.astype(F32)
    h2 = r1.astype(BF16)
    h3 = (r1 - h2.astype(F32)).astype(BF16)
    return h1, h2, h3


def _softplus(x):
    return jnp.maximum(x, 0.0) + jnp.log1p(jnp.exp(-jnp.abs(x)))


def _silu(x):
    return x * (1.0 / (1.0 + jnp.exp(-x)))


def _proj_kernel(x_ref, g_ref, w_ref, o_ref, h_ref):
    @pl.when(pl.program_id(1) == 0)
    def _():
        x = x_ref[...]
        ms = jnp.mean(x * x, axis=-1, keepdims=True)
        h_ref[...] = (x * lax.rsqrt(ms + EPS) * g_ref[...]).astype(BF16)

    o_ref[...] = _dot(h_ref[...], w_ref[...])


def _proj(x, g, w):
    t, d = x.shape
    n = w.shape[1]
    tm = _tile(t, 1024)
    tn = 512
    return pl.pallas_call(
        _proj_kernel,
        grid=(t // tm, n // tn),
        in_specs=[pl.BlockSpec((tm, d), lambda i, j: (i, 0)),
                  pl.BlockSpec((1, d), lambda i, j: (0, 0)),
                  pl.BlockSpec((d, tn), lambda i, j: (0, j))],
        out_specs=pl.BlockSpec((tm, tn), lambda i, j: (i, j)),
        out_shape=jax.ShapeDtypeStruct((t, n), F32),
        scratch_shapes=[pltpu.VMEM((tm, d), BF16)],
        compiler_params=_params(("parallel", "arbitrary")),
        name="in_proj",
    )(x, g, w)


def _post_kernel(aq_r, bq_r, bk_r, bv_r, iq_r, ak_r, misc_r, rc_r, rs1_r, rs2_r,
                 gaq_r, gak_r, gik_r, gbq_r, gbk_r,
                 aq_o, ak_o, iq_o, ik_o, iw_o, bq_o, bk_o, bk16_o, bv16_o):
    r = lax.broadcasted_iota(I32, (LANES, LANES), 0) // HEAD_DIM
    c = lax.broadcasted_iota(I32, (LANES, LANES), 1) // HEAD_DIM
    segm = jnp.where(r == c, 1.0, 0.0).astype(BF16)
    rc, rs1, rs2 = rc_r[...], rs1_r[...], rs2_r[...]

    def headnorm(xc, g_row):
        h1, h2, h3 = _split3(xc * xc)
        ss = _dot(h1, segm) + _dot(h2, segm) + _dot(h3, segm)
        return xc * lax.rsqrt(ss * (1.0 / HEAD_DIM) + EPS) * g_row

    def rope(xc):
        return xc * rc + pltpu.roll(xc, 8, 1) * rs1 + pltpu.roll(xc, LANES - 8, 1) * rs2

    for k in range(A_WIDTH // LANES):
        sl = slice(k * LANES, (k + 1) * LANES)
        aq_o[:, sl] = (rope(headnorm(aq_r[:, sl], gaq_r[...])) * (HEAD_DIM ** -0.5)).astype(BF16)
        bq_o[:, sl] = (headnorm(bq_r[:, sl], gbq_r[...]) * (HEAD_DIM ** -0.5)).astype(BF16)
        bk = headnorm(bk_r[:, sl], gbk_r[...])
        bk_o[:, sl] = bk
        bk16_o[:, sl] = bk.astype(BF16)
        bv16_o[:, sl] = bv_r[:, sl].astype(BF16)
    for k in range(IDX_HEADS * IDX_DIM // LANES):
        sl = slice(k * LANES, (k + 1) * LANES)
        iq_o[:, sl] = rope(iq_r[:, sl]).astype(BF16)
    ak_o[...] = rope(headnorm(ak_r[...], gak_r[...]))
    misc = misc_r[...]
    ik_o[...] = rope(headnorm(misc, gik_r[...]))[:, MISC_IK:MISC_IK + IDX_DIM]
    iw_o[...] = misc[:, MISC_IW:MISC_IW + IDX_HEADS] * (IDX_HEADS ** -0.5 * IDX_DIM ** -0.5)


def _post(p, rc, rs1, rs2, gaq, gak, gik, gbq, gbk):
    t = p.shape[0]
    tm = _tile(t, 256)

    def col(width, off):
        return pl.BlockSpec((tm, width), lambda i: (i, off // width))

    def row(width):
        return pl.BlockSpec((1, width), lambda i: (0, 0))

    def out(width):
        return pl.BlockSpec((tm, width), lambda i: (i, 0))

    widths_dtypes = [(A_WIDTH, BF16), (LANES, F32), (IDX_HEADS * IDX_DIM, BF16), (IDX_DIM, F32),
                     (IDX_HEADS, F32), (B_WIDTH, BF16), (B_WIDTH, F32), (B_WIDTH, BF16), (B_WIDTH, BF16)]
    return pl.pallas_call(
        _post_kernel,
        grid=(t // tm,),
        in_specs=[col(A_WIDTH, COL_AQ), col(B_WIDTH, COL_BQ), col(B_WIDTH, COL_BK), col(B_WIDTH, COL_BV),
                  col(IDX_HEADS * IDX_DIM, COL_IQ), col(LANES, COL_AK), col(LANES, COL_MISC),
                  out(LANES), out(LANES), out(LANES),
                  row(LANES), row(LANES), row(LANES), row(LANES), row(LANES)],
        out_specs=[out(w) for w, _ in widths_dtypes],
        out_shape=[jax.ShapeDtypeStruct((t, w), dt) for w, dt in widths_dtypes],
        compiler_params=_params(("parallel",)),
        name="head_post",
    )(p, p, p, p, p, p, p, rc, rs1, rs2, gaq, gak, gik, gbq, gbk)


CACHE_CHUNK = 512


def _dsap_kernel(*refs, tq, sn, lc, topk, kblk, has_cache, layer):
    nb = LANES // tq
    if has_cache:
        (aq_ref, iq_ref, iw_ref, ikn_ref, akn_ref, avn_ref, ikc_hbm, akc_hbm, avc_hbm, o_ref,
         key_ref, bias_ref, ik16_ref, ak16_ref, vt_ref, iqh_ref, aqh_ref, m_ref, acc_ref,
         ikbuf, akbuf, avbuf, sem) = refs
    else:
        (aq_ref, iq_ref, iw_ref, ikn_ref, akn_ref, avn_ref, o_ref,
         key_ref, bias_ref, ik16_ref, ak16_ref, vt_ref, iqh_ref, aqh_ref, m_ref, acc_ref) = refs
    grp = pl.program_id(0)
    qi = pl.program_id(1)
    n_keys = lc + sn
    lp = ik16_ref.shape[0]
    rep = A_HEADS // A_KV_HEADS
    ch = CACHE_CHUNK

    @pl.when(qi == 0)
    def _():
        def put(row, j, ik, k0, k1, v_t):
            n = ik.shape[0]
            cols = slice(j * IDX_DIM, (j + 1) * IDX_DIM)
            ik16_ref[pl.ds(row, n), cols] = ik.astype(BF16)
            ak16_ref[0, pl.ds(row, n), cols] = k0.astype(BF16)
            ak16_ref[1, pl.ds(row, n), cols] = k1.astype(BF16)
            vt = v_t.astype(BF16)
            for e in range(n // kblk):
                for g in range(A_KV_HEADS):
                    vt_ref[g, row // kblk + e, 0:HEAD_DIM, j * kblk:(j + 1) * kblk] = (
                        vt[g * HEAD_DIM:(g + 1) * HEAD_DIM, e * kblk:(e + 1) * kblk])

        if has_cache:
            nc = lc // ch

            def copies(j, c, slot):
                b = grp * nb + j
                src = pl.ds(pl.multiple_of(c * ch, ch), ch)
                return (pltpu.make_async_copy(ikc_hbm.at[layer, b, :, src], ikbuf.at[slot], sem.at[0, slot]),
                        pltpu.make_async_copy(akc_hbm.at[layer, b, :, :, src], akbuf.at[slot], sem.at[1, slot]),
                        pltpu.make_async_copy(avc_hbm.at[layer, b, :, :, src], avbuf.at[slot], sem.at[2, slot]))

            def start(j, c, slot):
                for cp in copies(j, c, slot):
                    cp.start()

            start(0, 0, 0)
            for j in range(nb):
                def chunk_body(c, carry, j=j):
                    slot = c % 2
                    for cp in copies(j, c, slot):
                        cp.wait()

                    @pl.when(c + 1 < nc)
                    def _():
                        start(j, c + 1, 1 - slot)

                    if j + 1 < nb:
                        @pl.when(c + 1 == nc)
                        def _():
                            start(j + 1, 0, 0)

                    kk = akbuf[slot].reshape(LANES, ch).T
                    ik = jnp.concatenate([ikbuf[slot], jnp.zeros((LANES - IDX_DIM, ch), F32)], axis=0).T
                    put(pl.multiple_of(c * ch, ch), j, ik[:, 0:IDX_DIM], kk[:, 0:HEAD_DIM], kk[:, HEAD_DIM:LANES],
                        avbuf[slot].reshape(LANES, ch))
                    return carry

                lax.fori_loop(0, nc, chunk_body, 0)

        for j in range(nb):
            if sn % ch == 0:
                def new_body(c, carry, j=j):
                    src = pl.ds(pl.multiple_of(j * sn + c * ch, ch), ch)
                    ak = akn_ref[src, :]
                    put(pl.multiple_of(lc + c * ch, ch), j, ikn_ref[src, :], ak[:, 0:HEAD_DIM], ak[:, HEAD_DIM:LANES],
                        avn_ref[src, :].T)
                    return carry

                lax.fori_loop(0, sn // ch, new_body, 0)
            else:
                def padded(ref):
                    return jnp.concatenate([ref[j * sn:(j + 1) * sn, :],
                                            jnp.zeros((kblk - sn, ref.shape[-1]), F32)], axis=0)

                ak = padded(akn_ref)
                put(lc, j, padded(ikn_ref), ak[:, 0:HEAD_DIM], ak[:, HEAD_DIM:LANES], padded(avn_ref).T)
        for g in range(A_KV_HEADS):
            vt_ref[g, :, HEAD_DIM:LANES, :] = jnp.ones((lp // kblk, LANES - HEAD_DIM, nb * kblk), BF16)

    qstart = lc + qi * tq
    kmax = jnp.minimum(n_keys, ((qstart + tq - 1) // CHUNK + 1) * CHUNK)
    nkb = (kmax + kblk - 1) // kblk
    lane = lax.broadcasted_iota(I32, (1, LANES), 1)
    qpos = qstart + lane % tq
    krow = lax.broadcasted_iota(I32, (kblk, 1), 0)
    lane_group = lane // tq
    two = 2 * LANES

    def head_major(dst, src, nheads):
        x_t = src[...].astype(F32).T
        for h in range(nheads):
            xh = x_t[h * HEAD_DIM:(h + 1) * HEAD_DIM, :]
            if nb > 1:
                xh = jnp.concatenate([jnp.where(lane_group == j, xh, 0.0) for j in range(nb)], axis=0)
            dst[:, h * LANES:(h + 1) * LANES] = xh.astype(BF16)

    head_major(iqh_ref, iq_ref, IDX_HEADS)
    head_major(aqh_ref, aq_ref, A_HEADS)
    iw_t = jnp.concatenate([iw_ref[...], jnp.zeros((LANES, LANES - IDX_HEADS), F32)], axis=1).T

    def score_body(kb, carry):
        start = pl.multiple_of(kb * kblk, kblk)
        ikb = ik16_ref[pl.ds(start, kblk), :]
        acc = jnp.zeros((kblk, LANES), F32)
        for pr in range(IDX_HEADS // 2):
            logit = _dot(ikb, iqh_ref[:, pr * two:(pr + 1) * two])
            for e in range(2):
                h = 2 * pr + e
                acc = acc + iw_t[h:h + 1, :] * jnp.maximum(logit[:, e * LANES:(e + 1) * LANES], 0.0)
        kpos = start + krow
        adm = jnp.logical_and(kpos // CHUNK <= qpos // CHUNK, kpos < n_keys)
        bits = pltpu.bitcast(jnp.where(adm, acc, -jnp.inf), I32)
        bits = jnp.where(bits == INT_MIN, 0, bits)
        key_ref[kb] = bits ^ ((bits >> 31) & 0x7FFFFFFF)
        return carry

    lax.fori_loop(0, nkb, score_body, 0)

    def count(pred):
        def body(kb, acc):
            hit = jnp.where(pred(key_ref[kb], kb), 1.0, 0.0)
            return acc + jnp.sum(hit.reshape(kblk // 64, 64, LANES), axis=0)
        acc = lax.fori_loop(0, nkb, body, jnp.zeros((64, LANES), F32))
        return jnp.sum(acc, axis=0, keepdims=True)

    kf = float(topk)

    def bit_body(i, thr):
        cand = thr + (jnp.int32(1) << (31 - i))
        cnt = count(lambda key, kb: key >= cand)
        return jnp.where(cnt >= kf, cand, thr)

    thr = lax.fori_loop(0, 32, bit_body, jnp.full((1, LANES), INT_MIN, I32))
    n_gt = count(lambda key, kb: key > thr)
    n_ge = count(lambda key, kb: key >= thr)
    need = kf - n_gt

    def tie_search():
        def jbody(i, jl):
            cand = jl + (jnp.int32(1) << (14 - i))
            cnt = count(lambda key, kb: jnp.logical_and(key == thr, kb * kblk + krow < cand))
            return jnp.where(cnt <= need, cand, jl)
        return lax.fori_loop(0, 15, jbody, jnp.zeros((1, LANES), I32))

    jlim = lax.cond(jnp.max(n_ge) > kf, tie_search, lambda: jnp.full((1, LANES), 2 ** 30, I32))

    def bias_body(kb, carry):
        key = key_ref[kb]
        kpos = kb * kblk + krow
        sel = jnp.logical_or(key > thr, jnp.logical_and(key == thr, kpos < jlim))
        sel = jnp.logical_and(sel, jnp.logical_and(key > KEY_NEG_INF, key < KEY_POS_INF))
        bias_ref[kb] = jnp.where(sel, 0.0, NEG_BIG)
        return carry

    lax.fori_loop(0, nkb, bias_body, 0)

    m_ref[...] = jnp.full(m_ref.shape, -3e38, F32)
    acc_ref[...] = jnp.zeros(acc_ref.shape, F32)
    pair_group = (lax.broadcasted_iota(I32, (1, 2 * LANES), 1) % LANES) // tq

    def att_body(kb, carry):
        start = pl.multiple_of(kb * kblk, kblk)
        bias = bias_ref[kb]
        m_all = m_ref[...]
        kks = [ak16_ref[g, pl.ds(start, kblk), :] for g in range(A_KV_HEADS)]
        s_ts = [_dot(kks[h0 // rep], aqh_ref[:, h0 * LANES:(h0 + 2) * LANES]) for h0 in range(0, A_HEADS, 2)]
        ps, alphas, m_rows = [], [], []
        for h in range(A_HEADS):
            x = s_ts[h // 2][:, (h % 2) * LANES:(h % 2 + 1) * LANES] + bias
            part = jnp.max(x.reshape(kblk // 64, 64, LANES), axis=0)
            bmax = jnp.max(part, axis=0, keepdims=True)
            m_old = m_all[h:h + 1, :]
            m_new = jnp.maximum(m_old, bmax)
            alphas.append(jnp.exp(m_old - m_new))
            ps.append(jnp.exp(x - m_new).astype(BF16))
            m_rows.append(m_new)
        m_ref[...] = jnp.concatenate(m_rows, axis=0)
        upds = []
        for h0 in range(0, A_HEADS, 2):
            pp = jnp.concatenate(ps[h0:h0 + 2], axis=1)
            if nb > 1:
                pp = jnp.concatenate([jnp.where(pair_group == j, pp, jnp.zeros_like(pp)) for j in range(nb)], axis=0)
            upds.append(_dot(vt_ref[h0 // rep, kb], pp))
        for h in range(A_HEADS):
            acc_ref[h] = acc_ref[h] * alphas[h] + upds[h // 2][:, (h % 2) * LANES:(h % 2 + 1) * LANES]
        return carry

    lax.fori_loop(0, nkb, att_body, 0)

    outs = []
    for h in range(A_HEADS):
        a = acc_ref[h]
        outs.append(a[0:HEAD_DIM, :] / a[HEAD_DIM:LANES, :])
    o_ref[...] = jnp.concatenate(outs, axis=0).T.astype(o_ref.dtype)


def _dsap_call(aq, iq, iw, ik, ak, p, caches, *, batch, seq_q, tq, row0, layer=0, kblk=256):
    has_cache = caches is not None
    lc = caches[0].shape[-1] if has_cache else 0
    sn = seq_q
    nb = LANES // tq
    n_keys = lc + sn
    lp = -(-n_keys // kblk) * kblk
    assert LANES % tq == 0 and batch % nb == 0 and sn % tq == 0 and row0 % (nb * sn) == 0 and row0 % LANES == 0
    assert lc % (2 * CACHE_CHUNK) == 0 and CACHE_CHUNK % kblk == 0
    assert (sn % CACHE_CHUNK == 0 and lp == n_keys) or (sn < kblk and lp == lc + kblk)
    groups = batch // nb
    nqb = seq_q // tq
    assert nb == 1 or nqb == 1
    rb0 = row0 // LANES
    nb0 = row0 // (nb * sn)
    topk = min(TOPK_MAX, n_keys // 4)
    nkb_max = lp // kblk

    def qspec(width):
        return pl.BlockSpec((LANES, width), lambda g, i: (rb0 + g * nqb + i, 0))

    def nspec(width, colblk=0):
        return pl.BlockSpec((nb * sn, width), lambda g, i: (nb0 + g, colblk))

    in_specs = [qspec(A_WIDTH), qspec(IDX_HEADS * IDX_DIM), qspec(IDX_HEADS),
                nspec(IDX_DIM), nspec(LANES), nspec(LANES, COL_AV // LANES)]
    args = [aq, iq, iw, ik, ak, p]
    scratch = [pltpu.VMEM((nkb_max, kblk, LANES), I32),
               pltpu.VMEM((nkb_max, kblk, LANES), F32),
               pltpu.VMEM((lp, nb * IDX_DIM), BF16),
               pltpu.VMEM((A_KV_HEADS, lp, nb * HEAD_DIM), BF16),
               pltpu.VMEM((A_KV_HEADS, nkb_max, LANES, nb * kblk), BF16),
               pltpu.VMEM((nb * IDX_DIM, IDX_HEADS * LANES), BF16),
               pltpu.VMEM((nb * HEAD_DIM, A_HEADS * LANES), BF16),
               pltpu.VMEM((A_HEADS, LANES), F32),
               pltpu.VMEM((A_HEADS, LANES, LANES), F32)]
    if has_cache:
        in_specs += [pl.BlockSpec(memory_space=pl.ANY)] * 3
        args += list(caches)
        scratch += [pltpu.VMEM((2, IDX_DIM, CACHE_CHUNK), F32),
                    pltpu.VMEM((2, A_KV_HEADS, HEAD_DIM, CACHE_CHUNK), F32),
                    pltpu.VMEM((2, A_KV_HEADS, HEAD_DIM, CACHE_CHUNK), F32),
                    pltpu.SemaphoreType.DMA((3, 2))]
    kern = functools.partial(_dsap_kernel, tq=tq, sn=sn, lc=lc, topk=topk, kblk=kblk, has_cache=has_cache,
                             layer=layer)
    return pl.pallas_call(
        kern,
        grid=(groups, nqb),
        in_specs=in_specs,
        out_specs=pl.BlockSpec((LANES, A_WIDTH), lambda g, i: (g * nqb + i, 0)),
        out_shape=jax.ShapeDtypeStruct((batch * seq_q, A_WIDTH), BF16),
        scratch_shapes=scratch,
        compiler_params=_params(("parallel", "arbitrary")),
        name="dsa_attention",
    )(*args)


def _sb_kernel(*refs, tq, sn, lc, kblk, has_cache, layer):
    if has_cache:
        q_ref, kn_ref, vn_ref, kc_hbm, vc_hbm, o_ref, qh_ref, carry_ref, acc_ref, kbuf, vbuf, sem = refs
    else:
        q_ref, kn_ref, vn_ref, o_ref, qh_ref, carry_ref, acc_ref = refs
    qlocal = pl.program_id(1) * tq
    qpos = qlocal + lax.broadcasted_iota(I32, (tq, 1), 0)
    kiota = lax.broadcasted_iota(I32, (1, kblk), 1)
    r = lax.broadcasted_iota(I32, (kblk, kblk), 0)
    c = lax.broadcasted_iota(I32, (kblk, kblk), 1)
    upper = jnp.where(r > c, 1.0, 0.0).astype(BF16)
    heads = range(B_HEADS)
    for h in heads:
        qh_ref[h] = q_ref[:, h * HEAD_DIM:(h + 1) * HEAD_DIM]
    carry_ref[...] = jnp.zeros(carry_ref.shape, F32)
    acc_ref[...] = jnp.zeros(acc_ref.shape, F32)

    def visit(score, apply, causal):
        zs = [score(h) for h in heads]
        sps = [_softplus(z) for z in zs]
        lks = [-sp if causal is None else jnp.where(causal, -sp, 0.0) for sp in sps]
        carries = [carry_ref[h] for h in heads]
        laters = []
        for h in heads:
            l1, l2, l3 = _split3(lks[h])
            laters.append(carries[h] + (_dot(l1, upper) + _dot(l2, upper) + _dot(l3, upper)))
        weights = [jnp.exp((zs[h] - sps[h]) + laters[h]) for h in heads]
        if causal is not None:
            weights = [jnp.where(causal, a, 0.0) for a in weights]
        upds = [apply(h, weights[h].astype(BF16)) for h in heads]
        new = [carries[h] + jnp.sum(lks[h], axis=1, keepdims=True) for h in heads]
        top = new[0]
        for h in heads:
            acc_ref[h] += upds[h]
            carry_ref[h] = new[h]
            top = jnp.maximum(top, new[h])
        return jnp.max(top) > SB_DEAD

    def alive_cond(state):
        j, alive = state
        return jnp.logical_and(j >= 0, alive)

    def hcols(h):
        return slice(h * HEAD_DIM, (h + 1) * HEAD_DIM)

    if sn >= kblk:
        def new_body(state):
            j, _ = state
            start = pl.multiple_of(j * kblk, kblk)
            alive = visit(lambda h: _dot_nt(qh_ref[h], kn_ref[pl.ds(start, kblk), hcols(h)]),
                          lambda h, a: _dot(a, vn_ref[pl.ds(start, kblk), hcols(h)]), start + kiota < qpos)
            return j - 1, alive

        _, alive = lax.while_loop(alive_cond, new_body, ((qlocal + tq - 1) // kblk, True))
    else:
        def padded(ref, h):
            return jnp.concatenate([ref[:, hcols(h)], jnp.zeros((kblk - sn, HEAD_DIM), BF16)], axis=0)

        alive = visit(lambda h: _dot_nt(qh_ref[h], padded(kn_ref, h)), lambda h, a: _dot(a, padded(vn_ref, h)),
                      kiota < qpos)

    if has_cache:
        b = pl.program_id(0)

        def copies(j, slot):
            src = pl.ds(pl.multiple_of(j * kblk, kblk), kblk)
            return (pltpu.make_async_copy(kc_hbm.at[layer, b, :, :, src], kbuf.at[slot], sem.at[0, slot]),
                    pltpu.make_async_copy(vc_hbm.at[layer, b, :, :, src], vbuf.at[slot], sem.at[1, slot]))

        def start(j, slot):
            for cp in copies(j, slot):
                cp.start()

        def wait(j, slot):
            for cp in copies(j, slot):
                cp.wait()

        j0 = lc // kblk - 1

        @pl.when(alive)
        def _():
            start(j0, j0 % 2)

        def cache_body(state):
            j, _ = state
            slot = j % 2
            wait(j, slot)

            @pl.when(j > 0)
            def _():
                start(j - 1, 1 - slot)

            alive = visit(lambda h: _dot(qh_ref[h], kbuf[slot, h].astype(BF16)),
                          lambda h, a: _dot_nt(a, vbuf[slot, h].astype(BF16)), None)
            return j - 1, alive

        j_end, _ = lax.while_loop(alive_cond, cache_body, (j0, alive))

        @pl.when(jnp.logical_and(alive, j_end >= 0))
        def _():
            wait(j_end, j_end % 2)

    for h in heads:
        o_ref[:, hcols(h)] = acc_ref[h].astype(o_ref.dtype)


def _sb_call(bq, bk16, bv16, caches, *, batch, seq_q, tq, row0, layer=0, kblk=128):
    has_cache = caches is not None
    lc = caches[0].shape[-1] if has_cache else 0
    sn = seq_q
    assert kblk == LANES and lc % kblk == 0 and row0 % sn == 0 and sn % tq == 0
    assert sn % kblk == 0 or (sn < kblk and tq == sn)
    nqb = seq_q // tq
    rb0 = row0 // tq
    nb0 = row0 // sn
    in_specs = [pl.BlockSpec((tq, B_WIDTH), lambda b, i: (rb0 + b * nqb + i, 0)),
                pl.BlockSpec((sn, B_WIDTH), lambda b, i: (nb0 + b, 0)),
                pl.BlockSpec((sn, B_WIDTH), lambda b, i: (nb0 + b, 0))]
    args = [bq, bk16, bv16]
    scratch = [pltpu.VMEM((B_HEADS, tq, HEAD_DIM), BF16), pltpu.VMEM((B_HEADS, tq, LANES), F32),
               pltpu.VMEM((B_HEADS, tq, HEAD_DIM), F32)]
    if has_cache:
        in_specs += [pl.BlockSpec(memory_space=pl.ANY)] * 2
        args += list(caches)
        scratch += [pltpu.VMEM((2, B_HEADS, HEAD_DIM, kblk), F32), pltpu.VMEM((2, B_HEADS, HEAD_DIM, kblk), F32),
                    pltpu.SemaphoreType.DMA((2, 2))]
    kern = functools.partial(_sb_kernel, tq=tq, sn=sn, lc=lc, kblk=kblk, has_cache=has_cache, layer=layer)
    return pl.pallas_call(
        kern,
        grid=(batch, nqb),
        in_specs=in_specs,
        out_specs=pl.BlockSpec((tq, B_WIDTH), lambda b, i: (b * nqb + i, 0)),
        out_shape=jax.ShapeDtypeStruct((batch * seq_q, B_WIDTH), BF16),
        scratch_shapes=scratch,
        compiler_params=_params(("parallel", "arbitrary")),
        name="stick_breaking",
    )(*args)


def _ssd_kernel(z_ref, xbc_ref, misc_ref, cprev_ref, sprev_ref, cw_ref, cb_ref, dtb_ref, alog_ref, dsk_ref,
                ng_ref, y_ref, snew_ref, xpad_ref, st_ref, *, cs):
    ci = pl.program_id(1)
    nci = pl.num_programs(1)
    hpg = C_HEADS // C_GROUPS
    gw = hpg * C_HEAD_DIM

    @pl.when(ci == 0)
    def _():
        xpad_ref[5:8, :] = cprev_ref[0]
        for g in range(C_GROUPS):
            st_ref[g] = sprev_ref[0, g * hpg:(g + 1) * hpg].reshape(gw, D_STATE).T

    @pl.when(ci > 0)
    def _():
        xpad_ref[5:8, :] = xpad_ref[cs + 5:cs + 8, :]

    xpad_ref[8:8 + cs, :] = xbc_ref[...]
    xc = cb_ref[...] + xpad_ref[5:5 + cs, :] * cw_ref[0:1, :]
    for w in range(1, CONV_WIDTH):
        xc = xc + xpad_ref[5 + w:5 + w + cs, :] * cw_ref[w:w + 1, :]
    xc = _silu(xc)

    dt = _softplus(misc_ref[...] + dtb_ref[...])
    a = dt * (-jnp.exp(alog_ref[...]))
    r = lax.broadcasted_iota(I32, (cs, cs), 0)
    c = lax.broadcasted_iota(I32, (cs, cs), 1)
    tril = r >= c
    tril16 = jnp.where(tril, 1.0, 0.0).astype(BF16)
    a1, a2, a3 = _split3(a)
    a_cum = _dot(tril16, a1) + _dot(tril16, a2) + _dot(tril16, a3)
    pad = LANES - cs
    a_sq = a_cum if pad == 0 else jnp.concatenate([a_cum, jnp.zeros((pad, LANES), F32)], axis=0)
    a_cum_t = a_sq.T
    a_last = a_cum[cs - 1:cs, :]
    exp_a = jnp.exp(a_cum)
    dte = jnp.exp(a_last - a_cum)
    exp_last = jnp.exp(a_last)
    low_half = lax.broadcasted_iota(I32, (1, LANES), 1) < C_HEAD_DIM

    for g in range(C_GROUPS):
        bm = xc[:, C_D_INNER + g * D_STATE:C_D_INNER + (g + 1) * D_STATE].astype(BF16)
        cm = xc[:, C_D_INNER + (C_GROUPS + g) * D_STATE:C_D_INNER + (C_GROUPS + g + 1) * D_STATE].astype(BF16)
        cb = _dot_nt(cm, bm)
        st = st_ref[g]
        y_off = _dot(cm, st.astype(BF16))
        y_pairs, x_pairs, dec_pairs = [], [], []
        for pp in range(hpg // 2):
            h0 = g * hpg + 2 * pp
            ln0, ln1 = MISC_DT + h0, MISC_DT + h0 + 1

            def pat(v, ln0=ln0, ln1=ln1):
                return jnp.where(low_half, v[:, ln0:ln0 + 1], v[:, ln1:ln1 + 1])

            xd = xc[:, h0 * C_HEAD_DIM:(h0 + 2) * C_HEAD_DIM] * pat(dt)
            xd16 = xd.astype(BF16)
            y_pair = y_off[:, pp * LANES:(pp + 1) * LANES] * pat(exp_a)
            for ln, keep in ((ln0, low_half), (ln1, jnp.logical_not(low_half))):
                seg = a_cum[:, ln:ln + 1] - a_cum_t[ln:ln + 1, 0:cs]
                decay = jnp.where(tril, jnp.exp(jnp.where(tril, seg, 0.0)), 0.0)
                y_pair = y_pair + _dot((cb * decay).astype(BF16), jnp.where(keep, xd16, jnp.zeros_like(xd16)))
            y_pairs.append(y_pair)
            x_pairs.append((xd * pat(dte)).astype(BF16))
            dec_pairs.append(pat(exp_last))
        y = jnp.concatenate(y_pairs, axis=1)
        st_new = st * jnp.concatenate(dec_pairs, axis=1) + _dot_tn(bm, jnp.concatenate(x_pairs, axis=1))
        st_ref[g] = st_new

        gs = slice(g * gw, (g + 1) * gw)
        y = y + xc[:, gs] * dsk_ref[:, gs]
        y = y * _silu(z_ref[:, gs])
        y = y * lax.rsqrt(jnp.mean(y * y, axis=1, keepdims=True) + EPS) * ng_ref[:, gs]
        y_ref[:, gs] = y.astype(y_ref.dtype)

        @pl.when(ci == nci - 1)
        def _():
            snew_ref[0, g * hpg:(g + 1) * hpg] = st_new.T.reshape(hpg, C_HEAD_DIM, D_STATE)


def _ssd_call(p, cprev, sprev, cw, cb, dtb, alog, dsk, ng, *, batch, seq, cs, row0):
    nci = seq // cs
    rb0 = row0 // cs

    def pcol(width, off):
        return pl.BlockSpec((cs, width), lambda b, i: (rb0 + b * nci + i, off // width))

    def row(width, rows=1):
        return pl.BlockSpec((rows, width), lambda b, i: (0, 0))

    kern = functools.partial(_ssd_kernel, cs=cs)
    return pl.pallas_call(
        kern,
        grid=(batch, nci),
        in_specs=[pcol(C_D_INNER, COL_Z), pcol(C_CONV_DIM, COL_XBC), pcol(LANES, COL_MISC),
                  pl.BlockSpec((1, CONV_WIDTH - 1, C_CONV_DIM), lambda b, i: (b, 0, 0)),
                  pl.BlockSpec((1, C_HEADS, C_HEAD_DIM, D_STATE), lambda b, i: (b, 0, 0, 0)),
                  row(C_CONV_DIM, CONV_WIDTH), row(C_CONV_DIM), row(LANES), row(LANES),
                  row(C_D_INNER), row(C_D_INNER)],
        out_specs=[pl.BlockSpec((cs, C_D_INNER), lambda b, i: (b * nci + i, 0)),
                   pl.BlockSpec((1, C_HEADS, C_HEAD_DIM, D_STATE), lambda b, i: (b, 0, 0, 0))],
        out_shape=[jax.ShapeDtypeStruct((batch * seq, C_D_INNER), BF16),
                   jax.ShapeDtypeStruct((batch, C_HEADS, C_HEAD_DIM, D_STATE), F32)],
        scratch_shapes=[pltpu.VMEM((cs + 8, C_CONV_DIM), F32),
                        pltpu.VMEM((C_GROUPS, D_STATE, C_HEADS // C_GROUPS * C_HEAD_DIM), F32)],
        compiler_params=_params(("parallel", "arbitrary")),
        name="ssd_mixer",
    )(p, p, p, cprev, sprev, cw, cb, dtb, alog, dsk, ng)


def _outproj_kernel(x_ref, yap_ref, ybp_ref, ycp_ref, yas_ref, ybs_ref, ycs_ref, wa_ref, wb_ref, wc_ref, o_ref, *,
                    prompt_tiles):
    def mix(ya_ref, yb_ref, yc_ref):
        o_ref[...] = (x_ref[...] + _dot(ya_ref[...], wa_ref[...]) + _dot(yb_ref[...], wb_ref[...])
                      + _dot(yc_ref[...], wc_ref[...]))

    is_prompt = pl.program_id(0) < prompt_tiles

    @pl.when(is_prompt)
    def _():
        mix(yap_ref, ybp_ref, ycp_ref)

    @pl.when(jnp.logical_not(is_prompt))
    def _():
        mix(yas_ref, ybs_ref, ycs_ref)


def _outproj(x, y_prompt, y_sample, w):
    t, d = x.shape
    tp, ts = y_prompt[0].shape[0], y_sample[0].shape[0]
    tm = _tile(math.gcd(tp, ts), 256)
    npt = tp // tm

    def tok(width):
        return pl.BlockSpec((tm, width), lambda i: (i, 0))

    def ptok(width):
        return pl.BlockSpec((tm, width), lambda i: (jnp.minimum(i, npt - 1), 0))

    def stok(width):
        return pl.BlockSpec((tm, width), lambda i: (jnp.maximum(i - npt, 0), 0))

    widths = (A_WIDTH, B_WIDTH, C_D_INNER)
    return pl.pallas_call(
        functools.partial(_outproj_kernel, prompt_tiles=npt),
        grid=(t // tm,),
        in_specs=[tok(d)] + [ptok(wd) for wd in widths] + [stok(wd) for wd in widths] + [
            pl.BlockSpec((A_WIDTH, d), lambda i: (0, 0)),
            pl.BlockSpec((B_WIDTH, d), lambda i: (1, 0)),
            pl.BlockSpec((C_D_INNER, d), lambda i: (1, 0))],
        out_specs=tok(d),
        out_shape=jax.ShapeDtypeStruct((t, d), F32),
        compiler_params=_params(("parallel",)),
        name="out_proj",
    )(x, *y_prompt, *y_sample, w, w, w)


def _ffn_kernel(x_ref, g_ref, wg_ref, wu_ref, wd_ref, o_ref, h_ref):
    @pl.when(pl.program_id(1) == 0)
    def _():
        x = x_ref[...]
        ms = jnp.mean(x * x, axis=-1, keepdims=True)
        h_ref[...] = (x * lax.rsqrt(ms + EPS) * g_ref[...]).astype(BF16)
        o_ref[...] = x

    h = h_ref[...]
    act = _silu(_dot(h, wg_ref[...])) * _dot(h, wu_ref[...])
    o_ref[...] += _dot(act.astype(BF16), wd_ref[...])


def _ffn(x, g, wg, wu, wd):
    t, d = x.shape
    f = wg.shape[1]
    tm = _tile(t, 1024)
    tf = 512
    once = pl.Buffered(1)
    return pl.pallas_call(
        _ffn_kernel,
        grid=(t // tm, f // tf),
        in_specs=[pl.BlockSpec((tm, d), lambda i, j: (i, 0), pipeline_mode=once),
                  pl.BlockSpec((1, d), lambda i, j: (0, 0)),
                  pl.BlockSpec((d, tf), lambda i, j: (0, j)),
                  pl.BlockSpec((d, tf), lambda i, j: (0, j)),
                  pl.BlockSpec((tf, d), lambda i, j: (j, 0))],
        out_specs=pl.BlockSpec((tm, d), lambda i, j: (i, 0), pipeline_mode=once),
        out_shape=jax.ShapeDtypeStruct((t, d), F32),
        scratch_shapes=[pltpu.VMEM((tm, d), BF16)],
        compiler_params=_params(("parallel", "arbitrary")),
        name="swiglu",
    )(x, g, wg, wu, wd)


def _rope_rows(pos):
    half = ROPE_DIM // 2
    inv = 1.0 / (ROPE_THETA ** (jnp.arange(0, ROPE_DIM, 2, dtype=F32) / ROPE_DIM))
    ang = pos.astype(F32)[:, None] * inv[None, :]
    cos, sin = jnp.cos(ang), jnp.sin(ang)
    n = pos.shape[0]
    one = jnp.ones((n, HEAD_DIM - ROPE_DIM), F32)
    zero = jnp.zeros((n, HEAD_DIM - ROPE_DIM), F32)
    zh = jnp.zeros((n, half), F32)
    rc = jnp.concatenate([cos, cos, one], axis=1)
    rs1 = jnp.concatenate([zh, sin, zero], axis=1)
    rs2 = jnp.concatenate([-sin, zh, zero], axis=1)
    return tuple(jnp.tile(a, (1, LANES // HEAD_DIM)) for a in (rc, rs1, rs2))


def _w_in_segments():
    offs = [0]
    for n in IN_SIZES:
        offs.append(offs[-1] + n)
    dst = (COL_AQ, COL_AK, COL_AV, COL_IQ, COL_MISC + MISC_IK, COL_MISC + MISC_IW, COL_BQ, COL_BK, COL_BV,
           COL_Z, COL_XBC, COL_MISC + MISC_DT)
    return [(offs[i], dst[i], IN_SIZES[i]) for i in range(len(IN_SIZES))]


def _regroup_kernel(w_ref, o_ref):
    for src, dst, width in _w_in_segments():
        o_ref[:, dst:dst + width] = w_ref[0, :, src:src + width].astype(BF16)
    used = COL_MISC + MISC_DT + C_HEADS
    o_ref[:, used:] = jnp.zeros((o_ref.shape[0], P_COLS - used), BF16)


def _rearrange_w_in(w, layer):
    _, d, n = w.shape
    tr = _tile(d, 256)
    return pl.pallas_call(
        _regroup_kernel,
        grid=(d // tr,),
        in_specs=[pl.BlockSpec((1, tr, n), lambda i: (layer, i, 0))],
        out_specs=pl.BlockSpec((tr, P_COLS), lambda i: (i, 0)),
        out_shape=jax.ShapeDtypeStruct((d, P_COLS), BF16),
        compiler_params=_params(("parallel",)),
        name="regroup_w_in",
    )(w)


def _lane_row(v, off):
    return jnp.zeros((1, LANES), F32).at[0, off:off + v.shape[0]].set(v)


def _gain_row(g):
    return jnp.tile(g, LANES // HEAD_DIM)[None, :]


def kernel(x_prompt, x_sample, cache_a_k, cache_a_v, cache_a_idx_k, cache_b_k, cache_b_v, state_c_ssm, state_c_conv, norm1_g, w_in, a_q_norm_g, a_k_norm_g, idx_k_norm_g, b_q_norm_g, b_k_norm_g, c_conv_w, c_conv_b, c_dt_bias, c_a_log, c_d, c_norm_g, w_out, norm2_g, w_gate, w_up, w_down):
    bp, sp, d = x_prompt.shape
    bs, ss, _ = x_sample.shape
    depth = w_in.shape[0]
    past = cache_a_k.shape[2]
    tp, ts = bp * sp, bs * ss
    tq_p = 128
    cs_p = 128 if sp % 128 == 0 else sp
    assert ss >= CONV_WIDTH - 1 and sp >= CONV_WIDTH - 1

    pos = jnp.concatenate([jnp.tile(jnp.arange(sp, dtype=I32), bp),
                           jnp.tile(past + jnp.arange(ss, dtype=I32), bs)])
    rc, rs1, rs2 = _rope_rows(pos)
    x = jnp.concatenate([x_prompt.reshape(tp, d), x_sample.reshape(ts, d)], axis=0)
    zero_conv = jnp.zeros((bp, CONV_WIDTH - 1, C_CONV_DIM), F32)
    zero_ssm = jnp.zeros((bp, C_HEADS, C_HEAD_DIM, D_STATE), F32)

    p_states, s_states = [], []
    for l in range(depth):
        p = _proj(x, norm1_g[l][None, :], _rearrange_w_in(w_in, l))
        gik = jnp.concatenate([idx_k_norm_g[l], jnp.ones((LANES - IDX_DIM,), F32)])[None, :]
        (aq, ak, iq, ik, iw, bq, bk, bk16, bv16) = _post(
            p, rc, rs1, rs2, _gain_row(a_q_norm_g[l]), _gain_row(a_k_norm_g[l]), gik,
            _gain_row(b_q_norm_g[l]), _gain_row(b_k_norm_g[l]))

        a_caches = (jnp.transpose(cache_a_idx_k, (0, 1, 3, 2)), jnp.transpose(cache_a_k, (0, 1, 3, 4, 2)),
                    jnp.transpose(cache_a_v, (0, 1, 3, 4, 2)))
        ya_p = _dsap_call(aq, iq, iw, ik, ak, p, None, batch=bp, seq_q=sp, tq=tq_p, row0=0)
        ya_s = _dsap_call(aq, iq, iw, ik, ak, p, a_caches, batch=bs, seq_q=ss, tq=ss, row0=tp, layer=l)
        b_caches = (jnp.transpose(cache_b_k, (0, 1, 3, 4, 2)), jnp.transpose(cache_b_v, (0, 1, 3, 4, 2)))
        yb_p = _sb_call(bq, bk16, bv16, None, batch=bp, seq_q=sp, tq=tq_p, row0=0)
        yb_s = _sb_call(bq, bk16, bv16, b_caches, batch=bs, seq_q=ss, tq=ss, row0=tp, layer=l)
        cw, cb = c_conv_w[l], c_conv_b[l][None, :]
        dtb, alog = _lane_row(c_dt_bias[l], MISC_DT), _lane_row(c_a_log[l], MISC_DT)
        dsk, ng = jnp.repeat(c_d[l], C_HEAD_DIM)[None, :], c_norm_g[l][None, :]
        yc_p, ssm_p = _ssd_call(p, zero_conv, zero_ssm, cw, cb, dtb, alog, dsk, ng,
                                batch=bp, seq=sp, cs=cs_p, row0=0)
        yc_s, ssm_s = _ssd_call(p, state_c_conv[l], state_c_ssm[l], cw, cb, dtb, alog, dsk, ng,
                                batch=bs, seq=ss, cs=ss, row0=tp)

        x = _outproj(x, (ya_p, yb_p, yc_p), (ya_s, yb_s, yc_s), w_out[l].astype(BF16))
        x = _ffn(x, norm2_g[l][None, :], w_gate[l].astype(BF16), w_up[l].astype(BF16), w_down[l].astype(BF16))

        def state(lo, hi, b, s, ssm):
            av = p[lo:hi, COL_AV:COL_AV + LANES]
            bv = p[lo:hi, COL_BV:COL_BV + B_WIDTH]
            conv = jnp.stack([p[lo + (i + 1) * s - (CONV_WIDTH - 1):lo + (i + 1) * s, COL_XBC:COL_XBC + C_CONV_DIM]
                              for i in range(b)])
            return (ak[lo:hi].reshape(b, s, A_KV_HEADS, HEAD_DIM), av.reshape(b, s, A_KV_HEADS, HEAD_DIM),
                    ik[lo:hi].reshape(b, s, IDX_DIM), bk[lo:hi].reshape(b, s, B_HEADS, HEAD_DIM),
                    bv.reshape(b, s, B_HEADS, HEAD_DIM), ssm, conv)

        p_states.append(state(0, tp, bp, sp, ssm_p))
        s_states.append(state(tp, tp + ts, bs, ss, ssm_s))

    p_out = [jnp.stack(zs) for zs in zip(*p_states)]
    s_out = [jnp.stack(zs) for zs in zip(*s_states)]
    return (x[:tp].reshape(bp, sp, d), x[tp:].reshape(bs, ss, d), *p_out, *s_out)
```

```python
import functools
import math

import jax
import jax.numpy as jnp
from jax import lax
from jax.experimental import pallas as pl
from jax.experimental.pallas import tpu as pltpu

F32 = jnp.float32
BF16 = jnp.bfloat16
I32 = jnp.int32
I16 = jnp.int16

D_MODEL = 2048
EPS = 1e-6
CHUNK = 64
HEAD_DIM = 64
ROPE_DIM = HEAD_DIM // 4
ROPE_THETA = 500000.0
A_HEADS = 8
A_KV_HEADS = 2
A_WIDTH = A_HEADS * HEAD_DIM
IDX_HEADS = 16
IDX_DIM = 64
TOPK_MAX = 256
B_HEADS = 8
B_WIDTH = B_HEADS * HEAD_DIM
C_D_INNER = D_MODEL // 2
C_HEAD_DIM = 64
C_HEADS = C_D_INNER // C_HEAD_DIM
C_GROUPS = 4
D_STATE = 128
CONV_WIDTH = 4
C_CONV_DIM = C_D_INNER + 2 * C_GROUPS * D_STATE
IN_SIZES = (A_WIDTH, A_KV_HEADS * HEAD_DIM, A_KV_HEADS * HEAD_DIM, IDX_HEADS * IDX_DIM, IDX_DIM, IDX_HEADS,
            B_WIDTH, B_WIDTH, B_WIDTH, C_D_INNER, C_CONV_DIM, C_HEADS)

LANES = 128
VMEM_LIMIT = 48 * 1024 * 1024

COL_XBC, COL_IQ, COL_Z, COL_AQ, COL_BQ, COL_BK, COL_BV, COL_AK, COL_AV, COL_MISC = (
    0, 2048, 3072, 4096, 4608, 5120, 5632, 6144, 6272, 6400)
P_COLS = 6656
MISC_IK, MISC_IW, MISC_DT = 0, 64, 80

NEG_BIG = -1e30
INT_MIN = -2 ** 31
KEY_POS_INF = 0x7F800000
KEY_NEG_INF = INT_MIN + 0x7FFFFF
SB_DEAD = -104.0


def _params(sem):
    return pltpu.CompilerParams(dimension_semantics=sem, vmem_limit_bytes=VMEM_LIMIT)


def _tile(n, pref):
    t = min(n, pref)
    while n % t:
        t -= 8
    return t


def _dot_nt(a, b):
    return lax.dot_general(a, b, (((1,), (1,)), ((), ())), preferred_element_type=F32)


def _dot_tn(a, b):
    return lax.dot_general(a, b, (((0,), (0,)), ((), ())), preferred_element_type=F32)


def _dot(a, b):
    return jnp.dot(a, b, preferred_element_type=F32)


def _split3(x):
    h1 = x.astype(BF16)
    r1 = x - h1.astype(F32)
    h2 = r1.astype(BF16)
    h3 = (r1 - h2.astype(F32)).astype(BF16)
    return h1, h2, h3


def _softplus(x):
    return jnp.maximum(x, 0.0) + jnp.log1p(jnp.exp(-jnp.abs(x)))


def _silu(x):
    return x * (1.0 / (1.0 + jnp.exp(-x)))


def _proj_kernel(x_ref, g_ref, w_ref, o_ref, h_ref):
    @pl.when(pl.program_id(1) == 0)
    def _():
        x = x_ref[...]
        ms = jnp.mean(x * x, axis=-1, keepdims=True)
        h_ref[...] = (x * lax.rsqrt(ms + EPS) * g_ref[...]).astype(BF16)

    o_ref[...] = _dot(h_ref[...], w_ref[...])


def _proj(x, g, w):
    t, d = x.shape
    n = w.shape[1]
    tm = _tile(t, 1024)
    tn = 512
    return pl.pallas_call(
        _proj_kernel,
        grid=(t // tm, n // tn),
        in_specs=[pl.BlockSpec((tm, d), lambda i, j: (i, 0)),
                  pl.BlockSpec((1, d), lambda i, j: (0, 0)),
                  pl.BlockSpec((d, tn), lambda i, j: (0, j))],
        out_specs=pl.BlockSpec((tm, tn), lambda i, j: (i, j)),
        out_shape=jax.ShapeDtypeStruct((t, n), F32),
        scratch_shapes=[pltpu.VMEM((tm, d), BF16)],
        compiler_params=_params(("parallel", "arbitrary")),
        name="in_proj",
    )(x, g, w)


def _post_kernel(aq_r, bq_r, bk_r, bv_r, iq_r, ak_r, misc_r, rc_r, rs1_r, rs2_r,
                 gaq_r, gak_r, gik_r, gbq_r, gbk_r,
                 aq_o, ak_o, iq_o, ik_o, iw_o, bq_o, bk_o, bk16_o, bv16_o):
    r = lax.broadcasted_iota(I32, (LANES, LANES), 0) // HEAD_DIM
    c = lax.broadcasted_iota(I32, (LANES, LANES), 1) // HEAD_DIM
    segm = jnp.where(r == c, 1.0, 0.0).astype(BF16)
    rc, rs1, rs2 = rc_r[...], rs1_r[...], rs2_r[...]

    def headnorm(xc, g_row):
        h1, h2, h3 = _split3(xc * xc)
        ss = _dot(h1, segm) + _dot(h2, segm) + _dot(h3, segm)
        return xc * lax.rsqrt(ss * (1.0 / HEAD_DIM) + EPS) * g_row

    def rope(xc):
        return xc * rc + pltpu.roll(xc, 8, 1) * rs1 + pltpu.roll(xc, LANES - 8, 1) * rs2

    for k in range(A_WIDTH // LANES):
        sl = slice(k * LANES, (k + 1) * LANES)
        aq_o[:, sl] = (rope(headnorm(aq_r[:, sl], gaq_r[...])) * (HEAD_DIM ** -0.5)).astype(BF16)
        bq_o[:, sl] = (headnorm(bq_r[:, sl], gbq_r[...]) * (HEAD_DIM ** -0.5)).astype(BF16)
        bk = headnorm(bk_r[:, sl], gbk_r[...])
        bk_o[:, sl] = bk
        bk16_o[:, sl] = bk.astype(BF16)
        bv16_o[:, sl] = bv_r[:, sl].astype(BF16)
    for k in range(IDX_HEADS * IDX_DIM // LANES):
        sl = slice(k * LANES, (k + 1) * LANES)
        iq_o[:, sl] = rope(iq_r[:, sl]).astype(BF16)
    ak_o[...] = rope(headnorm(ak_r[...], gak_r[...]))
    misc = misc_r[...]
    ik_o[...] = rope(headnorm(misc, gik_r[...]))[:, MISC_IK:MISC_IK + IDX_DIM]
    iw_o[...] = misc[:, MISC_IW:MISC_IW + IDX_HEADS] * (IDX_HEADS ** -0.5 * IDX_DIM ** -0.5)


def _post(p, rc, rs1, rs2, gaq, gak, gik, gbq, gbk):
    t = p.shape[0]
    tm = _tile(t, 256)

    def col(width, off):
        return pl.BlockSpec((tm, width), lambda i: (i, off // width))

    def row(width):
        return pl.BlockSpec((1, width), lambda i: (0, 0))

    def out(width):
        return pl.BlockSpec((tm, width), lambda i: (i, 0))

    widths_dtypes = [(A_WIDTH, BF16), (LANES, F32), (IDX_HEADS * IDX_DIM, BF16), (IDX_DIM, F32),
                     (IDX_HEADS, F32), (B_WIDTH, BF16), (B_WIDTH, F32), (B_WIDTH, BF16), (B_WIDTH, BF16)]
    return pl.pallas_call(
        _post_kernel,
        grid=(t // tm,),
        in_specs=[col(A_WIDTH, COL_AQ), col(B_WIDTH, COL_BQ), col(B_WIDTH, COL_BK), col(B_WIDTH, COL_BV),
                  col(IDX_HEADS * IDX_DIM, COL_IQ), col(LANES, COL_AK), col(LANES, COL_MISC),
                  out(LANES), out(LANES), out(LANES),
                  row(LANES), row(LANES), row(LANES), row(LANES), row(LANES)],
        out_specs=[out(w) for w, _ in widths_dtypes],
        out_shape=[jax.ShapeDtypeStruct((t, w), dt) for w, dt in widths_dtypes],
        compiler_params=_params(("parallel",)),
        name="head_post",
    )(p, p, p, p, p, p, p, rc, rs1, rs2, gaq, gak, gik, gbq, gbk)


CACHE_CHUNK = 512


def _dsap_kernel(*refs, tq, sn, lc, topk, kblk, has_cache, layer):
    nb = LANES // tq
    if has_cache:
        (aq_ref, iq_ref, iw_ref, ikn_ref, akn_ref, avn_ref, ikc_hbm, akc_hbm, avc_hbm, o_ref,
         key_ref, hi_ref, lo_ref, eq_ref, bias_ref, ik16_ref, ak16_ref, vt_ref, iqh_ref, aqh_ref, m_ref, acc_ref,
         ikbuf, akbuf, avbuf, sem) = refs
    else:
        (aq_ref, iq_ref, iw_ref, ikn_ref, akn_ref, avn_ref, o_ref,
         key_ref, hi_ref, lo_ref, eq_ref, bias_ref, ik16_ref, ak16_ref, vt_ref, iqh_ref, aqh_ref, m_ref,
         acc_ref) = refs
    grp = pl.program_id(0)
    qi = pl.program_id(1)
    n_keys = lc + sn
    lp = ik16_ref.shape[0]
    rep = A_HEADS // A_KV_HEADS
    ch = CACHE_CHUNK

    @pl.when(qi == 0)
    def _():
        def put(row, j, ik, k0, k1, v_t):
            n = ik.shape[0]
            cols = slice(j * IDX_DIM, (j + 1) * IDX_DIM)
            ik16_ref[pl.ds(row, n), cols] = ik.astype(BF16)
            ak16_ref[0, pl.ds(row, n), cols] = k0.astype(BF16)
            ak16_ref[1, pl.ds(row, n), cols] = k1.astype(BF16)
            vt = v_t.astype(BF16)
            for e in range(n // kblk):
                for g in range(A_KV_HEADS):
                    vt_ref[g, row // kblk + e, 0:HEAD_DIM, j * kblk:(j + 1) * kblk] = (
                        vt[g * HEAD_DIM:(g + 1) * HEAD_DIM, e * kblk:(e + 1) * kblk])

        if has_cache:
            nc = lc // ch

            def copies(j, c, slot):
                b = grp * nb + j
                src = pl.ds(pl.multiple_of(c * ch, ch), ch)
                return (pltpu.make_async_copy(ikc_hbm.at[layer, b, :, src], ikbuf.at[slot], sem.at[0, slot]),
                        pltpu.make_async_copy(akc_hbm.at[layer, b, :, :, src], akbuf.at[slot], sem.at[1, slot]),
                        pltpu.make_async_copy(avc_hbm.at[layer, b, :, :, src], avbuf.at[slot], sem.at[2, slot]))

            def start(j, c, slot):
                for cp in copies(j, c, slot):
                    cp.start()

            start(0, 0, 0)
            for j in range(nb):
                def chunk_body(c, carry, j=j):
                    slot = c % 2
                    for cp in copies(j, c, slot):
                        cp.wait()

                    @pl.when(c + 1 < nc)
                    def _():
                        start(j, c + 1, 1 - slot)

                    if j + 1 < nb:
                        @pl.when(c + 1 == nc)
                        def _():
                            start(j + 1, 0, 0)

                    kk = akbuf[slot].reshape(LANES, ch).T
                    ik = jnp.concatenate([ikbuf[slot], jnp.zeros((LANES - IDX_DIM, ch), F32)], axis=0).T
                    put(pl.multiple_of(c * ch, ch), j, ik[:, 0:IDX_DIM], kk[:, 0:HEAD_DIM], kk[:, HEAD_DIM:LANES],
                        avbuf[slot].reshape(LANES, ch))
                    return carry

                lax.fori_loop(0, nc, chunk_body, 0)

        for j in range(nb):
            if sn % ch == 0:
                def new_body(c, carry, j=j):
                    src = pl.ds(pl.multiple_of(j * sn + c * ch, ch), ch)
                    ak = akn_ref[src, :]
                    put(pl.multiple_of(lc + c * ch, ch), j, ikn_ref[src, :], ak[:, 0:HEAD_DIM], ak[:, HEAD_DIM:LANES],
                        avn_ref[src, :].T)
                    return carry

                lax.fori_loop(0, sn // ch, new_body, 0)
            else:
                def padded(ref):
                    return jnp.concatenate([ref[j * sn:(j + 1) * sn, :],
                                            jnp.zeros((kblk - sn, ref.shape[-1]), F32)], axis=0)

                ak = padded(akn_ref)
                put(lc, j, padded(ikn_ref), ak[:, 0:HEAD_DIM], ak[:, HEAD_DIM:LANES], padded(avn_ref).T)
        for g in range(A_KV_HEADS):
            vt_ref[g, :, HEAD_DIM:LANES, :] = jnp.ones((lp // kblk, LANES - HEAD_DIM, nb * kblk), BF16)

    qstart = lc + qi * tq
    kmax = jnp.minimum(n_keys, ((qstart + tq - 1) // CHUNK + 1) * CHUNK)
    nkb = (kmax + kblk - 1) // kblk
    lane = lax.broadcasted_iota(I32, (1, LANES), 1)
    qpos = qstart + lane % tq
    krow = lax.broadcasted_iota(I32, (kblk, 1), 0)
    lane_group = lane // tq
    two = 2 * LANES

    def head_major(dst, src, nheads):
        x_t = src[...].astype(F32).T
        for h in range(nheads):
            xh = x_t[h * HEAD_DIM:(h + 1) * HEAD_DIM, :]
            if nb > 1:
                xh = jnp.concatenate([jnp.where(lane_group == j, xh, 0.0) for j in range(nb)], axis=0)
            dst[:, h * LANES:(h + 1) * LANES] = xh.astype(BF16)

    head_major(iqh_ref, iq_ref, IDX_HEADS)
    head_major(aqh_ref, aq_ref, A_HEADS)
    iw_t = jnp.concatenate([iw_ref[...], jnp.zeros((LANES, LANES - IDX_HEADS), F32)], axis=1).T

    def score_body(kb, carry):
        for sub in range(kblk // LANES):
            start = pl.multiple_of(kb * kblk + sub * LANES, LANES)
            ikb = ik16_ref[pl.ds(start, LANES), :]
            acc = jnp.zeros((LANES, LANES), F32)
            for pr in range(IDX_HEADS // 2):
                logit = _dot(ikb, iqh_ref[:, pr * two:(pr + 1) * two])
                for e in range(2):
                    h = 2 * pr + e
                    acc = acc + iw_t[h:h + 1, :] * jnp.maximum(logit[:, e * LANES:(e + 1) * LANES], 0.0)
            kpos = start + krow[0:LANES]
            adm = jnp.logical_and(kpos // CHUNK <= qpos // CHUNK, kpos < n_keys)
            bits = pltpu.bitcast(jnp.where(adm, acc, -jnp.inf), I32)
            bits = jnp.where(bits == INT_MIN, 0, bits)
            key = bits ^ ((bits >> 31) & 0x7FFFFFFF)
            rows = slice(sub * LANES, (sub + 1) * LANES)
            key_ref[kb, rows, :] = key
            hi_ref[kb, rows, :] = (key >> 16).astype(I16)
            lo_ref[kb, rows, :] = ((key & 0xFFFF) - 32768).astype(I16)
        return carry

    lax.fori_loop(0, nkb, score_body, 0)

    def count(pred):
        def body(kb, acc):
            hit = jnp.where(pred(key_ref[kb], kb), 1.0, 0.0)
            return acc + jnp.sum(hit.reshape(kblk // 64, 64, LANES), axis=0)
        acc = lax.fori_loop(0, nkb, body, jnp.zeros((64, LANES), F32))
        return jnp.sum(acc, axis=0, keepdims=True)

    one16, zero16 = jnp.int16(1), jnp.int16(0)

    def count16(marks):
        def body(kb, acc):
            m4 = marks(kb).reshape(kblk // 64, 64, LANES)
            return acc + ((m4[0] + m4[1]) + (m4[2] + m4[3]))
        acc = lax.fori_loop(0, nkb, body, jnp.zeros((64, LANES), I16))
        return jnp.sum(acc.astype(I32), axis=0, keepdims=True).astype(F32)

    def half_search(count_at, target):
        def body(i, t):
            cand = t + (jnp.int32(1) << (15 - i))
            return jnp.where(count_at(cand.astype(I16)) >= target, cand, t)
        return lax.fori_loop(0, 16, body, jnp.full((1, LANES), -32768, I32))

    kf = float(topk)
    thr_hi = half_search(lambda c: count16(lambda kb: jnp.where(hi_ref[kb] >= c, one16, zero16)), kf)
    thr_hi16 = thr_hi.astype(I16)
    n_above = count16(lambda kb: jnp.where(hi_ref[kb] > thr_hi16, one16, zero16))

    def eq_body(kb, carry):
        eq_ref[kb] = jnp.where(hi_ref[kb] == thr_hi16, one16, zero16)
        return carry

    lax.fori_loop(0, nkb, eq_body, 0)
    thr_lo = half_search(lambda c: count16(lambda kb: jnp.where(lo_ref[kb] >= c, eq_ref[kb], zero16)), kf - n_above)
    thr = thr_hi * 65536 + (thr_lo + 32768)
    n_gt = count(lambda key, kb: key > thr)
    n_ge = count(lambda key, kb: key >= thr)
    need = kf - n_gt

    def tie_search():
        def jbody(i, jl):
            cand = jl + (jnp.int32(1) << (14 - i))
            cnt = count(lambda key, kb: jnp.logical_and(key == thr, kb * kblk + krow < cand))
            return jnp.where(cnt <= need, cand, jl)
        return lax.fori_loop(0, 15, jbody, jnp.zeros((1, LANES), I32))

    jlim = lax.cond(jnp.max(n_ge) > kf, tie_search, lambda: jnp.full((1, LANES), 2 ** 30, I32))

    def bias_body(kb, carry):
        key = key_ref[kb]
        kpos = kb * kblk + krow
        sel = jnp.logical_or(key > thr, jnp.logical_and(key == thr, kpos < jlim))
        sel = jnp.logical_and(sel, jnp.logical_and(key > KEY_NEG_INF, key < KEY_POS_INF))
        bias_ref[kb] = jnp.where(sel, 0.0, NEG_BIG)
        return carry

    lax.fori_loop(0, nkb, bias_body, 0)

    m_ref[...] = jnp.full(m_ref.shape, -3e38, F32)
    acc_ref[...] = jnp.zeros(acc_ref.shape, F32)
    pair_group = (lax.broadcasted_iota(I32, (1, 2 * LANES), 1) % LANES) // tq

    def att_body(kb, carry):
        start = pl.multiple_of(kb * kblk, kblk)
        bias = bias_ref[kb]
        m_all = m_ref[...]
        kks = [ak16_ref[g, pl.ds(start, kblk), :] for g in range(A_KV_HEADS)]
        s_ts = [_dot(kks[h0 // rep], aqh_ref[:, h0 * LANES:(h0 + 2) * LANES]) for h0 in range(0, A_HEADS, 2)]
        ps, alphas, m_rows = [], [], []
        for h in range(A_HEADS):
            x = s_ts[h // 2][:, (h % 2) * LANES:(h % 2 + 1) * LANES] + bias
            part = jnp.max(x.reshape(kblk // 64, 64, LANES), axis=0)
            bmax = jnp.max(part, axis=0, keepdims=True)
            m_old = m_all[h:h + 1, :]
            m_new = jnp.maximum(m_old, bmax)
            alphas.append(jnp.exp(m_old - m_new))
            ps.append(jnp.exp(x - m_new).astype(BF16))
            m_rows.append(m_new)
        m_ref[...] = jnp.concatenate(m_rows, axis=0)
        upds = []
        for h0 in range(0, A_HEADS, 2):
            pp = jnp.concatenate(ps[h0:h0 + 2], axis=1)
            if nb > 1:
                pp = jnp.concatenate([jnp.where(pair_group == j, pp, jnp.zeros_like(pp)) for j in range(nb)], axis=0)
            upds.append(_dot(vt_ref[h0 // rep, kb], pp))
        for h in range(A_HEADS):
            acc_ref[h] = acc_ref[h] * alphas[h] + upds[h // 2][:, (h % 2) * LANES:(h % 2 + 1) * LANES]
        return carry

    lax.fori_loop(0, nkb, att_body, 0)

    outs = []
    for h in range(A_HEADS):
        a = acc_ref[h]
        outs.append(a[0:HEAD_DIM, :] / a[HEAD_DIM:LANES, :])
    o_ref[...] = jnp.concatenate(outs, axis=0).T.astype(o_ref.dtype)


def _dsap_call(aq, iq, iw, ik, ak, p, caches, *, batch, seq_q, tq, row0, layer=0, kblk=256):
    has_cache = caches is not None
    lc = caches[0].shape[-1] if has_cache else 0
    sn = seq_q
    nb = LANES // tq
    n_keys = lc + sn
    lp = -(-n_keys // kblk) * kblk
    assert LANES % tq == 0 and batch % nb == 0 and sn % tq == 0 and row0 % (nb * sn) == 0 and row0 % LANES == 0
    assert lc % (2 * CACHE_CHUNK) == 0 and CACHE_CHUNK % kblk == 0
    assert (sn % CACHE_CHUNK == 0 and lp == n_keys) or (sn < kblk and lp == lc + kblk)
    groups = batch // nb
    nqb = seq_q // tq
    assert nb == 1 or nqb == 1
    rb0 = row0 // LANES
    nb0 = row0 // (nb * sn)
    topk = min(TOPK_MAX, n_keys // 4)
    nkb_max = lp // kblk

    def qspec(width):
        return pl.BlockSpec((LANES, width), lambda g, i: (rb0 + g * nqb + i, 0))

    def nspec(width, colblk=0):
        return pl.BlockSpec((nb * sn, width), lambda g, i: (nb0 + g, colblk))

    in_specs = [qspec(A_WIDTH), qspec(IDX_HEADS * IDX_DIM), qspec(IDX_HEADS),
                nspec(IDX_DIM), nspec(LANES), nspec(LANES, COL_AV // LANES)]
    args = [aq, iq, iw, ik, ak, p]
    scratch = [pltpu.VMEM((nkb_max, kblk, LANES), I32),
               pltpu.VMEM((nkb_max, kblk, LANES), I16),
               pltpu.VMEM((nkb_max, kblk, LANES), I16),
               pltpu.VMEM((nkb_max, kblk, LANES), I16),
               pltpu.VMEM((nkb_max, kblk, LANES), F32),
               pltpu.VMEM((lp, nb * IDX_DIM), BF16),
               pltpu.VMEM((A_KV_HEADS, lp, nb * HEAD_DIM), BF16),
               pltpu.VMEM((A_KV_HEADS, nkb_max, LANES, nb * kblk), BF16),
               pltpu.VMEM((nb * IDX_DIM, IDX_HEADS * LANES), BF16),
               pltpu.VMEM((nb * HEAD_DIM, A_HEADS * LANES), BF16),
               pltpu.VMEM((A_HEADS, LANES), F32),
               pltpu.VMEM((A_HEADS, LANES, LANES), F32)]
    if has_cache:
        in_specs += [pl.BlockSpec(memory_space=pl.ANY)] * 3
        args += list(caches)
        scratch += [pltpu.VMEM((2, IDX_DIM, CACHE_CHUNK), F32),
                    pltpu.VMEM((2, A_KV_HEADS, HEAD_DIM, CACHE_CHUNK), F32),
                    pltpu.VMEM((2, A_KV_HEADS, HEAD_DIM, CACHE_CHUNK), F32),
                    pltpu.SemaphoreType.DMA((3, 2))]
    kern = functools.partial(_dsap_kernel, tq=tq, sn=sn, lc=lc, topk=topk, kblk=kblk, has_cache=has_cache,
                             layer=layer)
    return pl.pallas_call(
        kern,
        grid=(groups, nqb),
        in_specs=in_specs,
        out_specs=pl.BlockSpec((LANES, A_WIDTH), lambda g, i: (g * nqb + i, 0)),
        out_shape=jax.ShapeDtypeStruct((batch * seq_q, A_WIDTH), BF16),
        scratch_shapes=scratch,
        compiler_params=_params(("parallel", "arbitrary")),
        name="dsa_attention",
    )(*args)


def _sb_kernel(*refs, tq, sn, lc, kblk, has_cache, layer):
    if has_cache:
        q_ref, kn_ref, vn_ref, kc_hbm, vc_hbm, o_ref, qh_ref, carry_ref, acc_ref, kbuf, vbuf, sem = refs
    else:
        q_ref, kn_ref, vn_ref, o_ref, qh_ref, carry_ref, acc_ref = refs
    qlocal = pl.program_id(1) * tq
    qpos = qlocal + lax.broadcasted_iota(I32, (tq, 1), 0)
    kiota = lax.broadcasted_iota(I32, (1, kblk), 1)
    r = lax.broadcasted_iota(I32, (kblk, kblk), 0)
    c = lax.broadcasted_iota(I32, (kblk, kblk), 1)
    upper = jnp.where(r > c, 1.0, 0.0).astype(BF16)
    heads = range(B_HEADS)
    for h in heads:
        qh_ref[h] = q_ref[:, h * HEAD_DIM:(h + 1) * HEAD_DIM]
    carry_ref[...] = jnp.zeros(carry_ref.shape, F32)
    acc_ref[...] = jnp.zeros(acc_ref.shape, F32)

    def visit(score, apply, causal):
        zs = [score(h) for h in heads]
        sps = [_softplus(z) for z in zs]
        lks = [-sp if causal is None else jnp.where(causal, -sp, 0.0) for sp in sps]
        carries = [carry_ref[h] for h in heads]
        laters = []
        for h in heads:
            l1, l2, l3 = _split3(lks[h])
            laters.append(carries[h] + (_dot(l1, upper) + _dot(l2, upper) + _dot(l3, upper)))
        weights = [jnp.exp((zs[h] - sps[h]) + laters[h]) for h in heads]
        if causal is not None:
            weights = [jnp.where(causal, a, 0.0) for a in weights]
        upds = [apply(h, weights[h].astype(BF16)) for h in heads]
        new = [carries[h] + jnp.sum(lks[h], axis=1, keepdims=True) for h in heads]
        top = new[0]
        for h in heads:
            acc_ref[h] += upds[h]
            carry_ref[h] = new[h]
            top = jnp.maximum(top, new[h])
        return jnp.max(top) > SB_DEAD

    def alive_cond(state):
        j, alive = state
        return jnp.logical_and(j >= 0, alive)

    def hcols(h):
        return slice(h * HEAD_DIM, (h + 1) * HEAD_DIM)

    if sn >= kblk:
        def new_body(state):
            j, _ = state
            start = pl.multiple_of(j * kblk, kblk)
            alive = visit(lambda h: _dot_nt(qh_ref[h], kn_ref[pl.ds(start, kblk), hcols(h)]),
                          lambda h, a: _dot(a, vn_ref[pl.ds(start, kblk), hcols(h)]), start + kiota < qpos)
            return j - 1, alive

        _, alive = lax.while_loop(alive_cond, new_body, ((qlocal + tq - 1) // kblk, True))
    else:
        def padded(ref, h):
            return jnp.concatenate([ref[:, hcols(h)], jnp.zeros((kblk - sn, HEAD_DIM), BF16)], axis=0)

        alive = visit(lambda h: _dot_nt(qh_ref[h], padded(kn_ref, h)), lambda h, a: _dot(a, padded(vn_ref, h)),
                      kiota < qpos)

    if has_cache:
        b = pl.program_id(0)

        def copies(j, slot):
            src = pl.ds(pl.multiple_of(j * kblk, kblk), kblk)
            return (pltpu.make_async_copy(kc_hbm.at[layer, b, :, :, src], kbuf.at[slot], sem.at[0, slot]),
                    pltpu.make_async_copy(vc_hbm.at[layer, b, :, :, src], vbuf.at[slot], sem.at[1, slot]))

        def start(j, slot):
            for cp in copies(j, slot):
                cp.start()

        def wait(j, slot):
            for cp in copies(j, slot):
                cp.wait()

        j0 = lc // kblk - 1

        @pl.when(alive)
        def _():
            start(j0, j0 % 2)

        def cache_body(state):
            j, _ = state
            slot = j % 2
            wait(j, slot)

            @pl.when(j > 0)
            def _():
                start(j - 1, 1 - slot)

            alive = visit(lambda h: _dot(qh_ref[h], kbuf[slot, h].astype(BF16)),
                          lambda h, a: _dot_nt(a, vbuf[slot, h].astype(BF16)), None)
            return j - 1, alive

        j_end, _ = lax.while_loop(alive_cond, cache_body, (j0, alive))

        @pl.when(jnp.logical_and(alive, j_end >= 0))
        def _():
            wait(j_end, j_end % 2)

    for h in heads:
        o_ref[:, hcols(h)] = acc_ref[h].astype(o_ref.dtype)


def _sb_call(bq, bk16, bv16, caches, *, batch, seq_q, tq, row0, layer=0, kblk=128):
    has_cache = caches is not None
    lc = caches[0].shape[-1] if has_cache else 0
    sn = seq_q
    assert kblk == LANES and lc % kblk == 0 and row0 % sn == 0 and sn % tq == 0
    assert sn % kblk == 0 or (sn < kblk and tq == sn)
    nqb = seq_q // tq
    rb0 = row0 // tq
    nb0 = row0 // sn
    in_specs = [pl.BlockSpec((tq, B_WIDTH), lambda b, i: (rb0 + b * nqb + i, 0)),
                pl.BlockSpec((sn, B_WIDTH), lambda b, i: (nb0 + b, 0)),
                pl.BlockSpec((sn, B_WIDTH), lambda b, i: (nb0 + b, 0))]
    args = [bq, bk16, bv16]
    scratch = [pltpu.VMEM((B_HEADS, tq, HEAD_DIM), BF16), pltpu.VMEM((B_HEADS, tq, LANES), F32),
               pltpu.VMEM((B_HEADS, tq, HEAD_DIM), F32)]
    if has_cache:
        in_specs += [pl.BlockSpec(memory_space=pl.ANY)] * 2
        args += list(caches)
        scratch += [pltpu.VMEM((2, B_HEADS, HEAD_DIM, kblk), F32), pltpu.VMEM((2, B_HEADS, HEAD_DIM, kblk), F32),
                    pltpu.SemaphoreType.DMA((2, 2))]
    kern = functools.partial(_sb_kernel, tq=tq, sn=sn, lc=lc, kblk=kblk, has_cache=has_cache, layer=layer)
    return pl.pallas_call(
        kern,
        grid=(batch, nqb),
        in_specs=in_specs,
        out_specs=pl.BlockSpec((tq, B_WIDTH), lambda b, i: (b * nqb + i, 0)),
        out_shape=jax.ShapeDtypeStruct((batch * seq_q, B_WIDTH), BF16),
        scratch_shapes=scratch,
        compiler_params=_params(("parallel", "arbitrary")),
        name="stick_breaking",
    )(*args)


def _ssd_kernel(z_ref, xbc_ref, misc_ref, cprev_ref, sprev_ref, cw_ref, cb_ref, dtb_ref, alog_ref, dsk_ref,
                ng_ref, y_ref, snew_ref, xpad_ref, st_ref, *, cs):
    ci = pl.program_id(1)
    nci = pl.num_programs(1)
    hpg = C_HEADS // C_GROUPS
    gw = hpg * C_HEAD_DIM

    @pl.when(ci == 0)
    def _():
        xpad_ref[5:8, :] = cprev_ref[0]
        for g in range(C_GROUPS):
            st_ref[g] = sprev_ref[0, g * hpg:(g + 1) * hpg].reshape(gw, D_STATE).T

    @pl.when(ci > 0)
    def _():
        xpad_ref[5:8, :] = xpad_ref[cs + 5:cs + 8, :]

    xpad_ref[8:8 + cs, :] = xbc_ref[...]
    xc = cb_ref[...] + xpad_ref[5:5 + cs, :] * cw_ref[0:1, :]
    for w in range(1, CONV_WIDTH):
        xc = xc + xpad_ref[5 + w:5 + w + cs, :] * cw_ref[w:w + 1, :]
    xc = _silu(xc)

    dt = _softplus(misc_ref[...] + dtb_ref[...])
    a = dt * (-jnp.exp(alog_ref[...]))
    r = lax.broadcasted_iota(I32, (cs, cs), 0)
    c = lax.broadcasted_iota(I32, (cs, cs), 1)
    tril = r >= c
    tril16 = jnp.where(tril, 1.0, 0.0).astype(BF16)
    a1, a2, a3 = _split3(a)
    a_cum = _dot(tril16, a1) + _dot(tril16, a2) + _dot(tril16, a3)
    pad = LANES - cs
    a_sq = a_cum if pad == 0 else jnp.concatenate([a_cum, jnp.zeros((pad, LANES), F32)], axis=0)
    a_cum_t = a_sq.T
    a_last = a_cum[cs - 1:cs, :]
    exp_a = jnp.exp(a_cum)
    dte = jnp.exp(a_last - a_cum)
    exp_last = jnp.exp(a_last)
    low_half = lax.broadcasted_iota(I32, (1, LANES), 1) < C_HEAD_DIM

    for g in range(C_GROUPS):
        bm = xc[:, C_D_INNER + g * D_STATE:C_D_INNER + (g + 1) * D_STATE].astype(BF16)
        cm = xc[:, C_D_INNER + (C_GROUPS + g) * D_STATE:C_D_INNER + (C_GROUPS + g + 1) * D_STATE].astype(BF16)
        cb = _dot_nt(cm, bm)
        st = st_ref[g]
        y_off = _dot(cm, st.astype(BF16))
        y_pairs, x_pairs, dec_pairs = [], [], []
        for pp in range(hpg // 2):
            h0 = g * hpg + 2 * pp
            ln0, ln1 = MISC_DT + h0, MISC_DT + h0 + 1

            def pat(v, ln0=ln0, ln1=ln1):
                return jnp.where(low_half, v[:, ln0:ln0 + 1], v[:, ln1:ln1 + 1])

            xd = xc[:, h0 * C_HEAD_DIM:(h0 + 2) * C_HEAD_DIM] * pat(dt)
            xd16 = xd.astype(BF16)
            y_pair = y_off[:, pp * LANES:(pp + 1) * LANES] * pat(exp_a)
            for ln, keep in ((ln0, low_half), (ln1, jnp.logical_not(low_half))):
                seg = a_cum[:, ln:ln + 1] - a_cum_t[ln:ln + 1, 0:cs]
                decay = jnp.where(tril, jnp.exp(jnp.where(tril, seg, 0.0)), 0.0)
                y_pair = y_pair + _dot((cb * decay).astype(BF16), jnp.where(keep, xd16, jnp.zeros_like(xd16)))
            y_pairs.append(y_pair)
            x_pairs.append((xd * pat(dte)).astype(BF16))
            dec_pairs.append(pat(exp_last))
        y = jnp.concatenate(y_pairs, axis=1)
        st_new = st * jnp.concatenate(dec_pairs, axis=1) + _dot_tn(bm, jnp.concatenate(x_pairs, axis=1))
        st_ref[g] = st_new

        gs = slice(g * gw, (g + 1) * gw)
        y = y + xc[:, gs] * dsk_ref[:, gs]
        y = y * _silu(z_ref[:, gs])
        y = y * lax.rsqrt(jnp.mean(y * y, axis=1, keepdims=True) + EPS) * ng_ref[:, gs]
        y_ref[:, gs] = y.astype(y_ref.dtype)

        @pl.when(ci == nci - 1)
        def _():
            snew_ref[0, g * hpg:(g + 1) * hpg] = st_new.T.reshape(hpg, C_HEAD_DIM, D_STATE)


def _ssd_call(p, cprev, sprev, cw, cb, dtb, alog, dsk, ng, *, batch, seq, cs, row0):
    nci = seq // cs
    rb0 = row0 // cs

    def pcol(width, off):
        return pl.BlockSpec((cs, width), lambda b, i: (rb0 + b * nci + i, off // width))

    def row(width, rows=1):
        return pl.BlockSpec((rows, width), lambda b, i: (0, 0))

    kern = functools.partial(_ssd_kernel, cs=cs)
    return pl.pallas_call(
        kern,
        grid=(batch, nci),
        in_specs=[pcol(C_D_INNER, COL_Z), pcol(C_CONV_DIM, COL_XBC), pcol(LANES, COL_MISC),
                  pl.BlockSpec((1, CONV_WIDTH - 1, C_CONV_DIM), lambda b, i: (b, 0, 0)),
                  pl.BlockSpec((1, C_HEADS, C_HEAD_DIM, D_STATE), lambda b, i: (b, 0, 0, 0)),
                  row(C_CONV_DIM, CONV_WIDTH), row(C_CONV_DIM), row(LANES), row(LANES),
                  row(C_D_INNER), row(C_D_INNER)],
        out_specs=[pl.BlockSpec((cs, C_D_INNER), lambda b, i: (b * nci + i, 0)),
                   pl.BlockSpec((1, C_HEADS, C_HEAD_DIM, D_STATE), lambda b, i: (b, 0, 0, 0))],
        out_shape=[jax.ShapeDtypeStruct((batch * seq, C_D_INNER), BF16),
                   jax.ShapeDtypeStruct((batch, C_HEADS, C_HEAD_DIM, D_STATE), F32)],
        scratch_shapes=[pltpu.VMEM((cs + 8, C_CONV_DIM), F32),
                        pltpu.VMEM((C_GROUPS, D_STATE, C_HEADS // C_GROUPS * C_HEAD_DIM), F32)],
        compiler_params=_params(("parallel", "arbitrary")),
        name="ssd_mixer",
    )(p, p, p, cprev, sprev, cw, cb, dtb, alog, dsk, ng)


def _outproj_kernel(x_ref, yap_ref, ybp_ref, ycp_ref, yas_ref, ybs_ref, ycs_ref, wa_ref, wb_ref, wc_ref, o_ref, *,
                    prompt_tiles):
    def mix(ya_ref, yb_ref, yc_ref):
        o_ref[...] = (x_ref[...] + _dot(ya_ref[...], wa_ref[...]) + _dot(yb_ref[...], wb_ref[...])
                      + _dot(yc_ref[...], wc_ref[...]))

    is_prompt = pl.program_id(0) < prompt_tiles

    @pl.when(is_prompt)
    def _():
        mix(yap_ref, ybp_ref, ycp_ref)

    @pl.when(jnp.logical_not(is_prompt))
    def _():
        mix(yas_ref, ybs_ref, ycs_ref)


def _outproj(x, y_prompt, y_sample, w):
    t, d = x.shape
    tp, ts = y_prompt[0].shape[0], y_sample[0].shape[0]
    tm = _tile(math.gcd(tp, ts), 256)
    npt = tp // tm

    def tok(width):
        return pl.BlockSpec((tm, width), lambda i: (i, 0))

    def ptok(width):
        return pl.BlockSpec((tm, width), lambda i: (jnp.minimum(i, npt - 1), 0))

    def stok(width):
        return pl.BlockSpec((tm, width), lambda i: (jnp.maximum(i - npt, 0), 0))

    widths = (A_WIDTH, B_WIDTH, C_D_INNER)
    return pl.pallas_call(
        functools.partial(_outproj_kernel, prompt_tiles=npt),
        grid=(t // tm,),
        in_specs=[tok(d)] + [ptok(wd) for wd in widths] + [stok(wd) for wd in widths] + [
            pl.BlockSpec((A_WIDTH, d), lambda i: (0, 0)),
            pl.BlockSpec((B_WIDTH, d), lambda i: (1, 0)),
            pl.BlockSpec((C_D_INNER, d), lambda i: (1, 0))],
        out_specs=tok(d),
        out_shape=jax.ShapeDtypeStruct((t, d), F32),
        compiler_params=_params(("parallel",)),
        name="out_proj",
    )(x, *y_prompt, *y_sample, w, w, w)


def _ffn_kernel(x_ref, g_ref, wg_ref, wu_ref, wd_ref, o_ref, h_ref):
    @pl.when(pl.program_id(1) == 0)
    def _():
        x = x_ref[...]
        ms = jnp.mean(x * x, axis=-1, keepdims=True)
        h_ref[...] = (x * lax.rsqrt(ms + EPS) * g_ref[...]).astype(BF16)
        o_ref[...] = x

    h = h_ref[...]
    act = _silu(_dot(h, wg_ref[...])) * _dot(h, wu_ref[...])
    o_ref[...] += _dot(act.astype(BF16), wd_ref[...])


def _ffn(x, g, wg, wu, wd):
    t, d = x.shape
    f = wg.shape[1]
    tm = _tile(t, 1024)
    tf = 512
    once = pl.Buffered(1)
    return pl.pallas_call(
        _ffn_kernel,
        grid=(t // tm, f // tf),
        in_specs=[pl.BlockSpec((tm, d), lambda i, j: (i, 0), pipeline_mode=once),
                  pl.BlockSpec((1, d), lambda i, j: (0, 0)),
                  pl.BlockSpec((d, tf), lambda i, j: (0, j)),
                  pl.BlockSpec((d, tf), lambda i, j: (0, j)),
                  pl.BlockSpec((tf, d), lambda i, j: (j, 0))],
        out_specs=pl.BlockSpec((tm, d), lambda i, j: (i, 0), pipeline_mode=once),
        out_shape=jax.ShapeDtypeStruct((t, d), F32),
        scratch_shapes=[pltpu.VMEM((tm, d), BF16)],
        compiler_params=_params(("parallel", "arbitrary")),
        name="swiglu",
    )(x, g, wg, wu, wd)


def _rope_rows(pos):
    half = ROPE_DIM // 2
    inv = 1.0 / (ROPE_THETA ** (jnp.arange(0, ROPE_DIM, 2, dtype=F32) / ROPE_DIM))
    ang = pos.astype(F32)[:, None] * inv[None, :]
    cos, sin = jnp.cos(ang), jnp.sin(ang)
    n = pos.shape[0]
    one = jnp.ones((n, HEAD_DIM - ROPE_DIM), F32)
    zero = jnp.zeros((n, HEAD_DIM - ROPE_DIM), F32)
    zh = jnp.zeros((n, half), F32)
    rc = jnp.concatenate([cos, cos, one], axis=1)
    rs1 = jnp.concatenate([zh, sin, zero], axis=1)
    rs2 = jnp.concatenate([-sin, zh, zero], axis=1)
    return tuple(jnp.tile(a, (1, LANES // HEAD_DIM)) for a in (rc, rs1, rs2))


def _w_in_segments():
    offs = [0]
    for n in IN_SIZES:
        offs.append(offs[-1] + n)
    dst = (COL_AQ, COL_AK, COL_AV, COL_IQ, COL_MISC + MISC_IK, COL_MISC + MISC_IW, COL_BQ, COL_BK, COL_BV,
           COL_Z, COL_XBC, COL_MISC + MISC_DT)
    return [(offs[i], dst[i], IN_SIZES[i]) for i in range(len(IN_SIZES))]


def _regroup_kernel(w_ref, o_ref):
    for src, dst, width in _w_in_segments():
        o_ref[:, dst:dst + width] = w_ref[0, :, src:src + width].astype(BF16)
    used = COL_MISC + MISC_DT + C_HEADS
    o_ref[:, used:] = jnp.zeros((o_ref.shape[0], P_COLS - used), BF16)


def _rearrange_w_in(w, layer):
    _, d, n = w.shape
    tr = _tile(d, 256)
    return pl.pallas_call(
        _regroup_kernel,
        grid=(d // tr,),
        in_specs=[pl.BlockSpec((1, tr, n), lambda i: (layer, i, 0))],
        out_specs=pl.BlockSpec((tr, P_COLS), lambda i: (i, 0)),
        out_shape=jax.ShapeDtypeStruct((d, P_COLS), BF16),
        compiler_params=_params(("parallel",)),
        name="regroup_w_in",
    )(w)


def _lane_row(v, off):
    return jnp.zeros((1, LANES), F32).at[0, off:off + v.shape[0]].set(v)


def _gain_row(g):
    return jnp.tile(g, LANES // HEAD_DIM)[None, :]


def kernel(x_prompt, x_sample, cache_a_k, cache_a_v, cache_a_idx_k, cache_b_k, cache_b_v, state_c_ssm, state_c_conv, norm1_g, w_in, a_q_norm_g, a_k_norm_g, idx_k_norm_g, b_q_norm_g, b_k_norm_g, c_conv_w, c_conv_b, c_dt_bias, c_a_log, c_d, c_norm_g, w_out, norm2_g, w_gate, w_up, w_down):
    bp, sp, d = x_prompt.shape
    bs, ss, _ = x_sample.shape
    depth = w_in.shape[0]
    past = cache_a_k.shape[2]
    tp, ts = bp * sp, bs * ss
    tq_p = 128
    cs_p = 128 if sp % 128 == 0 else sp
    assert ss >= CONV_WIDTH - 1 and sp >= CONV_WIDTH - 1

    pos = jnp.concatenate([jnp.tile(jnp.arange(sp, dtype=I32), bp),
                           jnp.tile(past + jnp.arange(ss, dtype=I32), bs)])
    rc, rs1, rs2 = _rope_rows(pos)
    x = jnp.concatenate([x_prompt.reshape(tp, d), x_sample.reshape(ts, d)], axis=0)
    zero_conv = jnp.zeros((bp, CONV_WIDTH - 1, C_CONV_DIM), F32)
    zero_ssm = jnp.zeros((bp, C_HEADS, C_HEAD_DIM, D_STATE), F32)

    p_states, s_states = [], []
    for l in range(depth):
        p = _proj(x, norm1_g[l][None, :], _rearrange_w_in(w_in, l))
        gik = jnp.concatenate([idx_k_norm_g[l], jnp.ones((LANES - IDX_DIM,), F32)])[None, :]
        (aq, ak, iq, ik, iw, bq, bk, bk16, bv16) = _post(
            p, rc, rs1, rs2, _gain_row(a_q_norm_g[l]), _gain_row(a_k_norm_g[l]), gik,
            _gain_row(b_q_norm_g[l]), _gain_row(b_k_norm_g[l]))

        a_caches = (jnp.transpose(cache_a_idx_k, (0, 1, 3, 2)), jnp.transpose(cache_a_k, (0, 1, 3, 4, 2)),
                    jnp.transpose(cache_a_v, (0, 1, 3, 4, 2)))
        ya_p = _dsap_call(aq, iq, iw, ik, ak, p, None, batch=bp, seq_q=sp, tq=tq_p, row0=0)
        ya_s = _dsap_call(aq, iq, iw, ik, ak, p, a_caches, batch=bs, seq_q=ss, tq=ss, row0=tp, layer=l)
        b_caches = (jnp.transpose(cache_b_k, (0, 1, 3, 4, 2)), jnp.transpose(cache_b_v, (0, 1, 3, 4, 2)))
        yb_p = _sb_call(bq, bk16, bv16, None, batch=bp, seq_q=sp, tq=tq_p, row0=0)
        yb_s = _sb_call(bq, bk16, bv16, b_caches, batch=bs, seq_q=ss, tq=ss, row0=tp, layer=l)
        cw, cb = c_conv_w[l], c_conv_b[l][None, :]
        dtb, alog = _lane_row(c_dt_bias[l], MISC_DT), _lane_row(c_a_log[l], MISC_DT)
        dsk, ng = jnp.repeat(c_d[l], C_HEAD_DIM)[None, :], c_norm_g[l][None, :]
        yc_p, ssm_p = _ssd_call(p, zero_conv, zero_ssm, cw, cb, dtb, alog, dsk, ng,
                                batch=bp, seq=sp, cs=cs_p, row0=0)
        yc_s, ssm_s = _ssd_call(p, state_c_conv[l], state_c_ssm[l], cw, cb, dtb, alog, dsk, ng,
                                batch=bs, seq=ss, cs=ss, row0=tp)

        x = _outproj(x, (ya_p, yb_p, yc_p), (ya_s, yb_s, yc_s), w_out[l].astype(BF16))
        x = _ffn(x, norm2_g[l][None, :], w_gate[l].astype(BF16), w_up[l].astype(BF16), w_down[l].astype(BF16))

        def state(lo, hi, b, s, ssm):
            av = p[lo:hi, COL_AV:COL_AV + LANES]
            bv = p[lo:hi, COL_BV:COL_BV + B_WIDTH]
            conv = jnp.stack([p[lo + (i + 1) * s - (CONV_WIDTH - 1):lo + (i + 1) * s, COL_XBC:COL_XBC + C_CONV_DIM]
                              for i in range(b)])
            return (ak[lo:hi].reshape(b, s, A_KV_HEADS, HEAD_DIM), av.reshape(b, s, A_KV_HEADS, HEAD_DIM),
                    ik[lo:hi].reshape(b, s, IDX_DIM), bk[lo:hi].reshape(b, s, B_HEADS, HEAD_DIM),
                    bv.reshape(b, s, B_HEADS, HEAD_DIM), ssm, conv)

        p_states.append(state(0, tp, bp, sp, ssm_p))
        s_states.append(state(tp, tp + ts, bs, ss, ssm_s))

    p_out = [jnp.stack(zs) for zs in zip(*p_states)]
    s_out = [jnp.stack(zs) for zs in zip(*s_states)]
    return (x[:tp].reshape(bp, sp, d), x[tp:].reshape(bs, ss, d), *p_out, *s_out)
```

```python
import functools
import math

import jax
import jax.numpy as jnp
from jax import lax
from jax.experimental import pallas as pl
from jax.experimental.pallas import tpu as pltpu

F32 = jnp.float32
BF16 = jnp.bfloat16
I32 = jnp.int32

D_MODEL = 2048
EPS = 1e-6
CHUNK = 64
HEAD_DIM = 64
ROPE_DIM = HEAD_DIM // 4
ROPE_THETA = 500000.0
A_HEADS = 8
A_KV_HEADS = 2
A_WIDTH = A_HEADS * HEAD_DIM
IDX_HEADS = 16
IDX_DIM = 64
TOPK_MAX = 256
B_HEADS = 8
B_WIDTH = B_HEADS * HEAD_DIM
C_D_INNER = D_MODEL // 2
C_HEAD_DIM = 64
C_HEADS = C_D_INNER // C_HEAD_DIM
C_GROUPS = 4
D_STATE = 128
CONV_WIDTH = 4
C_CONV_DIM = C_D_INNER + 2 * C_GROUPS * D_STATE
IN_SIZES = (A_WIDTH, A_KV_HEADS * HEAD_DIM, A_KV_HEADS * HEAD_DIM, IDX_HEADS * IDX_DIM, IDX_DIM, IDX_HEADS,
            B_WIDTH, B_WIDTH, B_WIDTH, C_D_INNER, C_CONV_DIM, C_HEADS)

LANES = 128
VMEM_LIMIT = 48 * 1024 * 1024

COL_XBC, COL_IQ, COL_Z, COL_AQ, COL_BQ, COL_BK, COL_BV, COL_AK, COL_AV, COL_MISC = (
    0, 2048, 3072, 4096, 4608, 5120, 5632, 6144, 6272, 6400)
P_COLS = 6656
MISC_IK, MISC_IW, MISC_DT = 0, 64, 80

NEG_BIG = -1e30
INT_MIN = -2 ** 31
KEY_POS_INF = 0x7F800000
KEY_NEG_INF = INT_MIN + 0x7FFFFF
SB_DEAD = -104.0


def _params(sem):
    return pltpu.CompilerParams(dimension_semantics=sem, vmem_limit_bytes=VMEM_LIMIT)


def _tile(n, pref):
    t = min(n, pref)
    while n % t:
        t -= 8
    return t


def _dot_nt(a, b):
    return lax.dot_general(a, b, (((1,), (1,)), ((), ())), preferred_element_type=F32)


def _dot_tn(a, b):
    return lax.dot_general(a, b, (((0,), (0,)), ((), ())), preferred_element_type=F32)


def _dot(a, b):
    return jnp.dot(a, b, preferred_element_type=F32)


def _split3(x):
    h1 = x.astype(BF16)
    r1 = x - h1.astype(F32)
    h2 = r1.astype(BF16)
    h3 = (r1 - h2.astype(F32)).astype(BF16)
    return h1, h2, h3


def _softplus(x):
    return jnp.maximum(x, 0.0) + jnp.log1p(jnp.exp(-jnp.abs(x)))


def _silu(x):
    return x * (1.0 / (1.0 + jnp.exp(-x)))


def _proj_kernel(x_ref, g_ref, w_ref, o_ref, h_ref):
    @pl.when(pl.program_id(1) == 0)
    def _():
        x = x_ref[...]
        ms = jnp.mean(x * x, axis=-1, keepdims=True)
        h_ref[...] = (x * lax.rsqrt(ms + EPS) * g_ref[...]).astype(BF16)

    o_ref[...] = _dot(h_ref[...], w_ref[...])


def _proj(x, g, w):
    t, d = x.shape
    n = w.shape[1]
    tm = _tile(t, 1024)
    tn = 512
    return pl.pallas_call(
        _proj_kernel,
        grid=(t // tm, n // tn),
        in_specs=[pl.BlockSpec((tm, d), lambda i, j: (i, 0)),
                  pl.BlockSpec((1, d), lambda i, j: (0, 0)),
                  pl.BlockSpec((d, tn), lambda i, j: (0, j))],
        out_specs=pl.BlockSpec((tm, tn), lambda i, j: (i, j)),
        out_shape=jax.ShapeDtypeStruct((t, n), F32),
        scratch_shapes=[pltpu.VMEM((tm, d), BF16)],
        compiler_params=_params(("parallel", "arbitrary")),
        name="in_proj",
    )(x, g, w)


def _post_kernel(aq_r, bq_r, bk_r, bv_r, iq_r, ak_r, misc_r, rc_r, rs1_r, rs2_r,
                 gaq_r, gak_r, gik_r, gbq_r, gbk_r,
                 aq_o, ak_o, iq_o, ik_o, iw_o, bq_o, bk_o, bk16_o, bv16_o):
    r = lax.broadcasted_iota(I32, (LANES, LANES), 0) // HEAD_DIM
    c = lax.broadcasted_iota(I32, (LANES, LANES), 1) // HEAD_DIM
    segm = jnp.where(r == c, 1.0, 0.0).astype(BF16)
    rc, rs1, rs2 = rc_r[...], rs1_r[...], rs2_r[...]

    def headnorm(xc, g_row):
        h1, h2, h3 = _split3(xc * xc)
        ss = _dot(h1, segm) + _dot(h2, segm) + _dot(h3, segm)
        return xc * lax.rsqrt(ss * (1.0 / HEAD_DIM) + EPS) * g_row

    def rope(xc):
        return xc * rc + pltpu.roll(xc, 8, 1) * rs1 + pltpu.roll(xc, LANES - 8, 1) * rs2

    for k in range(A_WIDTH // LANES):
        sl = slice(k * LANES, (k + 1) * LANES)
        aq_o[:, sl] = (rope(headnorm(aq_r[:, sl], gaq_r[...])) * (HEAD_DIM ** -0.5)).astype(BF16)
        bq_o[:, sl] = (headnorm(bq_r[:, sl], gbq_r[...]) * (HEAD_DIM ** -0.5)).astype(BF16)
        bk = headnorm(bk_r[:, sl], gbk_r[...])
        bk_o[:, sl] = bk
        bk16_o[:, sl] = bk.astype(BF16)
        bv16_o[:, sl] = bv_r[:, sl].astype(BF16)
    for k in range(IDX_HEADS * IDX_DIM // LANES):
        sl = slice(k * LANES, (k + 1) * LANES)
        iq_o[:, sl] = rope(iq_r[:, sl]).astype(BF16)
    ak_o[...] = rope(headnorm(ak_r[...], gak_r[...]))
    misc = misc_r[...]
    ik_o[...] = rope(headnorm(misc, gik_r[...]))[:, MISC_IK:MISC_IK + IDX_DIM]
    iw_o[...] = misc[:, MISC_IW:MISC_IW + IDX_HEADS] * (IDX_HEADS ** -0.5 * IDX_DIM ** -0.5)


def _post(p, rc, rs1, rs2, gaq, gak, gik, gbq, gbk):
    t = p.shape[0]
    tm = _tile(t, 256)

    def col(width, off):
        return pl.BlockSpec((tm, width), lambda i: (i, off // width))

    def row(width):
        return pl.BlockSpec((1, width), lambda i: (0, 0))

    def out(width):
        return pl.BlockSpec((tm, width), lambda i: (i, 0))

    widths_dtypes = [(A_WIDTH, BF16), (LANES, F32), (IDX_HEADS * IDX_DIM, BF16), (IDX_DIM, F32),
                     (IDX_HEADS, F32), (B_WIDTH, BF16), (B_WIDTH, F32), (B_WIDTH, BF16), (B_WIDTH, BF16)]
    return pl.pallas_call(
        _post_kernel,
        grid=(t // tm,),
        in_specs=[col(A_WIDTH, COL_AQ), col(B_WIDTH, COL_BQ), col(B_WIDTH, COL_BK), col(B_WIDTH, COL_BV),
                  col(IDX_HEADS * IDX_DIM, COL_IQ), col(LANES, COL_AK), col(LANES, COL_MISC),
                  out(LANES), out(LANES), out(LANES),
                  row(LANES), row(LANES), row(LANES), row(LANES), row(LANES)],
        out_specs=[out(w) for w, _ in widths_dtypes],
        out_shape=[jax.ShapeDtypeStruct((t, w), dt) for w, dt in widths_dtypes],
        compiler_params=_params(("parallel",)),
        name="head_post",
    )(p, p, p, p, p, p, p, rc, rs1, rs2, gaq, gak, gik, gbq, gbk)


CACHE_CHUNK = 2048
NEW_CHUNK = 512


def _dsap_kernel(*refs, tq, sn, lc, topk, kblk, has_cache, layer, cache_chunk):
    nb = LANES // tq
    if has_cache:
        (aq_ref, iq_ref, iw_ref, ikn_ref, akn_ref, avn_ref, ikc_hbm, akc_hbm, avc_hbm, o_ref,
         key_ref, bias_ref, ik16_ref, ak16_ref, vt_ref, iqh_ref, aqh_ref, m_ref, acc_ref,
         ikbuf, akbuf, avbuf, sem) = refs
    else:
        (aq_ref, iq_ref, iw_ref, ikn_ref, akn_ref, avn_ref, o_ref,
         key_ref, bias_ref, ik16_ref, ak16_ref, vt_ref, iqh_ref, aqh_ref, m_ref, acc_ref) = refs
    grp = pl.program_id(0)
    qi = pl.program_id(1)
    n_keys = lc + sn
    lp = ik16_ref.shape[0]
    rep = A_HEADS // A_KV_HEADS
    ch = cache_chunk
    chn = NEW_CHUNK

    @pl.when(qi == 0)
    def _():
        def put(row, j, ik, k0, k1, v_t):
            n = ik.shape[0]
            cols = slice(j * IDX_DIM, (j + 1) * IDX_DIM)
            ik16_ref[pl.ds(row, n), cols] = ik.astype(BF16)
            ak16_ref[0, pl.ds(row, n), cols] = k0.astype(BF16)
            ak16_ref[1, pl.ds(row, n), cols] = k1.astype(BF16)
            vt = v_t.astype(BF16)
            for e in range(n // kblk):
                for g in range(A_KV_HEADS):
                    vt_ref[g, row // kblk + e, 0:HEAD_DIM, j * kblk:(j + 1) * kblk] = (
                        vt[g * HEAD_DIM:(g + 1) * HEAD_DIM, e * kblk:(e + 1) * kblk])

        if has_cache:
            nc = lc // ch

            def copies(j, c, slot):
                b = grp * nb + j
                src = pl.ds(pl.multiple_of(c * ch, ch), ch)
                return (pltpu.make_async_copy(ikc_hbm.at[layer, b, :, src], ikbuf.at[slot], sem.at[0, slot]),
                        pltpu.make_async_copy(akc_hbm.at[layer, b, :, :, src], akbuf.at[slot], sem.at[1, slot]),
                        pltpu.make_async_copy(avc_hbm.at[layer, b, :, :, src], avbuf.at[slot], sem.at[2, slot]))

            def start(j, c, slot):
                for cp in copies(j, c, slot):
                    cp.start()

            start(0, 0, 0)
            for j in range(nb):
                def chunk_body(c, carry, j=j):
                    slot = c % 2
                    for cp in copies(j, c, slot):
                        cp.wait()

                    @pl.when(c + 1 < nc)
                    def _():
                        start(j, c + 1, 1 - slot)

                    if j + 1 < nb:
                        @pl.when(c + 1 == nc)
                        def _():
                            start(j + 1, 0, 0)

                    kk = akbuf[slot].reshape(LANES, ch).T
                    ik = jnp.concatenate([ikbuf[slot], jnp.zeros((LANES - IDX_DIM, ch), F32)], axis=0).T
                    put(pl.multiple_of(c * ch, ch), j, ik[:, 0:IDX_DIM], kk[:, 0:HEAD_DIM], kk[:, HEAD_DIM:LANES],
                        avbuf[slot].reshape(LANES, ch))
                    return carry

                lax.fori_loop(0, nc, chunk_body, 0)

        for j in range(nb):
            if sn % chn == 0:
                def new_body(c, carry, j=j):
                    src = pl.ds(pl.multiple_of(j * sn + c * chn, chn), chn)
                    ak = akn_ref[src, :]
                    put(pl.multiple_of(lc + c * chn, chn), j, ikn_ref[src, :], ak[:, 0:HEAD_DIM], ak[:, HEAD_DIM:LANES],
                        avn_ref[src, :].T)
                    return carry

                lax.fori_loop(0, sn // chn, new_body, 0)
            else:
                def padded(ref):
                    return jnp.concatenate([ref[j * sn:(j + 1) * sn, :],
                                            jnp.zeros((kblk - sn, ref.shape[-1]), F32)], axis=0)

                ak = padded(akn_ref)
                put(lc, j, padded(ikn_ref), ak[:, 0:HEAD_DIM], ak[:, HEAD_DIM:LANES], padded(avn_ref).T)
        for g in range(A_KV_HEADS):
            vt_ref[g, :, HEAD_DIM:LANES, :] = jnp.ones((lp // kblk, LANES - HEAD_DIM, nb * kblk), BF16)

    qstart = lc + qi * tq
    kmax = jnp.minimum(n_keys, ((qstart + tq - 1) // CHUNK + 1) * CHUNK)
    nkb = (kmax + kblk - 1) // kblk
    lane = lax.broadcasted_iota(I32, (1, LANES), 1)
    qpos = qstart + lane % tq
    krow = lax.broadcasted_iota(I32, (kblk, 1), 0)
    lane_group = lane // tq
    two = 2 * LANES

    def head_major(dst, src, nheads):
        x_t = src[...].astype(F32).T
        for h in range(nheads):
            xh = x_t[h * HEAD_DIM:(h + 1) * HEAD_DIM, :]
            if nb > 1:
                xh = jnp.concatenate([jnp.where(lane_group == j, xh, 0.0) for j in range(nb)], axis=0)
            dst[:, h * LANES:(h + 1) * LANES] = xh.astype(BF16)

    head_major(iqh_ref, iq_ref, IDX_HEADS)
    head_major(aqh_ref, aq_ref, A_HEADS)
    iw_t = jnp.concatenate([iw_ref[...], jnp.zeros((LANES, LANES - IDX_HEADS), F32)], axis=1).T

    def score_body(kb, carry):
        start = pl.multiple_of(kb * kblk, kblk)
        ikb = ik16_ref[pl.ds(start, kblk), :]
        acc = jnp.zeros((kblk, LANES), F32)
        for pr in range(IDX_HEADS // 2):
            logit = _dot(ikb, iqh_ref[:, pr * two:(pr + 1) * two])
            for e in range(2):
                h = 2 * pr + e
                acc = acc + iw_t[h:h + 1, :] * jnp.maximum(logit[:, e * LANES:(e + 1) * LANES], 0.0)
        kpos = start + krow
        adm = jnp.logical_and(kpos // CHUNK <= qpos // CHUNK, kpos < n_keys)
        bits = pltpu.bitcast(jnp.where(adm, acc, -jnp.inf), I32)
        bits = jnp.where(bits == INT_MIN, 0, bits)
        key_ref[kb] = bits ^ ((bits >> 31) & 0x7FFFFFFF)
        return carry

    lax.fori_loop(0, nkb, score_body, 0)

    def count(pred):
        def body(kb, acc):
            hit = jnp.where(pred(key_ref[kb], kb), 1.0, 0.0)
            return acc + jnp.sum(hit.reshape(kblk // 64, 64, LANES), axis=0)
        acc = lax.fori_loop(0, nkb, body, jnp.zeros((64, LANES), F32))
        return jnp.sum(acc, axis=0, keepdims=True)

    kf = float(topk)

    def bit_body(i, thr):
        cand = thr + (jnp.int32(1) << (31 - i))
        cnt = count(lambda key, kb: key >= cand)
        return jnp.where(cnt >= kf, cand, thr)

    thr = lax.fori_loop(0, 32, bit_body, jnp.full((1, LANES), INT_MIN, I32))
    n_gt = count(lambda key, kb: key > thr)
    n_ge = count(lambda key, kb: key >= thr)
    need = kf - n_gt

    def tie_search():
        def jbody(i, jl):
            cand = jl + (jnp.int32(1) << (14 - i))
            cnt = count(lambda key, kb: jnp.logical_and(key == thr, kb * kblk + krow < cand))
            return jnp.where(cnt <= need, cand, jl)
        return lax.fori_loop(0, 15, jbody, jnp.zeros((1, LANES), I32))

    jlim = lax.cond(jnp.max(n_ge) > kf, tie_search, lambda: jnp.full((1, LANES), 2 ** 30, I32))

    def bias_body(kb, carry):
        key = key_ref[kb]
        kpos = kb * kblk + krow
        sel = jnp.logical_or(key > thr, jnp.logical_and(key == thr, kpos < jlim))
        sel = jnp.logical_and(sel, jnp.logical_and(key > KEY_NEG_INF, key < KEY_POS_INF))
        bias_ref[kb] = jnp.where(sel, 0.0, NEG_BIG)
        return carry

    lax.fori_loop(0, nkb, bias_body, 0)

    m_ref[...] = jnp.full(m_ref.shape, -3e38, F32)
    acc_ref[...] = jnp.zeros(acc_ref.shape, F32)
    pair_group = (lax.broadcasted_iota(I32, (1, 2 * LANES), 1) % LANES) // tq

    def att_body(kb, carry):
        start = pl.multiple_of(kb * kblk, kblk)
        bias = bias_ref[kb]
        m_all = m_ref[...]
        kks = [ak16_ref[g, pl.ds(start, kblk), :] for g in range(A_KV_HEADS)]
        s_ts = [_dot(kks[h0 // rep], aqh_ref[:, h0 * LANES:(h0 + 2) * LANES]) for h0 in range(0, A_HEADS, 2)]
        ps, alphas, m_rows = [], [], []
        for h in range(A_HEADS):
            x = s_ts[h // 2][:, (h % 2) * LANES:(h % 2 + 1) * LANES] + bias
            part = jnp.max(x.reshape(kblk // 64, 64, LANES), axis=0)
            bmax = jnp.max(part, axis=0, keepdims=True)
            m_old = m_all[h:h + 1, :]
            m_new = jnp.maximum(m_old, bmax)
            alphas.append(jnp.exp(m_old - m_new))
            ps.append(jnp.exp(x - m_new).astype(BF16))
            m_rows.append(m_new)
        m_ref[...] = jnp.concatenate(m_rows, axis=0)
        upds = []
        for h0 in range(0, A_HEADS, 2):
            pp = jnp.concatenate(ps[h0:h0 + 2], axis=1)
            if nb > 1:
                pp = jnp.concatenate([jnp.where(pair_group == j, pp, jnp.zeros_like(pp)) for j in range(nb)], axis=0)
            upds.append(_dot(vt_ref[h0 // rep, kb], pp))
        for h in range(A_HEADS):
            acc_ref[h] = acc_ref[h] * alphas[h] + upds[h // 2][:, (h % 2) * LANES:(h % 2 + 1) * LANES]
        return carry

    lax.fori_loop(0, nkb, att_body, 0)

    outs = []
    for h in range(A_HEADS):
        a = acc_ref[h]
        outs.append(a[0:HEAD_DIM, :] / a[HEAD_DIM:LANES, :])
    o_ref[...] = jnp.concatenate(outs, axis=0).T.astype(o_ref.dtype)


def _dsap_call(aq, iq, iw, ik, ak, p, caches, *, batch, seq_q, tq, row0, layer=0, kblk=256):
    has_cache = caches is not None
    lc = caches[0].shape[-1] if has_cache else 0
    sn = seq_q
    nb = LANES // tq
    n_keys = lc + sn
    lp = -(-n_keys // kblk) * kblk
    assert LANES % tq == 0 and batch % nb == 0 and sn % tq == 0 and row0 % (nb * sn) == 0 and row0 % LANES == 0
    cache_chunk = min(CACHE_CHUNK, lc // 2) if has_cache else 0
    assert not has_cache or (lc % (2 * cache_chunk) == 0 and cache_chunk % NEW_CHUNK == 0)
    assert NEW_CHUNK % kblk == 0 and ((sn % NEW_CHUNK == 0 and lp == n_keys) or (sn < kblk and lp == lc + kblk))
    groups = batch // nb
    nqb = seq_q // tq
    assert nb == 1 or nqb == 1
    rb0 = row0 // LANES
    nb0 = row0 // (nb * sn)
    topk = min(TOPK_MAX, n_keys // 4)
    nkb_max = lp // kblk

    def qspec(width):
        return pl.BlockSpec((LANES, width), lambda g, i: (rb0 + g * nqb + i, 0))

    def nspec(width, colblk=0):
        return pl.BlockSpec((nb * sn, width), lambda g, i: (nb0 + g, colblk))

    in_specs = [qspec(A_WIDTH), qspec(IDX_HEADS * IDX_DIM), qspec(IDX_HEADS),
                nspec(IDX_DIM), nspec(LANES), nspec(LANES, COL_AV // LANES)]
    args = [aq, iq, iw, ik, ak, p]
    scratch = [pltpu.VMEM((nkb_max, kblk, LANES), I32),
               pltpu.VMEM((nkb_max, kblk, LANES), F32),
               pltpu.VMEM((lp, nb * IDX_DIM), BF16),
               pltpu.VMEM((A_KV_HEADS, lp, nb * HEAD_DIM), BF16),
               pltpu.VMEM((A_KV_HEADS, nkb_max, LANES, nb * kblk), BF16),
               pltpu.VMEM((nb * IDX_DIM, IDX_HEADS * LANES), BF16),
               pltpu.VMEM((nb * HEAD_DIM, A_HEADS * LANES), BF16),
               pltpu.VMEM((A_HEADS, LANES), F32),
               pltpu.VMEM((A_HEADS, LANES, LANES), F32)]
    if has_cache:
        in_specs += [pl.BlockSpec(memory_space=pl.ANY)] * 3
        args += list(caches)
        scratch += [pltpu.VMEM((2, IDX_DIM, cache_chunk), F32),
                    pltpu.VMEM((2, A_KV_HEADS, HEAD_DIM, cache_chunk), F32),
                    pltpu.VMEM((2, A_KV_HEADS, HEAD_DIM, cache_chunk), F32),
                    pltpu.SemaphoreType.DMA((3, 2))]
    kern = functools.partial(_dsap_kernel, tq=tq, sn=sn, lc=lc, topk=topk, kblk=kblk, has_cache=has_cache,
                             layer=layer, cache_chunk=cache_chunk)
    return pl.pallas_call(
        kern,
        grid=(groups, nqb),
        in_specs=in_specs,
        out_specs=pl.BlockSpec((LANES, A_WIDTH), lambda g, i: (g * nqb + i, 0)),
        out_shape=jax.ShapeDtypeStruct((batch * seq_q, A_WIDTH), BF16),
        scratch_shapes=scratch,
        compiler_params=_params(("parallel", "arbitrary")),
        name="dsa_attention",
    )(*args)


def _sb_kernel(*refs, tq, sn, lc, kblk, has_cache, layer):
    if has_cache:
        q_ref, kn_ref, vn_ref, kc_hbm, vc_hbm, o_ref, qh_ref, carry_ref, acc_ref, kbuf, vbuf, sem = refs
    else:
        q_ref, kn_ref, vn_ref, o_ref, qh_ref, carry_ref, acc_ref = refs
    qlocal = pl.program_id(1) * tq
    qpos = qlocal + lax.broadcasted_iota(I32, (tq, 1), 0)
    kiota = lax.broadcasted_iota(I32, (1, kblk), 1)
    r = lax.broadcasted_iota(I32, (kblk, kblk), 0)
    c = lax.broadcasted_iota(I32, (kblk, kblk), 1)
    upper = jnp.where(r > c, 1.0, 0.0).astype(BF16)
    heads = range(B_HEADS)
    for h in heads:
        qh_ref[h] = q_ref[:, h * HEAD_DIM:(h + 1) * HEAD_DIM]
    carry_ref[...] = jnp.zeros(carry_ref.shape, F32)
    acc_ref[...] = jnp.zeros(acc_ref.shape, F32)

    def visit(score, apply, causal):
        zs = [score(h) for h in heads]
        sps = [_softplus(z) for z in zs]
        lks = [-sp if causal is None else jnp.where(causal, -sp, 0.0) for sp in sps]
        carries = [carry_ref[h] for h in heads]
        laters = []
        for h in heads:
            l1, l2, l3 = _split3(lks[h])
            laters.append(carries[h] + (_dot(l1, upper) + _dot(l2, upper) + _dot(l3, upper)))
        weights = [jnp.exp((zs[h] - sps[h]) + laters[h]) for h in heads]
        if causal is not None:
            weights = [jnp.where(causal, a, 0.0) for a in weights]
        upds = [apply(h, weights[h].astype(BF16)) for h in heads]
        new = [carries[h] + jnp.sum(lks[h], axis=1, keepdims=True) for h in heads]
        top = new[0]
        for h in heads:
            acc_ref[h] += upds[h]
            carry_ref[h] = new[h]
            top = jnp.maximum(top, new[h])
        return jnp.max(top) > SB_DEAD

    def alive_cond(state):
        j, alive = state
        return jnp.logical_and(j >= 0, alive)

    def hcols(h):
        return slice(h * HEAD_DIM, (h + 1) * HEAD_DIM)

    if sn >= kblk:
        def new_body(state):
            j, _ = state
            start = pl.multiple_of(j * kblk, kblk)
            alive = visit(lambda h: _dot_nt(qh_ref[h], kn_ref[pl.ds(start, kblk), hcols(h)]),
                          lambda h, a: _dot(a, vn_ref[pl.ds(start, kblk), hcols(h)]), start + kiota < qpos)
            return j - 1, alive

        _, alive = lax.while_loop(alive_cond, new_body, ((qlocal + tq - 1) // kblk, True))
    else:
        def padded(ref, h):
            return jnp.concatenate([ref[:, hcols(h)], jnp.zeros((kblk - sn, HEAD_DIM), BF16)], axis=0)

        alive = visit(lambda h: _dot_nt(qh_ref[h], padded(kn_ref, h)), lambda h, a: _dot(a, padded(vn_ref, h)),
                      kiota < qpos)

    if has_cache:
        b = pl.program_id(0)

        def copies(j, slot):
            src = pl.ds(pl.multiple_of(j * kblk, kblk), kblk)
            return (pltpu.make_async_copy(kc_hbm.at[layer, b, :, :, src], kbuf.at[slot], sem.at[0, slot]),
                    pltpu.make_async_copy(vc_hbm.at[layer, b, :, :, src], vbuf.at[slot], sem.at[1, slot]))

        def start(j, slot):
            for cp in copies(j, slot):
                cp.start()

        def wait(j, slot):
            for cp in copies(j, slot):
                cp.wait()

        j0 = lc // kblk - 1

        @pl.when(alive)
        def _():
            start(j0, j0 % 2)

        def cache_body(state):
            j, _ = state
            slot = j % 2
            wait(j, slot)

            @pl.when(j > 0)
            def _():
                start(j - 1, 1 - slot)

            alive = visit(lambda h: _dot(qh_ref[h], kbuf[slot, h].astype(BF16)),
                          lambda h, a: _dot_nt(a, vbuf[slot, h].astype(BF16)), None)
            return j - 1, alive

        j_end, _ = lax.while_loop(alive_cond, cache_body, (j0, alive))

        @pl.when(jnp.logical_and(alive, j_end >= 0))
        def _():
            wait(j_end, j_end % 2)

    for h in heads:
        o_ref[:, hcols(h)] = acc_ref[h].astype(o_ref.dtype)


def _sb_call(bq, bk16, bv16, caches, *, batch, seq_q, tq, row0, layer=0, kblk=128):
    has_cache = caches is not None
    lc = caches[0].shape[-1] if has_cache else 0
    sn = seq_q
    assert kblk == LANES and lc % kblk == 0 and row0 % sn == 0 and sn % tq == 0
    assert sn % kblk == 0 or (sn < kblk and tq == sn)
    nqb = seq_q // tq
    rb0 = row0 // tq
    nb0 = row0 // sn
    in_specs = [pl.BlockSpec((tq, B_WIDTH), lambda b, i: (rb0 + b * nqb + i, 0)),
                pl.BlockSpec((sn, B_WIDTH), lambda b, i: (nb0 + b, 0)),
                pl.BlockSpec((sn, B_WIDTH), lambda b, i: (nb0 + b, 0))]
    args = [bq, bk16, bv16]
    scratch = [pltpu.VMEM((B_HEADS, tq, HEAD_DIM), BF16), pltpu.VMEM((B_HEADS, tq, LANES), F32),
               pltpu.VMEM((B_HEADS, tq, HEAD_DIM), F32)]
    if has_cache:
        in_specs += [pl.BlockSpec(memory_space=pl.ANY)] * 2
        args += list(caches)
        scratch += [pltpu.VMEM((2, B_HEADS, HEAD_DIM, kblk), F32), pltpu.VMEM((2, B_HEADS, HEAD_DIM, kblk), F32),
                    pltpu.SemaphoreType.DMA((2, 2))]
    kern = functools.partial(_sb_kernel, tq=tq, sn=sn, lc=lc, kblk=kblk, has_cache=has_cache, layer=layer)
    return pl.pallas_call(
        kern,
        grid=(batch, nqb),
        in_specs=in_specs,
        out_specs=pl.BlockSpec((tq, B_WIDTH), lambda b, i: (b * nqb + i, 0)),
        out_shape=jax.ShapeDtypeStruct((batch * seq_q, B_WIDTH), BF16),
        scratch_shapes=scratch,
        compiler_params=_params(("parallel", "arbitrary")),
        name="stick_breaking",
    )(*args)


def _ssd_kernel(z_ref, xbc_ref, misc_ref, cprev_ref, sprev_ref, cw_ref, cb_ref, dtb_ref, alog_ref, dsk_ref,
                ng_ref, y_ref, snew_ref, xpad_ref, st_ref, *, cs):
    ci = pl.program_id(1)
    nci = pl.num_programs(1)
    hpg = C_HEADS // C_GROUPS
    gw = hpg * C_HEAD_DIM

    @pl.when(ci == 0)
    def _():
        xpad_ref[5:8, :] = cprev_ref[0]
        for g in range(C_GROUPS):
            st_ref[g] = sprev_ref[0, g * hpg:(g + 1) * hpg].reshape(gw, D_STATE).T

    @pl.when(ci > 0)
    def _():
        xpad_ref[5:8, :] = xpad_ref[cs + 5:cs + 8, :]

    xpad_ref[8:8 + cs, :] = xbc_ref[...]
    xc = cb_ref[...] + xpad_ref[5:5 + cs, :] * cw_ref[0:1, :]
    for w in range(1, CONV_WIDTH):
        xc = xc + xpad_ref[5 + w:5 + w + cs, :] * cw_ref[w:w + 1, :]
    xc = _silu(xc)

    dt = _softplus(misc_ref[...] + dtb_ref[...])
    a = dt * (-jnp.exp(alog_ref[...]))
    r = lax.broadcasted_iota(I32, (cs, cs), 0)
    c = lax.broadcasted_iota(I32, (cs, cs), 1)
    tril = r >= c
    tril16 = jnp.where(tril, 1.0, 0.0).astype(BF16)
    a1, a2, a3 = _split3(a)
    a_cum = _dot(tril16, a1) + _dot(tril16, a2) + _dot(tril16, a3)
    pad = LANES - cs
    a_sq = a_cum if pad == 0 else jnp.concatenate([a_cum, jnp.zeros((pad, LANES), F32)], axis=0)
    a_cum_t = a_sq.T
    a_last = a_cum[cs - 1:cs, :]
    exp_a = jnp.exp(a_cum)
    dte = jnp.exp(a_last - a_cum)
    exp_last = jnp.exp(a_last)
    low_half = lax.broadcasted_iota(I32, (1, LANES), 1) < C_HEAD_DIM

    for g in range(C_GROUPS):
        bm = xc[:, C_D_INNER + g * D_STATE:C_D_INNER + (g + 1) * D_STATE].astype(BF16)
        cm = xc[:, C_D_INNER + (C_GROUPS + g) * D_STATE:C_D_INNER + (C_GROUPS + g + 1) * D_STATE].astype(BF16)
        cb = _dot_nt(cm, bm)
        st = st_ref[g]
        y_off = _dot(cm, st.astype(BF16))
        y_pairs, x_pairs, dec_pairs = [], [], []
        for pp in range(hpg // 2):
            h0 = g * hpg + 2 * pp
            ln0, ln1 = MISC_DT + h0, MISC_DT + h0 + 1

            def pat(v, ln0=ln0, ln1=ln1):
                return jnp.where(low_half, v[:, ln0:ln0 + 1], v[:, ln1:ln1 + 1])

            xd = xc[:, h0 * C_HEAD_DIM:(h0 + 2) * C_HEAD_DIM] * pat(dt)
            xd16 = xd.astype(BF16)
            y_pair = y_off[:, pp * LANES:(pp + 1) * LANES] * pat(exp_a)
            for ln, keep in ((ln0, low_half), (ln1, jnp.logical_not(low_half))):
                seg = a_cum[:, ln:ln + 1] - a_cum_t[ln:ln + 1, 0:cs]
                decay = jnp.where(tril, jnp.exp(jnp.where(tril, seg, 0.0)), 0.0)
                y_pair = y_pair + _dot((cb * decay).astype(BF16), jnp.where(keep, xd16, jnp.zeros_like(xd16)))
            y_pairs.append(y_pair)
            x_pairs.append((xd * pat(dte)).astype(BF16))
            dec_pairs.append(pat(exp_last))
        y = jnp.concatenate(y_pairs, axis=1)
        st_new = st * jnp.concatenate(dec_pairs, axis=1) + _dot_tn(bm, jnp.concatenate(x_pairs, axis=1))
        st_ref[g] = st_new

        gs = slice(g * gw, (g + 1) * gw)
        y = y + xc[:, gs] * dsk_ref[:, gs]
        y = y * _silu(z_ref[:, gs])
        y = y * lax.rsqrt(jnp.mean(y * y, axis=1, keepdims=True) + EPS) * ng_ref[:, gs]
        y_ref[:, gs] = y.astype(y_ref.dtype)

        @pl.when(ci == nci - 1)
        def _():
            snew_ref[0, g * hpg:(g + 1) * hpg] = st_new.T.reshape(hpg, C_HEAD_DIM, D_STATE)


def _ssd_call(p, cprev, sprev, cw, cb, dtb, alog, dsk, ng, *, batch, seq, cs, row0):
    nci = seq // cs
    rb0 = row0 // cs

    def pcol(width, off):
        return pl.BlockSpec((cs, width), lambda b, i: (rb0 + b * nci + i, off // width))

    def row(width, rows=1):
        return pl.BlockSpec((rows, width), lambda b, i: (0, 0))

    kern = functools.partial(_ssd_kernel, cs=cs)
    return pl.pallas_call(
        kern,
        grid=(batch, nci),
        in_specs=[pcol(C_D_INNER, COL_Z), pcol(C_CONV_DIM, COL_XBC), pcol(LANES, COL_MISC),
                  pl.BlockSpec((1, CONV_WIDTH - 1, C_CONV_DIM), lambda b, i: (b, 0, 0)),
                  pl.BlockSpec((1, C_HEADS, C_HEAD_DIM, D_STATE), lambda b, i: (b, 0, 0, 0)),
                  row(C_CONV_DIM, CONV_WIDTH), row(C_CONV_DIM), row(LANES), row(LANES),
                  row(C_D_INNER), row(C_D_INNER)],
        out_specs=[pl.BlockSpec((cs, C_D_INNER), lambda b, i: (b * nci + i, 0)),
                   pl.BlockSpec((1, C_HEADS, C_HEAD_DIM, D_STATE), lambda b, i: (b, 0, 0, 0))],
        out_shape=[jax.ShapeDtypeStruct((batch * seq, C_D_INNER), BF16),
                   jax.ShapeDtypeStruct((batch, C_HEADS, C_HEAD_DIM, D_STATE), F32)],
        scratch_shapes=[pltpu.VMEM((cs + 8, C_CONV_DIM), F32),
                        pltpu.VMEM((C_GROUPS, D_STATE, C_HEADS // C_GROUPS * C_HEAD_DIM), F32)],
        compiler_params=_params(("parallel", "arbitrary")),
        name="ssd_mixer",
    )(p, p, p, cprev, sprev, cw, cb, dtb, alog, dsk, ng)


def _outproj_kernel(x_ref, yap_ref, ybp_ref, ycp_ref, yas_ref, ybs_ref, ycs_ref, wa_ref, wb_ref, wc_ref, o_ref, *,
                    prompt_tiles):
    def mix(ya_ref, yb_ref, yc_ref):
        o_ref[...] = (x_ref[...] + _dot(ya_ref[...], wa_ref[...]) + _dot(yb_ref[...], wb_ref[...])
                      + _dot(yc_ref[...], wc_ref[...]))

    is_prompt = pl.program_id(0) < prompt_tiles

    @pl.when(is_prompt)
    def _():
        mix(yap_ref, ybp_ref, ycp_ref)

    @pl.when(jnp.logical_not(is_prompt))
    def _():
        mix(yas_ref, ybs_ref, ycs_ref)


def _outproj(x, y_prompt, y_sample, w):
    t, d = x.shape
    tp, ts = y_prompt[0].shape[0], y_sample[0].shape[0]
    tm = _tile(math.gcd(tp, ts), 256)
    npt = tp // tm

    def tok(width):
        return pl.BlockSpec((tm, width), lambda i: (i, 0))

    def ptok(width):
        return pl.BlockSpec((tm, width), lambda i: (jnp.minimum(i, npt - 1), 0))

    def stok(width):
        return pl.BlockSpec((tm, width), lambda i: (jnp.maximum(i - npt, 0), 0))

    widths = (A_WIDTH, B_WIDTH, C_D_INNER)
    return pl.pallas_call(
        functools.partial(_outproj_kernel, prompt_tiles=npt),
        grid=(t // tm,),
        in_specs=[tok(d)] + [ptok(wd) for wd in widths] + [stok(wd) for wd in widths] + [
            pl.BlockSpec((A_WIDTH, d), lambda i: (0, 0)),
            pl.BlockSpec((B_WIDTH, d), lambda i: (1, 0)),
            pl.BlockSpec((C_D_INNER, d), lambda i: (1, 0))],
        out_specs=tok(d),
        out_shape=jax.ShapeDtypeStruct((t, d), F32),
        compiler_params=_params(("parallel",)),
        name="out_proj",
    )(x, *y_prompt, *y_sample, w, w, w)


def _ffn_kernel(x_ref, g_ref, wg_ref, wu_ref, wd_ref, o_ref, h_ref):
    @pl.when(pl.program_id(1) == 0)
    def _():
        x = x_ref[...]
        ms = jnp.mean(x * x, axis=-1, keepdims=True)
        h_ref[...] = (x * lax.rsqrt(ms + EPS) * g_ref[...]).astype(BF16)
        o_ref[...] = x

    h = h_ref[...]
    act = _silu(_dot(h, wg_ref[...])) * _dot(h, wu_ref[...])
    o_ref[...] += _dot(act.astype(BF16), wd_ref[...])


def _ffn(x, g, wg, wu, wd):
    t, d = x.shape
    f = wg.shape[1]
    tm = _tile(t, 1024)
    tf = 512
    once = pl.Buffered(1)
    return pl.pallas_call(
        _ffn_kernel,
        grid=(t // tm, f // tf),
        in_specs=[pl.BlockSpec((tm, d), lambda i, j: (i, 0), pipeline_mode=once),
                  pl.BlockSpec((1, d), lambda i, j: (0, 0)),
                  pl.BlockSpec((d, tf), lambda i, j: (0, j)),
                  pl.BlockSpec((d, tf), lambda i, j: (0, j)),
                  pl.BlockSpec((tf, d), lambda i, j: (j, 0))],
        out_specs=pl.BlockSpec((tm, d), lambda i, j: (i, 0), pipeline_mode=once),
        out_shape=jax.ShapeDtypeStruct((t, d), F32),
        scratch_shapes=[pltpu.VMEM((tm, d), BF16)],
        compiler_params=_params(("parallel", "arbitrary")),
        name="swiglu",
    )(x, g, wg, wu, wd)


def _rope_rows(pos):
    half = ROPE_DIM // 2
    inv = 1.0 / (ROPE_THETA ** (jnp.arange(0, ROPE_DIM, 2, dtype=F32) / ROPE_DIM))
    ang = pos.astype(F32)[:, None] * inv[None, :]
    cos, sin = jnp.cos(ang), jnp.sin(ang)
    n = pos.shape[0]
    one = jnp.ones((n, HEAD_DIM - ROPE_DIM), F32)
    zero = jnp.zeros((n, HEAD_DIM - ROPE_DIM), F32)
    zh = jnp.zeros((n, half), F32)
    rc = jnp.concatenate([cos, cos, one], axis=1)
    rs1 = jnp.concatenate([zh, sin, zero], axis=1)
    rs2 = jnp.concatenate([-sin, zh, zero], axis=1)
    return tuple(jnp.tile(a, (1, LANES // HEAD_DIM)) for a in (rc, rs1, rs2))


def _w_in_segments():
    offs = [0]
    for n in IN_SIZES:
        offs.append(offs[-1] + n)
    dst = (COL_AQ, COL_AK, COL_AV, COL_IQ, COL_MISC + MISC_IK, COL_MISC + MISC_IW, COL_BQ, COL_BK, COL_BV,
           COL_Z, COL_XBC, COL_MISC + MISC_DT)
    return [(offs[i], dst[i], IN_SIZES[i]) for i in range(len(IN_SIZES))]


def _regroup_kernel(w_ref, o_ref):
    for src, dst, width in _w_in_segments():
        o_ref[:, dst:dst + width] = w_ref[0, :, src:src + width].astype(BF16)
    used = COL_MISC + MISC_DT + C_HEADS
    o_ref[:, used:] = jnp.zeros((o_ref.shape[0], P_COLS - used), BF16)


def _rearrange_w_in(w, layer):
    _, d, n = w.shape
    tr = _tile(d, 256)
    return pl.pallas_call(
        _regroup_kernel,
        grid=(d // tr,),
        in_specs=[pl.BlockSpec((1, tr, n), lambda i: (layer, i, 0))],
        out_specs=pl.BlockSpec((tr, P_COLS), lambda i: (i, 0)),
        out_shape=jax.ShapeDtypeStruct((d, P_COLS), BF16),
        compiler_params=_params(("parallel",)),
        name="regroup_w_in",
    )(w)


def _lane_row(v, off):
    return jnp.zeros((1, LANES), F32).at[0, off:off + v.shape[0]].set(v)


def _gain_row(g):
    return jnp.tile(g, LANES // HEAD_DIM)[None, :]


def kernel(x_prompt, x_sample, cache_a_k, cache_a_v, cache_a_idx_k, cache_b_k, cache_b_v, state_c_ssm, state_c_conv, norm1_g, w_in, a_q_norm_g, a_k_norm_g, idx_k_norm_g, b_q_norm_g, b_k_norm_g, c_conv_w, c_conv_b, c_dt_bias, c_a_log, c_d, c_norm_g, w_out, norm2_g, w_gate, w_up, w_down):
    bp, sp, d = x_prompt.shape
    bs, ss, _ = x_sample.shape
    depth = w_in.shape[0]
    past = cache_a_k.shape[2]
    tp, ts = bp * sp, bs * ss
    tq_p = 128
    cs_p = 128 if sp % 128 == 0 else sp
    assert ss >= CONV_WIDTH - 1 and sp >= CONV_WIDTH - 1

    pos = jnp.concatenate([jnp.tile(jnp.arange(sp, dtype=I32), bp),
                           jnp.tile(past + jnp.arange(ss, dtype=I32), bs)])
    rc, rs1, rs2 = _rope_rows(pos)
    x = jnp.concatenate([x_prompt.reshape(tp, d), x_sample.reshape(ts, d)], axis=0)
    zero_conv = jnp.zeros((bp, CONV_WIDTH - 1, C_CONV_DIM), F32)
    zero_ssm = jnp.zeros((bp, C_HEADS, C_HEAD_DIM, D_STATE), F32)

    p_states, s_states = [], []
    for l in range(depth):
        p = _proj(x, norm1_g[l][None, :], _rearrange_w_in(w_in, l))
        gik = jnp.concatenate([idx_k_norm_g[l], jnp.ones((LANES - IDX_DIM,), F32)])[None, :]
        (aq, ak, iq, ik, iw, bq, bk, bk16, bv16) = _post(
            p, rc, rs1, rs2, _gain_row(a_q_norm_g[l]), _gain_row(a_k_norm_g[l]), gik,
            _gain_row(b_q_norm_g[l]), _gain_row(b_k_norm_g[l]))

        a_caches = (jnp.transpose(cache_a_idx_k, (0, 1, 3, 2)), jnp.transpose(cache_a_k, (0, 1, 3, 4, 2)),
                    jnp.transpose(cache_a_v, (0, 1, 3, 4, 2)))
        ya_p = _dsap_call(aq, iq, iw, ik, ak, p, None, batch=bp, seq_q=sp, tq=tq_p, row0=0)
        ya_s = _dsap_call(aq, iq, iw, ik, ak, p, a_caches, batch=bs, seq_q=ss, tq=ss, row0=tp, layer=l)
        b_caches = (jnp.transpose(cache_b_k, (0, 1, 3, 4, 2)), jnp.transpose(cache_b_v, (0, 1, 3, 4, 2)))
        yb_p = _sb_call(bq, bk16, bv16, None, batch=bp, seq_q=sp, tq=tq_p, row0=0)
        yb_s = _sb_call(bq, bk16, bv16, b_caches, batch=bs, seq_q=ss, tq=ss, row0=tp, layer=l)
        cw, cb = c_conv_w[l], c_conv_b[l][None, :]
        dtb, alog = _lane_row(c_dt_bias[l], MISC_DT), _lane_row(c_a_log[l], MISC_DT)
        dsk, ng = jnp.repeat(c_d[l], C_HEAD_DIM)[None, :], c_norm_g[l][None, :]
        yc_p, ssm_p = _ssd_call(p, zero_conv, zero_ssm, cw, cb, dtb, alog, dsk, ng,
                                batch=bp, seq=sp, cs=cs_p, row0=0)
        yc_s, ssm_s = _ssd_call(p, state_c_conv[l], state_c_ssm[l], cw, cb, dtb, alog, dsk, ng,
                                batch=bs, seq=ss, cs=ss, row0=tp)

        x = _outproj(x, (ya_p, yb_p, yc_p), (ya_s, yb_s, yc_s), w_out[l].astype(BF16))
        x = _ffn(x, norm2_g[l][None, :], w_gate[l].astype(BF16), w_up[l].astype(BF16), w_down[l].astype(BF16))

        def state(lo, hi, b, s, ssm):
            av = p[lo:hi, COL_AV:COL_AV + LANES]
            bv = p[lo:hi, COL_BV:COL_BV + B_WIDTH]
            conv = jnp.stack([p[lo + (i + 1) * s - (CONV_WIDTH - 1):lo + (i + 1) * s, COL_XBC:COL_XBC + C_CONV_DIM]
                              for i in range(b)])
            return (ak[lo:hi].reshape(b, s, A_KV_HEADS, HEAD_DIM), av.reshape(b, s, A_KV_HEADS, HEAD_DIM),
                    ik[lo:hi].reshape(b, s, IDX_DIM), bk[lo:hi].reshape(b, s, B_HEADS, HEAD_DIM),
                    bv.reshape(b, s, B_HEADS, HEAD_DIM), ssm, conv)

        p_states.append(state(0, tp, bp, sp, ssm_p))
        s_states.append(state(tp, tp + ts, bs, ss, ssm_s))

    p_out = [jnp.stack(zs) for zs in zip(*p_states)]
    s_out = [jnp.stack(zs) for zs in zip(*s_states)]
    return (x[:tp].reshape(bp, sp, d), x[tp:].reshape(bs, ss, d), *p_out, *s_out)
```

```python
import functools
import math

import jax
import jax.numpy as jnp
from jax import lax
from jax.experimental import pallas as pl
from jax.experimental.pallas import tpu as pltpu

F32 = jnp.float32
BF16 = jnp.bfloat16
I32 = jnp.int32

D_MODEL = 2048
EPS = 1e-6
CHUNK = 64
HEAD_DIM = 64
ROPE_DIM = HEAD_DIM // 4
ROPE_THETA = 500000.0
A_HEADS = 8
A_KV_HEADS = 2
A_WIDTH = A_HEADS * HEAD_DIM
IDX_HEADS = 16
IDX_DIM = 64
TOPK_MAX = 256
B_HEADS = 8
B_WIDTH = B_HEADS * HEAD_DIM
C_D_INNER = D_MODEL // 2
C_HEAD_DIM = 64
C_HEADS = C_D_INNER // C_HEAD_DIM
C_GROUPS = 4
D_STATE = 128
CONV_WIDTH = 4
C_CONV_DIM = C_D_INNER + 2 * C_GROUPS * D_STATE
IN_SIZES = (A_WIDTH, A_KV_HEADS * HEAD_DIM, A_KV_HEADS * HEAD_DIM, IDX_HEADS * IDX_DIM, IDX_DIM, IDX_HEADS,
            B_WIDTH, B_WIDTH, B_WIDTH, C_D_INNER, C_CONV_DIM, C_HEADS)

LANES = 128
VMEM_LIMIT = 48 * 1024 * 1024

COL_XBC, COL_IQ, COL_Z, COL_AQ, COL_BQ, COL_BK, COL_BV, COL_AK, COL_AV, COL_MISC = (
    0, 2048, 3072, 4096, 4608, 5120, 5632, 6144, 6272, 6400)
P_COLS = 6656
MISC_IK, MISC_IW, MISC_DT = 0, 64, 80

NEG_BIG = -1e30
INT_MIN = -2 ** 31
KEY_POS_INF = 0x7F800000
KEY_NEG_INF = INT_MIN + 0x7FFFFF
SB_DEAD = -104.0


def _params(sem):
    return pltpu.CompilerParams(dimension_semantics=sem, vmem_limit_bytes=VMEM_LIMIT)


def _tile(n, pref):
    t = min(n, pref)
    while n % t:
        t -= 8
    return t


def _dot_nt(a, b):
    return lax.dot_general(a, b, (((1,), (1,)), ((), ())), preferred_element_type=F32)


def _dot_tn(a, b):
    return lax.dot_general(a, b, (((0,), (0,)), ((), ())), preferred_element_type=F32)


def _dot(a, b):
    return jnp.dot(a, b, preferred_element_type=F32)


def _split3(x):
    h1 = x.astype(BF16)
    r1 = x - h1.astype(F32)
    h2 = r1.astype(BF16)
    h3 = (r1 - h2.astype(F32)).astype(BF16)
    return h1, h2, h3


def _softplus(x):
    return jnp.maximum(x, 0.0) + jnp.log1p(jnp.exp(-jnp.abs(x)))


def _silu(x):
    return x * (1.0 / (1.0 + jnp.exp(-x)))


def _proj_kernel(x_ref, g_ref, w_ref, o_ref, h_ref):
    @pl.when(pl.program_id(1) == 0)
    def _():
        x = x_ref[...]
        ms = jnp.mean(x * x, axis=-1, keepdims=True)
        h_ref[...] = (x * lax.rsqrt(ms + EPS) * g_ref[...]).astype(BF16)

    o_ref[...] = _dot(h_ref[...], w_ref[...])


def _proj(x, g, w):
    t, d = x.shape
    n = w.shape[1]
    tm = _tile(t, 1024)
    tn = 512
    return pl.pallas_call(
        _proj_kernel,
        grid=(t // tm, n // tn),
        in_specs=[pl.BlockSpec((tm, d), lambda i, j: (i, 0)),
                  pl.BlockSpec((1, d), lambda i, j: (0, 0)),
                  pl.BlockSpec((d, tn), lambda i, j: (0, j))],
        out_specs=pl.BlockSpec((tm, tn), lambda i, j: (i, j)),
        out_shape=jax.ShapeDtypeStruct((t, n), F32),
        scratch_shapes=[pltpu.VMEM((tm, d), BF16)],
        compiler_params=_params(("parallel", "arbitrary")),
        name="in_proj",
    )(x, g, w)


def _post_kernel(aq_r, bq_r, bk_r, bv_r, iq_r, ak_r, misc_r, rc_r, rs1_r, rs2_r,
                 gaq_r, gak_r, gik_r, gbq_r, gbk_r,
                 aq_o, ak_o, iq_o, ik_o, iw_o, bq_o, bk_o, bk16_o, bv16_o):
    r = lax.broadcasted_iota(I32, (LANES, LANES), 0) // HEAD_DIM
    c = lax.broadcasted_iota(I32, (LANES, LANES), 1) // HEAD_DIM
    segm = jnp.where(r == c, 1.0, 0.0).astype(BF16)
    rc, rs1, rs2 = rc_r[...], rs1_r[...], rs2_r[...]

    def headnorm(xc, g_row):
        h1, h2, h3 = _split3(xc * xc)
        ss = _dot(h1, segm) + _dot(h2, segm) + _dot(h3, segm)
        return xc * lax.rsqrt(ss * (1.0 / HEAD_DIM) + EPS) * g_row

    def rope(xc):
        return xc * rc + pltpu.roll(xc, 8, 1) * rs1 + pltpu.roll(xc, LANES - 8, 1) * rs2

    for k in range(A_WIDTH // LANES):
        sl = slice(k * LANES, (k + 1) * LANES)
        aq_o[:, sl] = (rope(headnorm(aq_r[:, sl], gaq_r[...])) * (HEAD_DIM ** -0.5)).astype(BF16)
        bq_o[:, sl] = (headnorm(bq_r[:, sl], gbq_r[...]) * (HEAD_DIM ** -0.5)).astype(BF16)
        bk = headnorm(bk_r[:, sl], gbk_r[...])
        bk_o[:, sl] = bk
        bk16_o[:, sl] = bk.astype(BF16)
        bv16_o[:, sl] = bv_r[:, sl].astype(BF16)
    for k in range(IDX_HEADS * IDX_DIM // LANES):
        sl = slice(k * LANES, (k + 1) * LANES)
        iq_o[:, sl] = rope(iq_r[:, sl]).astype(BF16)
    ak_o[...] = rope(headnorm(ak_r[...], gak_r[...]))
    misc = misc_r[...]
    ik_o[...] = rope(headnorm(misc, gik_r[...]))[:, MISC_IK:MISC_IK + IDX_DIM]
    iw_o[...] = misc[:, MISC_IW:MISC_IW + IDX_HEADS] * (IDX_HEADS ** -0.5 * IDX_DIM ** -0.5)


def _post(p, rc, rs1, rs2, gaq, gak, gik, gbq, gbk):
    t = p.shape[0]
    tm = _tile(t, 256)

    def col(width, off):
        return pl.BlockSpec((tm, width), lambda i: (i, off // width))

    def row(width):
        return pl.BlockSpec((1, width), lambda i: (0, 0))

    def out(width):
        return pl.BlockSpec((tm, width), lambda i: (i, 0))

    widths_dtypes = [(A_WIDTH, BF16), (LANES, F32), (IDX_HEADS * IDX_DIM, BF16), (IDX_DIM, F32),
                     (IDX_HEADS, F32), (B_WIDTH, BF16), (B_WIDTH, F32), (B_WIDTH, BF16), (B_WIDTH, BF16)]
    return pl.pallas_call(
        _post_kernel,
        grid=(t // tm,),
        in_specs=[col(A_WIDTH, COL_AQ), col(B_WIDTH, COL_BQ), col(B_WIDTH, COL_BK), col(B_WIDTH, COL_BV),
                  col(IDX_HEADS * IDX_DIM, COL_IQ), col(LANES, COL_AK), col(LANES, COL_MISC),
                  out(LANES), out(LANES), out(LANES),
                  row(LANES), row(LANES), row(LANES), row(LANES), row(LANES)],
        out_specs=[out(w) for w, _ in widths_dtypes],
        out_shape=[jax.ShapeDtypeStruct((t, w), dt) for w, dt in widths_dtypes],
        compiler_params=_params(("parallel",)),
        name="head_post",
    )(p, p, p, p, p, p, p, rc, rs1, rs2, gaq, gak, gik, gbq, gbk)


CACHE_CHUNK = 4096
NEW_CHUNK = 512


def _dsap_kernel(*refs, tq, sn, lc, topk, kblk, has_cache, layer, cache_chunk):
    nb = LANES // tq
    if has_cache:
        (aq_ref, iq_ref, iw_ref, ikn_ref, akn_ref, avn_ref, ikc_hbm, akc_hbm, avc_hbm, o_ref,
         key_ref, bias_ref, ik16_ref, ak16_ref, vt_ref, iqh_ref, aqh_ref, m_ref, acc_ref,
         ikbuf, akbuf, avbuf, sem) = refs
    else:
        (aq_ref, iq_ref, iw_ref, ikn_ref, akn_ref, avn_ref, o_ref,
         key_ref, bias_ref, ik16_ref, ak16_ref, vt_ref, iqh_ref, aqh_ref, m_ref, acc_ref) = refs
    grp = pl.program_id(0)
    qi = pl.program_id(1)
    n_keys = lc + sn
    lp = ik16_ref.shape[0]
    rep = A_HEADS // A_KV_HEADS
    ch = cache_chunk
    chn = NEW_CHUNK

    @pl.when(qi == 0)
    def _():
        def put(row, j, ik, k0, k1, v_t):
            n = ik.shape[0]
            cols = slice(j * IDX_DIM, (j + 1) * IDX_DIM)
            ik16_ref[pl.ds(row, n), cols] = ik.astype(BF16)
            ak16_ref[0, pl.ds(row, n), cols] = k0.astype(BF16)
            ak16_ref[1, pl.ds(row, n), cols] = k1.astype(BF16)
            vt = v_t.astype(BF16)
            for e in range(n // kblk):
                for g in range(A_KV_HEADS):
                    vt_ref[g, row // kblk + e, 0:HEAD_DIM, j * kblk:(j + 1) * kblk] = (
                        vt[g * HEAD_DIM:(g + 1) * HEAD_DIM, e * kblk:(e + 1) * kblk])

        if has_cache:
            nc = lc // ch

            def copies(j, c, slot):
                b = grp * nb + j
                src = pl.ds(pl.multiple_of(c * ch, ch), ch)
                return (pltpu.make_async_copy(ikc_hbm.at[layer, b, :, src], ikbuf.at[slot], sem.at[0, slot]),
                        pltpu.make_async_copy(akc_hbm.at[layer, b, :, :, src], akbuf.at[slot], sem.at[1, slot]),
                        pltpu.make_async_copy(avc_hbm.at[layer, b, :, :, src], avbuf.at[slot], sem.at[2, slot]))

            def start(j, c, slot):
                for cp in copies(j, c, slot):
                    cp.start()

            start(0, 0, 0)
            for j in range(nb):
                def chunk_body(c, carry, j=j):
                    slot = (j * nc + c) % 2
                    for cp in copies(j, c, slot):
                        cp.wait()

                    @pl.when(c + 1 < nc)
                    def _():
                        start(j, c + 1, 1 - slot)

                    if j + 1 < nb:
                        @pl.when(c + 1 == nc)
                        def _():
                            start(j + 1, 0, ((j + 1) * nc) % 2)

                    kk = akbuf[slot].reshape(LANES, ch).T
                    ik = jnp.concatenate([ikbuf[slot], jnp.zeros((LANES - IDX_DIM, ch), F32)], axis=0).T
                    put(pl.multiple_of(c * ch, ch), j, ik[:, 0:IDX_DIM], kk[:, 0:HEAD_DIM], kk[:, HEAD_DIM:LANES],
                        avbuf[slot].reshape(LANES, ch))
                    return carry

                lax.fori_loop(0, nc, chunk_body, 0)

        for j in range(nb):
            if sn % chn == 0:
                def new_body(c, carry, j=j):
                    src = pl.ds(pl.multiple_of(j * sn + c * chn, chn), chn)
                    ak = akn_ref[src, :]
                    put(pl.multiple_of(lc + c * chn, chn), j, ikn_ref[src, :], ak[:, 0:HEAD_DIM], ak[:, HEAD_DIM:LANES],
                        avn_ref[src, :].T)
                    return carry

                lax.fori_loop(0, sn // chn, new_body, 0)
            else:
                def padded(ref):
                    return jnp.concatenate([ref[j * sn:(j + 1) * sn, :],
                                            jnp.zeros((kblk - sn, ref.shape[-1]), F32)], axis=0)

                ak = padded(akn_ref)
                put(lc, j, padded(ikn_ref), ak[:, 0:HEAD_DIM], ak[:, HEAD_DIM:LANES], padded(avn_ref).T)
        for g in range(A_KV_HEADS):
            vt_ref[g, :, HEAD_DIM:LANES, :] = jnp.ones((lp // kblk, LANES - HEAD_DIM, nb * kblk), BF16)

    qstart = lc + qi * tq
    kmax = jnp.minimum(n_keys, ((qstart + tq - 1) // CHUNK + 1) * CHUNK)
    nkb = (kmax + kblk - 1) // kblk
    lane = lax.broadcasted_iota(I32, (1, LANES), 1)
    qpos = qstart + lane % tq
    krow = lax.broadcasted_iota(I32, (kblk, 1), 0)
    lane_group = lane // tq
    two = 2 * LANES

    def head_major(dst, src, nheads):
        x_t = src[...].astype(F32).T
        for h in range(nheads):
            xh = x_t[h * HEAD_DIM:(h + 1) * HEAD_DIM, :]
            if nb > 1:
                xh = jnp.concatenate([jnp.where(lane_group == j, xh, 0.0) for j in range(nb)], axis=0)
            dst[:, h * LANES:(h + 1) * LANES] = xh.astype(BF16)

    head_major(iqh_ref, iq_ref, IDX_HEADS)
    head_major(aqh_ref, aq_ref, A_HEADS)
    iw_t = jnp.concatenate([iw_ref[...], jnp.zeros((LANES, LANES - IDX_HEADS), F32)], axis=1).T

    def score_body(kb, carry):
        start = pl.multiple_of(kb * kblk, kblk)
        ikb = ik16_ref[pl.ds(start, kblk), :]
        acc = jnp.zeros((kblk, LANES), F32)
        for pr in range(IDX_HEADS // 2):
            logit = _dot(ikb, iqh_ref[:, pr * two:(pr + 1) * two])
            for e in range(2):
                h = 2 * pr + e
                acc = acc + iw_t[h:h + 1, :] * jnp.maximum(logit[:, e * LANES:(e + 1) * LANES], 0.0)
        kpos = start + krow
        adm = jnp.logical_and(kpos // CHUNK <= qpos // CHUNK, kpos < n_keys)
        bits = pltpu.bitcast(jnp.where(adm, acc, -jnp.inf), I32)
        bits = jnp.where(bits == INT_MIN, 0, bits)
        key_ref[kb] = bits ^ ((bits >> 31) & 0x7FFFFFFF)
        return carry

    lax.fori_loop(0, nkb, score_body, 0)

    def count(pred):
        def body(kb, acc):
            hit = jnp.where(pred(key_ref[kb], kb), 1.0, 0.0)
            return acc + jnp.sum(hit.reshape(kblk // 64, 64, LANES), axis=0)
        acc = lax.fori_loop(0, nkb, body, jnp.zeros((64, LANES), F32))
        return jnp.sum(acc, axis=0, keepdims=True)

    kf = float(topk)

    def bit_body(i, thr):
        cand = thr + (jnp.int32(1) << (31 - i))
        cnt = count(lambda key, kb: key >= cand)
        return jnp.where(cnt >= kf, cand, thr)

    thr = lax.fori_loop(0, 32, bit_body, jnp.full((1, LANES), INT_MIN, I32))
    n_gt = count(lambda key, kb: key > thr)
    n_ge = count(lambda key, kb: key >= thr)
    need = kf - n_gt

    def tie_search():
        def jbody(i, jl):
            cand = jl + (jnp.int32(1) << (14 - i))
            cnt = count(lambda key, kb: jnp.logical_and(key == thr, kb * kblk + krow < cand))
            return jnp.where(cnt <= need, cand, jl)
        return lax.fori_loop(0, 15, jbody, jnp.zeros((1, LANES), I32))

    jlim = lax.cond(jnp.max(n_ge) > kf, tie_search, lambda: jnp.full((1, LANES), 2 ** 30, I32))

    def bias_body(kb, carry):
        key = key_ref[kb]
        kpos = kb * kblk + krow
        sel = jnp.logical_or(key > thr, jnp.logical_and(key == thr, kpos < jlim))
        sel = jnp.logical_and(sel, jnp.logical_and(key > KEY_NEG_INF, key < KEY_POS_INF))
        bias_ref[kb] = jnp.where(sel, 0.0, NEG_BIG)
        return carry

    lax.fori_loop(0, nkb, bias_body, 0)

    m_ref[...] = jnp.full(m_ref.shape, -3e38, F32)
    acc_ref[...] = jnp.zeros(acc_ref.shape, F32)
    pair_group = (lax.broadcasted_iota(I32, (1, 2 * LANES), 1) % LANES) // tq

    def att_body(kb, carry):
        start = pl.multiple_of(kb * kblk, kblk)
        bias = bias_ref[kb]
        m_all = m_ref[...]
        kks = [ak16_ref[g, pl.ds(start, kblk), :] for g in range(A_KV_HEADS)]
        s_ts = [_dot(kks[h0 // rep], aqh_ref[:, h0 * LANES:(h0 + 2) * LANES]) for h0 in range(0, A_HEADS, 2)]
        ps, alphas, m_rows = [], [], []
        for h in range(A_HEADS):
            x = s_ts[h // 2][:, (h % 2) * LANES:(h % 2 + 1) * LANES] + bias
            part = jnp.max(x.reshape(kblk // 64, 64, LANES), axis=0)
            bmax = jnp.max(part, axis=0, keepdims=True)
            m_old = m_all[h:h + 1, :]
            m_new = jnp.maximum(m_old, bmax)
            alphas.append(jnp.exp(m_old - m_new))
            ps.append(jnp.exp(x - m_new).astype(BF16))
            m_rows.append(m_new)
        m_ref[...] = jnp.concatenate(m_rows, axis=0)
        upds = []
        for h0 in range(0, A_HEADS, 2):
            pp = jnp.concatenate(ps[h0:h0 + 2], axis=1)
            if nb > 1:
                pp = jnp.concatenate([jnp.where(pair_group == j, pp, jnp.zeros_like(pp)) for j in range(nb)], axis=0)
            upds.append(_dot(vt_ref[h0 // rep, kb], pp))
        for h in range(A_HEADS):
            acc_ref[h] = acc_ref[h] * alphas[h] + upds[h // 2][:, (h % 2) * LANES:(h % 2 + 1) * LANES]
        return carry

    lax.fori_loop(0, nkb, att_body, 0)

    outs = []
    for h in range(A_HEADS):
        a = acc_ref[h]
        outs.append(a[0:HEAD_DIM, :] / a[HEAD_DIM:LANES, :])
    o_ref[...] = jnp.concatenate(outs, axis=0).T.astype(o_ref.dtype)


def _dsap_call(aq, iq, iw, ik, ak, p, caches, *, batch, seq_q, tq, row0, layer=0, kblk=256):
    has_cache = caches is not None
    lc = caches[0].shape[-1] if has_cache else 0
    sn = seq_q
    nb = LANES // tq
    n_keys = lc + sn
    lp = -(-n_keys // kblk) * kblk
    assert LANES % tq == 0 and batch % nb == 0 and sn % tq == 0 and row0 % (nb * sn) == 0 and row0 % LANES == 0
    cache_chunk = min(CACHE_CHUNK, lc) if has_cache else 0
    assert not has_cache or (lc % cache_chunk == 0 and cache_chunk % NEW_CHUNK == 0)
    assert NEW_CHUNK % kblk == 0 and ((sn % NEW_CHUNK == 0 and lp == n_keys) or (sn < kblk and lp == lc + kblk))
    groups = batch // nb
    nqb = seq_q // tq
    assert nb == 1 or nqb == 1
    rb0 = row0 // LANES
    nb0 = row0 // (nb * sn)
    topk = min(TOPK_MAX, n_keys // 4)
    nkb_max = lp // kblk

    def qspec(width):
        return pl.BlockSpec((LANES, width), lambda g, i: (rb0 + g * nqb + i, 0))

    def nspec(width, colblk=0):
        return pl.BlockSpec((nb * sn, width), lambda g, i: (nb0 + g, colblk))

    in_specs = [qspec(A_WIDTH), qspec(IDX_HEADS * IDX_DIM), qspec(IDX_HEADS),
                nspec(IDX_DIM), nspec(LANES), nspec(LANES, COL_AV // LANES)]
    args = [aq, iq, iw, ik, ak, p]
    scratch = [pltpu.VMEM((nkb_max, kblk, LANES), I32),
               pltpu.VMEM((nkb_max, kblk, LANES), F32),
               pltpu.VMEM((lp, nb * IDX_DIM), BF16),
               pltpu.VMEM((A_KV_HEADS, lp, nb * HEAD_DIM), BF16),
               pltpu.VMEM((A_KV_HEADS, nkb_max, LANES, nb * kblk), BF16),
               pltpu.VMEM((nb * IDX_DIM, IDX_HEADS * LANES), BF16),
               pltpu.VMEM((nb * HEAD_DIM, A_HEADS * LANES), BF16),
               pltpu.VMEM((A_HEADS, LANES), F32),
               pltpu.VMEM((A_HEADS, LANES, LANES), F32)]
    if has_cache:
        in_specs += [pl.BlockSpec(memory_space=pl.ANY)] * 3
        args += list(caches)
        scratch += [pltpu.VMEM((2, IDX_DIM, cache_chunk), F32),
                    pltpu.VMEM((2, A_KV_HEADS, HEAD_DIM, cache_chunk), F32),
                    pltpu.VMEM((2, A_KV_HEADS, HEAD_DIM, cache_chunk), F32),
                    pltpu.SemaphoreType.DMA((3, 2))]
    kern = functools.partial(_dsap_kernel, tq=tq, sn=sn, lc=lc, topk=topk, kblk=kblk, has_cache=has_cache,
                             layer=layer, cache_chunk=cache_chunk)
    return pl.pallas_call(
        kern,
        grid=(groups, nqb),
        in_specs=in_specs,
        out_specs=pl.BlockSpec((LANES, A_WIDTH), lambda g, i: (g * nqb + i, 0)),
        out_shape=jax.ShapeDtypeStruct((batch * seq_q, A_WIDTH), BF16),
        scratch_shapes=scratch,
        compiler_params=_params(("parallel", "arbitrary")),
        name="dsa_attention",
    )(*args)


def _sb_kernel(*refs, tq, sn, lc, kblk, has_cache, layer):
    if has_cache:
        q_ref, kn_ref, vn_ref, kc_hbm, vc_hbm, o_ref, qh_ref, carry_ref, acc_ref, kbuf, vbuf, sem = refs
    else:
        q_ref, kn_ref, vn_ref, o_ref, qh_ref, carry_ref, acc_ref = refs
    qlocal = pl.program_id(1) * tq
    qpos = qlocal + lax.broadcasted_iota(I32, (tq, 1), 0)
    kiota = lax.broadcasted_iota(I32, (1, kblk), 1)
    r = lax.broadcasted_iota(I32, (kblk, kblk), 0)
    c = lax.broadcasted_iota(I32, (kblk, kblk), 1)
    upper = jnp.where(r > c, 1.0, 0.0).astype(BF16)
    heads = range(B_HEADS)
    for h in heads:
        qh_ref[h] = q_ref[:, h * HEAD_DIM:(h + 1) * HEAD_DIM]
    carry_ref[...] = jnp.zeros(carry_ref.shape, F32)
    acc_ref[...] = jnp.zeros(acc_ref.shape, F32)

    def visit(score, apply, causal):
        zs = [score(h) for h in heads]
        sps = [_softplus(z) for z in zs]
        lks = [-sp if causal is None else jnp.where(causal, -sp, 0.0) for sp in sps]
        carries = [carry_ref[h] for h in heads]
        laters = []
        for h in heads:
            l1, l2, l3 = _split3(lks[h])
            laters.append(carries[h] + (_dot(l1, upper) + _dot(l2, upper) + _dot(l3, upper)))
        weights = [jnp.exp((zs[h] - sps[h]) + laters[h]) for h in heads]
        if causal is not None:
            weights = [jnp.where(causal, a, 0.0) for a in weights]
        upds = [apply(h, weights[h].astype(BF16)) for h in heads]
        new = [carries[h] + jnp.sum(lks[h], axis=1, keepdims=True) for h in heads]
        top = new[0]
        for h in heads:
            acc_ref[h] += upds[h]
            carry_ref[h] = new[h]
            top = jnp.maximum(top, new[h])
        return jnp.max(top) > SB_DEAD

    def alive_cond(state):
        j, alive = state
        return jnp.logical_and(j >= 0, alive)

    def hcols(h):
        return slice(h * HEAD_DIM, (h + 1) * HEAD_DIM)

    if sn >= kblk:
        def new_body(state):
            j, _ = state
            start = pl.multiple_of(j * kblk, kblk)
            alive = visit(lambda h: _dot_nt(qh_ref[h], kn_ref[pl.ds(start, kblk), hcols(h)]),
                          lambda h, a: _dot(a, vn_ref[pl.ds(start, kblk), hcols(h)]), start + kiota < qpos)
            return j - 1, alive

        _, alive = lax.while_loop(alive_cond, new_body, ((qlocal + tq - 1) // kblk, True))
    else:
        def padded(ref, h):
            return jnp.concatenate([ref[:, hcols(h)], jnp.zeros((kblk - sn, HEAD_DIM), BF16)], axis=0)

        alive = visit(lambda h: _dot_nt(qh_ref[h], padded(kn_ref, h)), lambda h, a: _dot(a, padded(vn_ref, h)),
                      kiota < qpos)

    if has_cache:
        b = pl.program_id(0)

        def copies(j, slot):
            src = pl.ds(pl.multiple_of(j * kblk, kblk), kblk)
            return (pltpu.make_async_copy(kc_hbm.at[layer, b, :, :, src], kbuf.at[slot], sem.at[0, slot]),
                    pltpu.make_async_copy(vc_hbm.at[layer, b, :, :, src], vbuf.at[slot], sem.at[1, slot]))

        def start(j, slot):
            for cp in copies(j, slot):
                cp.start()

        def wait(j, slot):
            for cp in copies(j, slot):
                cp.wait()

        j0 = lc // kblk - 1

        @pl.when(alive)
        def _():
            start(j0, j0 % 2)

        def cache_body(state):
            j, _ = state
            slot = j % 2
            wait(j, slot)

            @pl.when(j > 0)
            def _():
                start(j - 1, 1 - slot)

            alive = visit(lambda h: _dot(qh_ref[h], kbuf[slot, h].astype(BF16)),
                          lambda h, a: _dot_nt(a, vbuf[slot, h].astype(BF16)), None)
            return j - 1, alive

        j_end, _ = lax.while_loop(alive_cond, cache_body, (j0, alive))

        @pl.when(jnp.logical_and(alive, j_end >= 0))
        def _():
            wait(j_end, j_end % 2)

    for h in heads:
        o_ref[:, hcols(h)] = acc_ref[h].astype(o_ref.dtype)


def _sb_call(bq, bk16, bv16, caches, *, batch, seq_q, tq, row0, layer=0, kblk=128):
    has_cache = caches is not None
    lc = caches[0].shape[-1] if has_cache else 0
    sn = seq_q
    assert kblk == LANES and lc % kblk == 0 and row0 % sn == 0 and sn % tq == 0
    assert sn % kblk == 0 or (sn < kblk and tq == sn)
    nqb = seq_q // tq
    rb0 = row0 // tq
    nb0 = row0 // sn
    in_specs = [pl.BlockSpec((tq, B_WIDTH), lambda b, i: (rb0 + b * nqb + i, 0)),
                pl.BlockSpec((sn, B_WIDTH), lambda b, i: (nb0 + b, 0)),
                pl.BlockSpec((sn, B_WIDTH), lambda b, i: (nb0 + b, 0))]
    args = [bq, bk16, bv16]
    scratch = [pltpu.VMEM((B_HEADS, tq, HEAD_DIM), BF16), pltpu.VMEM((B_HEADS, tq, LANES), F32),
               pltpu.VMEM((B_HEADS, tq, HEAD_DIM), F32)]
    if has_cache:
        in_specs += [pl.BlockSpec(memory_space=pl.ANY)] * 2
        args += list(caches)
        scratch += [pltpu.VMEM((2, B_HEADS, HEAD_DIM, kblk), F32), pltpu.VMEM((2, B_HEADS, HEAD_DIM, kblk), F32),
                    pltpu.SemaphoreType.DMA((2, 2))]
    kern = functools.partial(_sb_kernel, tq=tq, sn=sn, lc=lc, kblk=kblk, has_cache=has_cache, layer=layer)
    return pl.pallas_call(
        kern,
        grid=(batch, nqb),
        in_specs=in_specs,
        out_specs=pl.BlockSpec((tq, B_WIDTH), lambda b, i: (b * nqb + i, 0)),
        out_shape=jax.ShapeDtypeStruct((batch * seq_q, B_WIDTH), BF16),
        scratch_shapes=scratch,
        compiler_params=_params(("parallel", "arbitrary")),
        name="stick_breaking",
    )(*args)


def _ssd_kernel(z_ref, xbc_ref, misc_ref, cprev_ref, sprev_ref, cw_ref, cb_ref, dtb_ref, alog_ref, dsk_ref,
                ng_ref, y_ref, snew_ref, xpad_ref, st_ref, *, cs):
    ci = pl.program_id(1)
    nci = pl.num_programs(1)
    hpg = C_HEADS // C_GROUPS
    gw = hpg * C_HEAD_DIM

    @pl.when(ci == 0)
    def _():
        xpad_ref[5:8, :] = cprev_ref[0]
        for g in range(C_GROUPS):
            st_ref[g] = sprev_ref[0, g * hpg:(g + 1) * hpg].reshape(gw, D_STATE).T

    @pl.when(ci > 0)
    def _():
        xpad_ref[5:8, :] = xpad_ref[cs + 5:cs + 8, :]

    xpad_ref[8:8 + cs, :] = xbc_ref[...]
    xc = cb_ref[...] + xpad_ref[5:5 + cs, :] * cw_ref[0:1, :]
    for w in range(1, CONV_WIDTH):
        xc = xc + xpad_ref[5 + w:5 + w + cs, :] * cw_ref[w:w + 1, :]
    xc = _silu(xc)

    dt = _softplus(misc_ref[...] + dtb_ref[...])
    a = dt * (-jnp.exp(alog_ref[...]))
    r = lax.broadcasted_iota(I32, (cs, cs), 0)
    c = lax.broadcasted_iota(I32, (cs, cs), 1)
    tril = r >= c
    tril16 = jnp.where(tril, 1.0, 0.0).astype(BF16)
    a1, a2, a3 = _split3(a)
    a_cum = _dot(tril16, a1) + _dot(tril16, a2) + _dot(tril16, a3)
    pad = LANES - cs
    a_sq = a_cum if pad == 0 else jnp.concatenate([a_cum, jnp.zeros((pad, LANES), F32)], axis=0)
    a_cum_t = a_sq.T
    a_last = a_cum[cs - 1:cs, :]
    exp_a = jnp.exp(a_cum)
    dte = jnp.exp(a_last - a_cum)
    exp_last = jnp.exp(a_last)
    low_half = lax.broadcasted_iota(I32, (1, LANES), 1) < C_HEAD_DIM

    for g in range(C_GROUPS):
        bm = xc[:, C_D_INNER + g * D_STATE:C_D_INNER + (g + 1) * D_STATE].astype(BF16)
        cm = xc[:, C_D_INNER + (C_GROUPS + g) * D_STATE:C_D_INNER + (C_GROUPS + g + 1) * D_STATE].astype(BF16)
        cb = _dot_nt(cm, bm)
        st = st_ref[g]
        y_off = _dot(cm, st.astype(BF16))
        y_pairs, x_pairs, dec_pairs = [], [], []
        for pp in range(hpg // 2):
            h0 = g * hpg + 2 * pp
            ln0, ln1 = MISC_DT + h0, MISC_DT + h0 + 1

            def pat(v, ln0=ln0, ln1=ln1):
                return jnp.where(low_half, v[:, ln0:ln0 + 1], v[:, ln1:ln1 + 1])

            xd = xc[:, h0 * C_HEAD_DIM:(h0 + 2) * C_HEAD_DIM] * pat(dt)
            xd16 = xd.astype(BF16)
            y_pair = y_off[:, pp * LANES:(pp + 1) * LANES] * pat(exp_a)
            for ln, keep in ((ln0, low_half), (ln1, jnp.logical_not(low_half))):
                seg = a_cum[:, ln:ln + 1] - a_cum_t[ln:ln + 1, 0:cs]
                decay = jnp.where(tril, jnp.exp(jnp.where(tril, seg, 0.0)), 0.0)
                y_pair = y_pair + _dot((cb * decay).astype(BF16), jnp.where(keep, xd16, jnp.zeros_like(xd16)))
            y_pairs.append(y_pair)
            x_pairs.append((xd * pat(dte)).astype(BF16))
            dec_pairs.append(pat(exp_last))
        y = jnp.concatenate(y_pairs, axis=1)
        st_new = st * jnp.concatenate(dec_pairs, axis=1) + _dot_tn(bm, jnp.concatenate(x_pairs, axis=1))
        st_ref[g] = st_new

        gs = slice(g * gw, (g + 1) * gw)
        y = y + xc[:, gs] * dsk_ref[:, gs]
        y = y * _silu(z_ref[:, gs])
        y = y * lax.rsqrt(jnp.mean(y * y, axis=1, keepdims=True) + EPS) * ng_ref[:, gs]
        y_ref[:, gs] = y.astype(y_ref.dtype)

        @pl.when(ci == nci - 1)
        def _():
            snew_ref[0, g * hpg:(g + 1) * hpg] = st_new.T.reshape(hpg, C_HEAD_DIM, D_STATE)


def _ssd_call(p, cprev, sprev, cw, cb, dtb, alog, dsk, ng, *, batch, seq, cs, row0):
    nci = seq // cs
    rb0 = row0 // cs

    def pcol(width, off):
        return pl.BlockSpec((cs, width), lambda b, i: (rb0 + b * nci + i, off // width))

    def row(width, rows=1):
        return pl.BlockSpec((rows, width), lambda b, i: (0, 0))

    kern = functools.partial(_ssd_kernel, cs=cs)
    return pl.pallas_call(
        kern,
        grid=(batch, nci),
        in_specs=[pcol(C_D_INNER, COL_Z), pcol(C_CONV_DIM, COL_XBC), pcol(LANES, COL_MISC),
                  pl.BlockSpec((1, CONV_WIDTH - 1, C_CONV_DIM), lambda b, i: (b, 0, 0)),
                  pl.BlockSpec((1, C_HEADS, C_HEAD_DIM, D_STATE), lambda b, i: (b, 0, 0, 0)),
                  row(C_CONV_DIM, CONV_WIDTH), row(C_CONV_DIM), row(LANES), row(LANES),
                  row(C_D_INNER), row(C_D_INNER)],
        out_specs=[pl.BlockSpec((cs, C_D_INNER), lambda b, i: (b * nci + i, 0)),
                   pl.BlockSpec((1, C_HEADS, C_HEAD_DIM, D_STATE), lambda b, i: (b, 0, 0, 0))],
        out_shape=[jax.ShapeDtypeStruct((batch * seq, C_D_INNER), BF16),
                   jax.ShapeDtypeStruct((batch, C_HEADS, C_HEAD_DIM, D_STATE), F32)],
        scratch_shapes=[pltpu.VMEM((cs + 8, C_CONV_DIM), F32),
                        pltpu.VMEM((C_GROUPS, D_STATE, C_HEADS // C_GROUPS * C_HEAD_DIM), F32)],
        compiler_params=_params(("parallel", "arbitrary")),
        name="ssd_mixer",
    )(p, p, p, cprev, sprev, cw, cb, dtb, alog, dsk, ng)


def _outproj_kernel(x_ref, yap_ref, ybp_ref, ycp_ref, yas_ref, ybs_ref, ycs_ref, wa_ref, wb_ref, wc_ref, o_ref, *,
                    prompt_tiles):
    def mix(ya_ref, yb_ref, yc_ref):
        o_ref[...] = (x_ref[...] + _dot(ya_ref[...], wa_ref[...]) + _dot(yb_ref[...], wb_ref[...])
                      + _dot(yc_ref[...], wc_ref[...]))

    is_prompt = pl.program_id(0) < prompt_tiles

    @pl.when(is_prompt)
    def _():
        mix(yap_ref, ybp_ref, ycp_ref)

    @pl.when(jnp.logical_not(is_prompt))
    def _():
        mix(yas_ref, ybs_ref, ycs_ref)


def _outproj(x, y_prompt, y_sample, w):
    t, d = x.shape
    tp, ts = y_prompt[0].shape[0], y_sample[0].shape[0]
    tm = _tile(math.gcd(tp, ts), 256)
    npt = tp // tm

    def tok(width):
        return pl.BlockSpec((tm, width), lambda i: (i, 0))

    def ptok(width):
        return pl.BlockSpec((tm, width), lambda i: (jnp.minimum(i, npt - 1), 0))

    def stok(width):
        return pl.BlockSpec((tm, width), lambda i: (jnp.maximum(i - npt, 0), 0))

    widths = (A_WIDTH, B_WIDTH, C_D_INNER)
    return pl.pallas_call(
        functools.partial(_outproj_kernel, prompt_tiles=npt),
        grid=(t // tm,),
        in_specs=[tok(d)] + [ptok(wd) for wd in widths] + [stok(wd) for wd in widths] + [
            pl.BlockSpec((A_WIDTH, d), lambda i: (0, 0)),
            pl.BlockSpec((B_WIDTH, d), lambda i: (1, 0)),
            pl.BlockSpec((C_D_INNER, d), lambda i: (1, 0))],
        out_specs=tok(d),
        out_shape=jax.ShapeDtypeStruct((t, d), F32),
        compiler_params=_params(("parallel",)),
        name="out_proj",
    )(x, *y_prompt, *y_sample, w, w, w)


def _ffn_kernel(x_ref, g_ref, wg_ref, wu_ref, wd_ref, o_ref, h_ref):
    @pl.when(pl.program_id(1) == 0)
    def _():
        x = x_ref[...]
        ms = jnp.mean(x * x, axis=-1, keepdims=True)
        h_ref[...] = (x * lax.rsqrt(ms + EPS) * g_ref[...]).astype(BF16)
        o_ref[...] = x

    h = h_ref[...]
    act = _silu(_dot(h, wg_ref[...])) * _dot(h, wu_ref[...])
    o_ref[...] += _dot(act.astype(BF16), wd_ref[...])


def _ffn(x, g, wg, wu, wd):
    t, d = x.shape
    f = wg.shape[1]
    tm = _tile(t, 512)
    tf = 512
    return pl.pallas_call(
        _ffn_kernel,
        grid=(t // tm, f // tf),
        in_specs=[pl.BlockSpec((tm, d), lambda i, j: (i, 0)),
                  pl.BlockSpec((1, d), lambda i, j: (0, 0)),
                  pl.BlockSpec((d, tf), lambda i, j: (0, j)),
                  pl.BlockSpec((d, tf), lambda i, j: (0, j)),
                  pl.BlockSpec((tf, d), lambda i, j: (j, 0))],
        out_specs=pl.BlockSpec((tm, d), lambda i, j: (i, 0)),
        out_shape=jax.ShapeDtypeStruct((t, d), F32),
        scratch_shapes=[pltpu.VMEM((tm, d), BF16)],
        compiler_params=_params(("parallel", "arbitrary")),
        name="swiglu",
    )(x, g, wg, wu, wd)


def _rope_rows(pos):
    half = ROPE_DIM // 2
    inv = 1.0 / (ROPE_THETA ** (jnp.arange(0, ROPE_DIM, 2, dtype=F32) / ROPE_DIM))
    ang = pos.astype(F32)[:, None] * inv[None, :]
    cos, sin = jnp.cos(ang), jnp.sin(ang)
    n = pos.shape[0]
    one = jnp.ones((n, HEAD_DIM - ROPE_DIM), F32)
    zero = jnp.zeros((n, HEAD_DIM - ROPE_DIM), F32)
    zh = jnp.zeros((n, half), F32)
    rc = jnp.concatenate([cos, cos, one], axis=1)
    rs1 = jnp.concatenate([zh, sin, zero], axis=1)
    rs2 = jnp.concatenate([-sin, zh, zero], axis=1)
    return tuple(jnp.tile(a, (1, LANES // HEAD_DIM)) for a in (rc, rs1, rs2))


def _w_in_segments():
    offs = [0]
    for n in IN_SIZES:
        offs.append(offs[-1] + n)
    dst = (COL_AQ, COL_AK, COL_AV, COL_IQ, COL_MISC + MISC_IK, COL_MISC + MISC_IW, COL_BQ, COL_BK, COL_BV,
           COL_Z, COL_XBC, COL_MISC + MISC_DT)
    return [(offs[i], dst[i], IN_SIZES[i]) for i in range(len(IN_SIZES))]


def _regroup_kernel(w_ref, o_ref):
    for src, dst, width in _w_in_segments():
        o_ref[:, dst:dst + width] = w_ref[0, :, src:src + width].astype(BF16)
    used = COL_MISC + MISC_DT + C_HEADS
    o_ref[:, used:] = jnp.zeros((o_ref.shape[0], P_COLS - used), BF16)


def _rearrange_w_in(w, layer):
    _, d, n = w.shape
    tr = _tile(d, 256)
    return pl.pallas_call(
        _regroup_kernel,
        grid=(d // tr,),
        in_specs=[pl.BlockSpec((1, tr, n), lambda i: (layer, i, 0))],
        out_specs=pl.BlockSpec((tr, P_COLS), lambda i: (i, 0)),
        out_shape=jax.ShapeDtypeStruct((d, P_COLS), BF16),
        compiler_params=_params(("parallel",)),
        name="regroup_w_in",
    )(w)


def _lane_row(v, off):
    return jnp.zeros((1, LANES), F32).at[0, off:off + v.shape[0]].set(v)


def _gain_row(g):
    return jnp.tile(g, LANES // HEAD_DIM)[None, :]


def kernel(x_prompt, x_sample, cache_a_k, cache_a_v, cache_a_idx_k, cache_b_k, cache_b_v, state_c_ssm, state_c_conv, norm1_g, w_in, a_q_norm_g, a_k_norm_g, idx_k_norm_g, b_q_norm_g, b_k_norm_g, c_conv_w, c_conv_b, c_dt_bias, c_a_log, c_d, c_norm_g, w_out, norm2_g, w_gate, w_up, w_down):
    bp, sp, d = x_prompt.shape
    bs, ss, _ = x_sample.shape
    depth = w_in.shape[0]
    past = cache_a_k.shape[2]
    tp, ts = bp * sp, bs * ss
    tq_p = 128
    cs_p = 128 if sp % 128 == 0 else sp
    assert ss >= CONV_WIDTH - 1 and sp >= CONV_WIDTH - 1

    pos = jnp.concatenate([jnp.tile(jnp.arange(sp, dtype=I32), bp),
                           jnp.tile(past + jnp.arange(ss, dtype=I32), bs)])
    rc, rs1, rs2 = _rope_rows(pos)
    x = jnp.concatenate([x_prompt.reshape(tp, d), x_sample.reshape(ts, d)], axis=0)
    zero_conv = jnp.zeros((bp, CONV_WIDTH - 1, C_CONV_DIM), F32)
    zero_ssm = jnp.zeros((bp, C_HEADS, C_HEAD_DIM, D_STATE), F32)

    p_states, s_states = [], []
    for l in range(depth):
        p = _proj(x, norm1_g[l][None, :], _rearrange_w_in(w_in, l))
        gik = jnp.concatenate([idx_k_norm_g[l], jnp.ones((LANES - IDX_DIM,), F32)])[None, :]
        (aq, ak, iq, ik, iw, bq, bk, bk16, bv16) = _post(
            p, rc, rs1, rs2, _gain_row(a_q_norm_g[l]), _gain_row(a_k_norm_g[l]), gik,
            _gain_row(b_q_norm_g[l]), _gain_row(b_k_norm_g[l]))

        a_caches = (jnp.transpose(cache_a_idx_k, (0, 1, 3, 2)), jnp.transpose(cache_a_k, (0, 1, 3, 4, 2)),
                    jnp.transpose(cache_a_v, (0, 1, 3, 4, 2)))
        ya_p = _dsap_call(aq, iq, iw, ik, ak, p, None, batch=bp, seq_q=sp, tq=tq_p, row0=0)
        ya_s = _dsap_call(aq, iq, iw, ik, ak, p, a_caches, batch=bs, seq_q=ss, tq=ss, row0=tp, layer=l)
        b_caches = (jnp.transpose(cache_b_k, (0, 1, 3, 4, 2)), jnp.transpose(cache_b_v, (0, 1, 3, 4, 2)))
        yb_p = _sb_call(bq, bk16, bv16, None, batch=bp, seq_q=sp, tq=tq_p, row0=0)
        yb_s = _sb_call(bq, bk16, bv16, b_caches, batch=bs, seq_q=ss, tq=ss, row0=tp, layer=l)
        cw, cb = c_conv_w[l], c_conv_b[l][None, :]
        dtb, alog = _lane_row(c_dt_bias[l], MISC_DT), _lane_row(c_a_log[l], MISC_DT)
        dsk, ng = jnp.repeat(c_d[l], C_HEAD_DIM)[None, :], c_norm_g[l][None, :]
        yc_p, ssm_p = _ssd_call(p, zero_conv, zero_ssm, cw, cb, dtb, alog, dsk, ng,
                                batch=bp, seq=sp, cs=cs_p, row0=0)
        yc_s, ssm_s = _ssd_call(p, state_c_conv[l], state_c_ssm[l], cw, cb, dtb, alog, dsk, ng,
                                batch=bs, seq=ss, cs=ss, row0=tp)

        x = _outproj(x, (ya_p, yb_p, yc_p), (ya_s, yb_s, yc_s), w_out[l].astype(BF16))
        x = _ffn(x, norm2_g[l][None, :], w_gate[l].astype(BF16), w_up[l].astype(BF16), w_down[l].astype(BF16))

        def state(lo, hi, b, s, ssm):
            av = p[lo:hi, COL_AV:COL_AV + LANES]
            bv = p[lo:hi, COL_BV:COL_BV + B_WIDTH]
            conv = jnp.stack([p[lo + (i + 1) * s - (CONV_WIDTH - 1):lo + (i + 1) * s, COL_XBC:COL_XBC + C_CONV_DIM]
                              for i in range(b)])
            return (ak[lo:hi].reshape(b, s, A_KV_HEADS, HEAD_DIM), av.reshape(b, s, A_KV_HEADS, HEAD_DIM),
                    ik[lo:hi].reshape(b, s, IDX_DIM), bk[lo:hi].reshape(b, s, B_HEADS, HEAD_DIM),
                    bv.reshape(b, s, B_HEADS, HEAD_DIM), ssm, conv)

        p_states.append(state(0, tp, bp, sp, ssm_p))
        s_states.append(state(tp, tp + ts, bs, ss, ssm_s))

    p_out = [jnp.stack(zs) for zs in zip(*p_states)]
    s_out = [jnp.stack(zs) for zs in zip(*s_states)]
    return (x[:tp].reshape(bp, sp, d), x[tp:].reshape(bs, ss, d), *p_out, *s_out)
```

```python
import functools
import math

import jax
import jax.numpy as jnp
from jax import lax
from jax.experimental import pallas as pl
from jax.experimental.pallas import tpu as pltpu

F32 = jnp.float32
BF16 = jnp.bfloat16
I32 = jnp.int32

D_MODEL = 2048
EPS = 1e-6
CHUNK = 64
HEAD_DIM = 64
ROPE_DIM = HEAD_DIM // 4
ROPE_THETA = 500000.0
A_HEADS = 8
A_KV_HEADS = 2
A_WIDTH = A_HEADS * HEAD_DIM
IDX_HEADS = 16
IDX_DIM = 64
TOPK_MAX = 256
B_HEADS = 8
B_WIDTH = B_HEADS * HEAD_DIM
C_D_INNER = D_MODEL // 2
C_HEAD_DIM = 64
C_HEADS = C_D_INNER // C_HEAD_DIM
C_GROUPS = 4
D_STATE = 128
CONV_WIDTH = 4
C_CONV_DIM = C_D_INNER + 2 * C_GROUPS * D_STATE
IN_SIZES = (A_WIDTH, A_KV_HEADS * HEAD_DIM, A_KV_HEADS * HEAD_DIM, IDX_HEADS * IDX_DIM, IDX_DIM, IDX_HEADS,
            B_WIDTH, B_WIDTH, B_WIDTH, C_D_INNER, C_CONV_DIM, C_HEADS)

LANES = 128
VMEM_LIMIT = 48 * 1024 * 1024

COL_XBC, COL_IQ, COL_Z, COL_AQ, COL_BQ, COL_BK, COL_BV, COL_AK, COL_AV, COL_MISC = (
    0, 2048, 3072, 4096, 4608, 5120, 5632, 6144, 6272, 6400)
P_COLS = 6656
MISC_IK, MISC_IW, MISC_DT = 0, 64, 80

NEG_BIG = -1e30
INT_MIN = -2 ** 31
KEY_POS_INF = 0x7F800000
KEY_NEG_INF = INT_MIN + 0x7FFFFF
SB_DEAD = -104.0


def _params(sem):
    return pltpu.CompilerParams(dimension_semantics=sem, vmem_limit_bytes=VMEM_LIMIT)


def _tile(n, pref):
    t = min(n, pref)
    while n % t:
        t -= 8
    return t


def _dot_nt(a, b):
    return lax.dot_general(a, b, (((1,), (1,)), ((), ())), preferred_element_type=F32)


def _dot_tn(a, b):
    return lax.dot_general(a, b, (((0,), (0,)), ((), ())), preferred_element_type=F32)


def _dot(a, b):
    return jnp.dot(a, b, preferred_element_type=F32)


def _split3(x):
    h1 = x.astype(BF16)
    r1 = x - h1.astype(F32)
    h2 = r1.astype(BF16)
    h3 = (r1 - h2.astype(F32)).astype(BF16)
    return h1, h2, h3


def _softplus(x):
    return jnp.maximum(x, 0.0) + jnp.log1p(jnp.exp(-jnp.abs(x)))


def _silu(x):
    return x * (1.0 / (1.0 + jnp.exp(-x)))


def _proj_kernel(x_ref, g_ref, w_ref, o_ref, h_ref):
    @pl.when(pl.program_id(1) == 0)
    def _():
        x = x_ref[...]
        ms = jnp.mean(x * x, axis=-1, keepdims=True)
        h_ref[...] = (x * lax.rsqrt(ms + EPS) * g_ref[...]).astype(BF16)

    o_ref[...] = _dot(h_ref[...], w_ref[...])


def _proj(x, g, w):
    t, d = x.shape
    n = w.shape[1]
    tm = _tile(t, 1024)
    tn = 512
    return pl.pallas_call(
        _proj_kernel,
        grid=(t // tm, n // tn),
        in_specs=[pl.BlockSpec((tm, d), lambda i, j: (i, 0)),
                  pl.BlockSpec((1, d), lambda i, j: (0, 0)),
                  pl.BlockSpec((d, tn), lambda i, j: (0, j))],
        out_specs=pl.BlockSpec((tm, tn), lambda i, j: (i, j)),
        out_shape=jax.ShapeDtypeStruct((t, n), F32),
        scratch_shapes=[pltpu.VMEM((tm, d), BF16)],
        compiler_params=_params(("parallel", "arbitrary")),
        name="in_proj",
    )(x, g, w)


def _post_kernel(aq_r, bq_r, bk_r, bv_r, iq_r, ak_r, misc_r, rc_r, rs1_r, rs2_r,
                 gaq_r, gak_r, gik_r, gbq_r, gbk_r,
                 aq_o, ak_o, iq_o, ik_o, iw_o, bq_o, bk_o, bk16_o, bv16_o):
    r = lax.broadcasted_iota(I32, (LANES, LANES), 0) // HEAD_DIM
    c = lax.broadcasted_iota(I32, (LANES, LANES), 1) // HEAD_DIM
    segm = jnp.where(r == c, 1.0, 0.0).astype(BF16)
    rc, rs1, rs2 = rc_r[...], rs1_r[...], rs2_r[...]

    def headnorm(xc, g_row):
        h1, h2, h3 = _split3(xc * xc)
        ss = _dot(h1, segm) + _dot(h2, segm) + _dot(h3, segm)
        return xc * lax.rsqrt(ss * (1.0 / HEAD_DIM) + EPS) * g_row

    def rope(xc):
        return xc * rc + pltpu.roll(xc, 8, 1) * rs1 + pltpu.roll(xc, LANES - 8, 1) * rs2

    for k in range(A_WIDTH // LANES):
        sl = slice(k * LANES, (k + 1) * LANES)
        aq_o[:, sl] = (rope(headnorm(aq_r[:, sl], gaq_r[...])) * (HEAD_DIM ** -0.5)).astype(BF16)
        bq_o[:, sl] = (headnorm(bq_r[:, sl], gbq_r[...]) * (HEAD_DIM ** -0.5)).astype(BF16)
        bk = headnorm(bk_r[:, sl], gbk_r[...])
        bk_o[:, sl] = bk
        bk16_o[:, sl] = bk.astype(BF16)
        bv16_o[:, sl] = bv_r[:, sl].astype(BF16)
    for k in range(IDX_HEADS * IDX_DIM // LANES):
        sl = slice(k * LANES, (k + 1) * LANES)
        iq_o[:, sl] = rope(iq_r[:, sl]).astype(BF16)
    ak_o[...] = rope(headnorm(ak_r[...], gak_r[...]))
    misc = misc_r[...]
    ik_o[...] = rope(headnorm(misc, gik_r[...]))[:, MISC_IK:MISC_IK + IDX_DIM]
    iw_o[...] = misc[:, MISC_IW:MISC_IW + IDX_HEADS] * (IDX_HEADS ** -0.5 * IDX_DIM ** -0.5)


def _post(p, rc, rs1, rs2, gaq, gak, gik, gbq, gbk):
    t = p.shape[0]
    tm = _tile(t, 256)

    def col(width, off):
        return pl.BlockSpec((tm, width), lambda i: (i, off // width))

    def row(width):
        return pl.BlockSpec((1, width), lambda i: (0, 0))

    def out(width):
        return pl.BlockSpec((tm, width), lambda i: (i, 0))

    widths_dtypes = [(A_WIDTH, BF16), (LANES, F32), (IDX_HEADS * IDX_DIM, BF16), (IDX_DIM, F32),
                     (IDX_HEADS, F32), (B_WIDTH, BF16), (B_WIDTH, F32), (B_WIDTH, BF16), (B_WIDTH, BF16)]
    return pl.pallas_call(
        _post_kernel,
        grid=(t // tm,),
        in_specs=[col(A_WIDTH, COL_AQ), col(B_WIDTH, COL_BQ), col(B_WIDTH, COL_BK), col(B_WIDTH, COL_BV),
                  col(IDX_HEADS * IDX_DIM, COL_IQ), col(LANES, COL_AK), col(LANES, COL_MISC),
                  out(LANES), out(LANES), out(LANES),
                  row(LANES), row(LANES), row(LANES), row(LANES), row(LANES)],
        out_specs=[out(w) for w, _ in widths_dtypes],
        out_shape=[jax.ShapeDtypeStruct((t, w), dt) for w, dt in widths_dtypes],
        compiler_params=_params(("parallel",)),
        name="head_post",
    )(p, p, p, p, p, p, p, rc, rs1, rs2, gaq, gak, gik, gbq, gbk)


CACHE_CHUNK = 4096
NEW_CHUNK = 512


def _dsap_kernel(*refs, tq, sn, lc, topk, kblk, has_cache, layer, cache_chunk):
    nb = LANES // tq
    if has_cache:
        (aq_ref, iq_ref, iw_ref, ikn_ref, akn_ref, avn_ref, ikc_hbm, akc_hbm, avc_hbm, o_ref,
         key_ref, bias_ref, ik16_ref, ak16_ref, vt_ref, iqh_ref, aqh_ref, m_ref, acc_ref,
         ikbuf, akbuf, avbuf, sem) = refs
    else:
        (aq_ref, iq_ref, iw_ref, ikn_ref, akn_ref, avn_ref, o_ref,
         key_ref, bias_ref, ik16_ref, ak16_ref, vt_ref, iqh_ref, aqh_ref, m_ref, acc_ref) = refs
    grp = pl.program_id(0)
    qi = pl.program_id(1)
    n_keys = lc + sn
    lp = ik16_ref.shape[0]
    rep = A_HEADS // A_KV_HEADS
    ch = cache_chunk
    chn = NEW_CHUNK

    @pl.when(qi == 0)
    def _():
        def put(row, j, ik, k0, k1, v_t):
            n = ik.shape[0]
            cols = slice(j * IDX_DIM, (j + 1) * IDX_DIM)
            ik16_ref[pl.ds(row, n), cols] = ik.astype(BF16)
            ak16_ref[0, pl.ds(row, n), cols] = k0.astype(BF16)
            ak16_ref[1, pl.ds(row, n), cols] = k1.astype(BF16)
            vt = v_t.astype(BF16)
            for e in range(n // kblk):
                for g in range(A_KV_HEADS):
                    vt_ref[g, row // kblk + e, 0:HEAD_DIM, j * kblk:(j + 1) * kblk] = (
                        vt[g * HEAD_DIM:(g + 1) * HEAD_DIM, e * kblk:(e + 1) * kblk])

        if has_cache:
            nc = lc // ch

            def copies(j, c, slot):
                b = grp * nb + j
                src = pl.ds(pl.multiple_of(c * ch, ch), ch)
                return (pltpu.make_async_copy(ikc_hbm.at[layer, b, :, src], ikbuf.at[slot], sem.at[0, slot]),
                        pltpu.make_async_copy(akc_hbm.at[layer, b, :, :, src], akbuf.at[slot], sem.at[1, slot]),
                        pltpu.make_async_copy(avc_hbm.at[layer, b, :, :, src], avbuf.at[slot], sem.at[2, slot]))

            def start(j, c, slot):
                for cp in copies(j, c, slot):
                    cp.start()

            start(0, 0, 0)
            for j in range(nb):
                def chunk_body(c, carry, j=j):
                    slot = (j * nc + c) % 2
                    for cp in copies(j, c, slot):
                        cp.wait()

                    @pl.when(c + 1 < nc)
                    def _():
                        start(j, c + 1, 1 - slot)

                    if j + 1 < nb:
                        @pl.when(c + 1 == nc)
                        def _():
                            start(j + 1, 0, ((j + 1) * nc) % 2)

                    kk = akbuf[slot].reshape(LANES, ch).T
                    ik = jnp.concatenate([ikbuf[slot], jnp.zeros((LANES - IDX_DIM, ch), F32)], axis=0).T
                    put(pl.multiple_of(c * ch, ch), j, ik[:, 0:IDX_DIM], kk[:, 0:HEAD_DIM], kk[:, HEAD_DIM:LANES],
                        avbuf[slot].reshape(LANES, ch))
                    return carry

                lax.fori_loop(0, nc, chunk_body, 0)

        for j in range(nb):
            if sn % chn == 0:
                def new_body(c, carry, j=j):
                    src = pl.ds(pl.multiple_of(j * sn + c * chn, chn), chn)
                    ak = akn_ref[src, :]
                    put(pl.multiple_of(lc + c * chn, chn), j, ikn_ref[src, :], ak[:, 0:HEAD_DIM], ak[:, HEAD_DIM:LANES],
                        avn_ref[src, :].T)
                    return carry

                lax.fori_loop(0, sn // chn, new_body, 0)
            else:
                def padded(ref):
                    return jnp.concatenate([ref[j * sn:(j + 1) * sn, :],
                                            jnp.zeros((kblk - sn, ref.shape[-1]), F32)], axis=0)

                ak = padded(akn_ref)
                put(lc, j, padded(ikn_ref), ak[:, 0:HEAD_DIM], ak[:, HEAD_DIM:LANES], padded(avn_ref).T)
        for g in range(A_KV_HEADS):
            vt_ref[g, :, HEAD_DIM:LANES, :] = jnp.ones((lp // kblk, LANES - HEAD_DIM, nb * kblk), BF16)

    qstart = lc + qi * tq
    kmax = jnp.minimum(n_keys, ((qstart + tq - 1) // CHUNK + 1) * CHUNK)
    nkb = (kmax + kblk - 1) // kblk
    lane = lax.broadcasted_iota(I32, (1, LANES), 1)
    qpos = qstart + lane % tq
    krow = lax.broadcasted_iota(I32, (kblk, 1), 0)
    lane_group = lane // tq
    two = 2 * LANES

    def head_major(dst, src, nheads):
        x_t = src[...].astype(F32).T
        for h in range(nheads):
            xh = x_t[h * HEAD_DIM:(h + 1) * HEAD_DIM, :]
            if nb > 1:
                xh = jnp.concatenate([jnp.where(lane_group == j, xh, 0.0) for j in range(nb)], axis=0)
            dst[:, h * LANES:(h + 1) * LANES] = xh.astype(BF16)

    head_major(iqh_ref, iq_ref, IDX_HEADS)
    head_major(aqh_ref, aq_ref, A_HEADS)
    iw_t = jnp.concatenate([iw_ref[...], jnp.zeros((LANES, LANES - IDX_HEADS), F32)], axis=1).T

    def score_body(kb, carry):
        for sub in range(kblk // LANES):
            start = pl.multiple_of(kb * kblk + sub * LANES, LANES)
            ikb = ik16_ref[pl.ds(start, LANES), :]
            acc = jnp.zeros((LANES, LANES), F32)
            for pr in range(IDX_HEADS // 2):
                logit = _dot(ikb, iqh_ref[:, pr * two:(pr + 1) * two])
                for e in range(2):
                    h = 2 * pr + e
                    acc = acc + iw_t[h:h + 1, :] * jnp.maximum(logit[:, e * LANES:(e + 1) * LANES], 0.0)
            kpos = start + krow[0:LANES]
            adm = jnp.logical_and(kpos // CHUNK <= qpos // CHUNK, kpos < n_keys)
            bits = pltpu.bitcast(jnp.where(adm, acc, -jnp.inf), I32)
            bits = jnp.where(bits == INT_MIN, 0, bits)
            key_ref[kb, sub * LANES:(sub + 1) * LANES, :] = bits ^ ((bits >> 31) & 0x7FFFFFFF)
        return carry

    lax.fori_loop(0, nkb, score_body, 0)

    def count(pred):
        def body(kb, acc):
            hit = jnp.where(pred(key_ref[kb], kb), 1.0, 0.0)
            return acc + jnp.sum(hit.reshape(kblk // 64, 64, LANES), axis=0)
        acc = lax.fori_loop(0, nkb, body, jnp.zeros((64, LANES), F32))
        return jnp.sum(acc, axis=0, keepdims=True)

    kf = float(topk)

    def bit_body(i, thr):
        cand = thr + (jnp.int32(1) << (31 - i))
        cnt = count(lambda key, kb: key >= cand)
        return jnp.where(cnt >= kf, cand, thr)

    thr = lax.fori_loop(0, 32, bit_body, jnp.full((1, LANES), INT_MIN, I32))
    n_gt = count(lambda key, kb: key > thr)
    n_ge = count(lambda key, kb: key >= thr)
    need = kf - n_gt

    def tie_search():
        def jbody(i, jl):
            cand = jl + (jnp.int32(1) << (14 - i))
            cnt = count(lambda key, kb: jnp.logical_and(key == thr, kb * kblk + krow < cand))
            return jnp.where(cnt <= need, cand, jl)
        return lax.fori_loop(0, 15, jbody, jnp.zeros((1, LANES), I32))

    jlim = lax.cond(jnp.max(n_ge) > kf, tie_search, lambda: jnp.full((1, LANES), 2 ** 30, I32))

    def bias_body(kb, carry):
        key = key_ref[kb]
        kpos = kb * kblk + krow
        sel = jnp.logical_or(key > thr, jnp.logical_and(key == thr, kpos < jlim))
        sel = jnp.logical_and(sel, jnp.logical_and(key > KEY_NEG_INF, key < KEY_POS_INF))
        bias_ref[kb] = jnp.where(sel, 0.0, NEG_BIG)
        return carry

    lax.fori_loop(0, nkb, bias_body, 0)

    m_ref[...] = jnp.full(m_ref.shape, -3e38, F32)
    acc_ref[...] = jnp.zeros(acc_ref.shape, F32)
    pair_group = (lax.broadcasted_iota(I32, (1, 2 * LANES), 1) % LANES) // tq

    def att_body(kb, carry):
        start = pl.multiple_of(kb * kblk, kblk)
        bias = bias_ref[kb]
        m_all = m_ref[...]
        kks = [ak16_ref[g, pl.ds(start, kblk), :] for g in range(A_KV_HEADS)]
        s_ts = [_dot(kks[h0 // rep], aqh_ref[:, h0 * LANES:(h0 + 2) * LANES]) for h0 in range(0, A_HEADS, 2)]
        ps, alphas, m_rows = [], [], []
        for h in range(A_HEADS):
            x = s_ts[h // 2][:, (h % 2) * LANES:(h % 2 + 1) * LANES] + bias
            part = jnp.max(x.reshape(kblk // 64, 64, LANES), axis=0)
            bmax = jnp.max(part, axis=0, keepdims=True)
            m_old = m_all[h:h + 1, :]
            m_new = jnp.maximum(m_old, bmax)
            alphas.append(jnp.exp(m_old - m_new))
            ps.append(jnp.exp(x - m_new).astype(BF16))
            m_rows.append(m_new)
        m_ref[...] = jnp.concatenate(m_rows, axis=0)
        upds = []
        for h0 in range(0, A_HEADS, 2):
            pp = jnp.concatenate(ps[h0:h0 + 2], axis=1)
            if nb > 1:
                pp = jnp.concatenate([jnp.where(pair_group == j, pp, jnp.zeros_like(pp)) for j in range(nb)], axis=0)
            upds.append(_dot(vt_ref[h0 // rep, kb], pp))
        for h in range(A_HEADS):
            acc_ref[h] = acc_ref[h] * alphas[h] + upds[h // 2][:, (h % 2) * LANES:(h % 2 + 1) * LANES]
        return carry

    lax.fori_loop(0, nkb, att_body, 0)

    outs = []
    for h in range(A_HEADS):
        a = acc_ref[h]
        outs.append(a[0:HEAD_DIM, :] / a[HEAD_DIM:LANES, :])
    o_ref[...] = jnp.concatenate(outs, axis=0).T.astype(o_ref.dtype)


def _dsap_call(aq, iq, iw, ik, ak, p, caches, *, batch, seq_q, tq, row0, layer=0, kblk=256):
    has_cache = caches is not None
    lc = caches[0].shape[-1] if has_cache else 0
    sn = seq_q
    nb = LANES // tq
    n_keys = lc + sn
    lp = -(-n_keys // kblk) * kblk
    assert LANES % tq == 0 and batch % nb == 0 and sn % tq == 0 and row0 % (nb * sn) == 0 and row0 % LANES == 0
    cache_chunk = min(CACHE_CHUNK, lc) if has_cache else 0
    assert not has_cache or (lc % cache_chunk == 0 and cache_chunk % NEW_CHUNK == 0)
    assert NEW_CHUNK % kblk == 0 and ((sn % NEW_CHUNK == 0 and lp == n_keys) or (sn < kblk and lp == lc + kblk))
    groups = batch // nb
    nqb = seq_q // tq
    assert nb == 1 or nqb == 1
    rb0 = row0 // LANES
    nb0 = row0 // (nb * sn)
    topk = min(TOPK_MAX, n_keys // 4)
    nkb_max = lp // kblk

    def qspec(width):
        return pl.BlockSpec((LANES, width), lambda g, i: (rb0 + g * nqb + i, 0))

    def nspec(width, colblk=0):
        return pl.BlockSpec((nb * sn, width), lambda g, i: (nb0 + g, colblk))

    in_specs = [qspec(A_WIDTH), qspec(IDX_HEADS * IDX_DIM), qspec(IDX_HEADS),
                nspec(IDX_DIM), nspec(LANES), nspec(LANES, COL_AV // LANES)]
    args = [aq, iq, iw, ik, ak, p]
    scratch = [pltpu.VMEM((nkb_max, kblk, LANES), I32),
               pltpu.VMEM((nkb_max, kblk, LANES), F32),
               pltpu.VMEM((lp, nb * IDX_DIM), BF16),
               pltpu.VMEM((A_KV_HEADS, lp, nb * HEAD_DIM), BF16),
               pltpu.VMEM((A_KV_HEADS, nkb_max, LANES, nb * kblk), BF16),
               pltpu.VMEM((nb * IDX_DIM, IDX_HEADS * LANES), BF16),
               pltpu.VMEM((nb * HEAD_DIM, A_HEADS * LANES), BF16),
               pltpu.VMEM((A_HEADS, LANES), F32),
               pltpu.VMEM((A_HEADS, LANES, LANES), F32)]
    if has_cache:
        in_specs += [pl.BlockSpec(memory_space=pl.ANY)] * 3
        args += list(caches)
        scratch += [pltpu.VMEM((2, IDX_DIM, cache_chunk), F32),
                    pltpu.VMEM((2, A_KV_HEADS, HEAD_DIM, cache_chunk), F32),
                    pltpu.VMEM((2, A_KV_HEADS, HEAD_DIM, cache_chunk), F32),
                    pltpu.SemaphoreType.DMA((3, 2))]
    kern = functools.partial(_dsap_kernel, tq=tq, sn=sn, lc=lc, topk=topk, kblk=kblk, has_cache=has_cache,
                             layer=layer, cache_chunk=cache_chunk)
    return pl.pallas_call(
        kern,
        grid=(groups, nqb),
        in_specs=in_specs,
        out_specs=pl.BlockSpec((LANES, A_WIDTH), lambda g, i: (g * nqb + i, 0)),
        out_shape=jax.ShapeDtypeStruct((batch * seq_q, A_WIDTH), BF16),
        scratch_shapes=scratch,
        compiler_params=_params(("parallel", "arbitrary")),
        name="dsa_attention",
    )(*args)


def _sb_kernel(*refs, tq, sn, lc, kblk, has_cache, layer):
    if has_cache:
        q_ref, kn_ref, vn_ref, kc_hbm, vc_hbm, o_ref, qh_ref, carry_ref, acc_ref, kbuf, vbuf, sem = refs
    else:
        q_ref, kn_ref, vn_ref, o_ref, qh_ref, carry_ref, acc_ref = refs
    qlocal = pl.program_id(1) * tq
    qpos = qlocal + lax.broadcasted_iota(I32, (tq, 1), 0)
    kiota = lax.broadcasted_iota(I32, (1, kblk), 1)
    r = lax.broadcasted_iota(I32, (kblk, kblk), 0)
    c = lax.broadcasted_iota(I32, (kblk, kblk), 1)
    upper = jnp.where(r > c, 1.0, 0.0).astype(BF16)
    heads = range(B_HEADS)
    for h in heads:
        qh_ref[h] = q_ref[:, h * HEAD_DIM:(h + 1) * HEAD_DIM]
    carry_ref[...] = jnp.zeros(carry_ref.shape, F32)
    acc_ref[...] = jnp.zeros(acc_ref.shape, F32)

    def visit(score, apply, causal):
        zs = [score(h) for h in heads]
        sps = [_softplus(z) for z in zs]
        lks = [-sp if causal is None else jnp.where(causal, -sp, 0.0) for sp in sps]
        carries = [carry_ref[h] for h in heads]
        laters = []
        for h in heads:
            l1, l2, l3 = _split3(lks[h])
            laters.append(carries[h] + (_dot(l1, upper) + _dot(l2, upper) + _dot(l3, upper)))
        weights = [jnp.exp((zs[h] - sps[h]) + laters[h]) for h in heads]
        if causal is not None:
            weights = [jnp.where(causal, a, 0.0) for a in weights]
        upds = [apply(h, weights[h].astype(BF16)) for h in heads]
        new = [carries[h] + jnp.sum(lks[h], axis=1, keepdims=True) for h in heads]
        top = new[0]
        for h in heads:
            acc_ref[h] += upds[h]
            carry_ref[h] = new[h]
            top = jnp.maximum(top, new[h])
        return jnp.max(top) > SB_DEAD

    def alive_cond(state):
        j, alive = state
        return jnp.logical_and(j >= 0, alive)

    def hcols(h):
        return slice(h * HEAD_DIM, (h + 1) * HEAD_DIM)

    if sn >= kblk:
        def new_body(state):
            j, _ = state
            start = pl.multiple_of(j * kblk, kblk)
            alive = visit(lambda h: _dot_nt(qh_ref[h], kn_ref[pl.ds(start, kblk), hcols(h)]),
                          lambda h, a: _dot(a, vn_ref[pl.ds(start, kblk), hcols(h)]), start + kiota < qpos)
            return j - 1, alive

        _, alive = lax.while_loop(alive_cond, new_body, ((qlocal + tq - 1) // kblk, True))
    else:
        def padded(ref, h):
            return jnp.concatenate([ref[:, hcols(h)], jnp.zeros((kblk - sn, HEAD_DIM), BF16)], axis=0)

        alive = visit(lambda h: _dot_nt(qh_ref[h], padded(kn_ref, h)), lambda h, a: _dot(a, padded(vn_ref, h)),
                      kiota < qpos)

    if has_cache:
        b = pl.program_id(0)

        def copies(j, slot):
            src = pl.ds(pl.multiple_of(j * kblk, kblk), kblk)
            return (pltpu.make_async_copy(kc_hbm.at[layer, b, :, :, src], kbuf.at[slot], sem.at[0, slot]),
                    pltpu.make_async_copy(vc_hbm.at[layer, b, :, :, src], vbuf.at[slot], sem.at[1, slot]))

        def start(j, slot):
            for cp in copies(j, slot):
                cp.start()

        def wait(j, slot):
            for cp in copies(j, slot):
                cp.wait()

        j0 = lc // kblk - 1

        @pl.when(alive)
        def _():
            start(j0, j0 % 2)

        def cache_body(state):
            j, _ = state
            slot = j % 2
            wait(j, slot)

            @pl.when(j > 0)
            def _():
                start(j - 1, 1 - slot)

            alive = visit(lambda h: _dot(qh_ref[h], kbuf[slot, h].astype(BF16)),
                          lambda h, a: _dot_nt(a, vbuf[slot, h].astype(BF16)), None)
            return j - 1, alive

        j_end, _ = lax.while_loop(alive_cond, cache_body, (j0, alive))

        @pl.when(jnp.logical_and(alive, j_end >= 0))
        def _():
            wait(j_end, j_end % 2)

    for h in heads:
        o_ref[:, hcols(h)] = acc_ref[h].astype(o_ref.dtype)


def _sb_call(bq, bk16, bv16, caches, *, batch, seq_q, tq, row0, layer=0, kblk=128):
    has_cache = caches is not None
    lc = caches[0].shape[-1] if has_cache else 0
    sn = seq_q
    assert kblk == LANES and lc % kblk == 0 and row0 % sn == 0 and sn % tq == 0
    assert sn % kblk == 0 or (sn < kblk and tq == sn)
    nqb = seq_q // tq
    rb0 = row0 // tq
    nb0 = row0 // sn
    in_specs = [pl.BlockSpec((tq, B_WIDTH), lambda b, i: (rb0 + b * nqb + i, 0)),
                pl.BlockSpec((sn, B_WIDTH), lambda b, i: (nb0 + b, 0)),
                pl.BlockSpec((sn, B_WIDTH), lambda b, i: (nb0 + b, 0))]
    args = [bq, bk16, bv16]
    scratch = [pltpu.VMEM((B_HEADS, tq, HEAD_DIM), BF16), pltpu.VMEM((B_HEADS, tq, LANES), F32),
               pltpu.VMEM((B_HEADS, tq, HEAD_DIM), F32)]
    if has_cache:
        in_specs += [pl.BlockSpec(memory_space=pl.ANY)] * 2
        args += list(caches)
        scratch += [pltpu.VMEM((2, B_HEADS, HEAD_DIM, kblk), F32), pltpu.VMEM((2, B_HEADS, HEAD_DIM, kblk), F32),
                    pltpu.SemaphoreType.DMA((2, 2))]
    kern = functools.partial(_sb_kernel, tq=tq, sn=sn, lc=lc, kblk=kblk, has_cache=has_cache, layer=layer)
    return pl.pallas_call(
        kern,
        grid=(batch, nqb),
        in_specs=in_specs,
        out_specs=pl.BlockSpec((tq, B_WIDTH), lambda b, i: (b * nqb + i, 0)),
        out_shape=jax.ShapeDtypeStruct((batch * seq_q, B_WIDTH), BF16),
        scratch_shapes=scratch,
        compiler_params=_params(("parallel", "arbitrary")),
        name="stick_breaking",
    )(*args)


def _ssd_kernel(z_ref, xbc_ref, misc_ref, cprev_ref, sprev_ref, cw_ref, cb_ref, dtb_ref, alog_ref, dsk_ref,
                ng_ref, y_ref, snew_ref, xpad_ref, st_ref, *, cs):
    ci = pl.program_id(1)
    nci = pl.num_programs(1)
    hpg = C_HEADS // C_GROUPS
    gw = hpg * C_HEAD_DIM

    @pl.when(ci == 0)
    def _():
        xpad_ref[5:8, :] = cprev_ref[0]
        for g in range(C_GROUPS):
            st_ref[g] = sprev_ref[0, g * hpg:(g + 1) * hpg].reshape(gw, D_STATE).T

    @pl.when(ci > 0)
    def _():
        xpad_ref[5:8, :] = xpad_ref[cs + 5:cs + 8, :]

    xpad_ref[8:8 + cs, :] = xbc_ref[...]
    xc = cb_ref[...] + xpad_ref[5:5 + cs, :] * cw_ref[0:1, :]
    for w in range(1, CONV_WIDTH):
        xc = xc + xpad_ref[5 + w:5 + w + cs, :] * cw_ref[w:w + 1, :]
    xc = _silu(xc)

    dt = _softplus(misc_ref[...] + dtb_ref[...])
    a = dt * (-jnp.exp(alog_ref[...]))
    r = lax.broadcasted_iota(I32, (cs, cs), 0)
    c = lax.broadcasted_iota(I32, (cs, cs), 1)
    tril = r >= c
    tril16 = jnp.where(tril, 1.0, 0.0).astype(BF16)
    a1, a2, a3 = _split3(a)
    a_cum = _dot(tril16, a1) + _dot(tril16, a2) + _dot(tril16, a3)
    pad = LANES - cs
    a_sq = a_cum if pad == 0 else jnp.concatenate([a_cum, jnp.zeros((pad, LANES), F32)], axis=0)
    a_cum_t = a_sq.T
    a_last = a_cum[cs - 1:cs, :]
    exp_a = jnp.exp(a_cum)
    dte = jnp.exp(a_last - a_cum)
    exp_last = jnp.exp(a_last)
    low_half = lax.broadcasted_iota(I32, (1, LANES), 1) < C_HEAD_DIM

    for g in range(C_GROUPS):
        bm = xc[:, C_D_INNER + g * D_STATE:C_D_INNER + (g + 1) * D_STATE].astype(BF16)
        cm = xc[:, C_D_INNER + (C_GROUPS + g) * D_STATE:C_D_INNER + (C_GROUPS + g + 1) * D_STATE].astype(BF16)
        cb = _dot_nt(cm, bm)
        st = st_ref[g]
        y_off = _dot(cm, st.astype(BF16))
        y_pairs, x_pairs, dec_pairs = [], [], []
        for pp in range(hpg // 2):
            h0 = g * hpg + 2 * pp
            ln0, ln1 = MISC_DT + h0, MISC_DT + h0 + 1

            def pat(v, ln0=ln0, ln1=ln1):
                return jnp.where(low_half, v[:, ln0:ln0 + 1], v[:, ln1:ln1 + 1])

            xd = xc[:, h0 * C_HEAD_DIM:(h0 + 2) * C_HEAD_DIM] * pat(dt)
            xd16 = xd.astype(BF16)
            y_pair = y_off[:, pp * LANES:(pp + 1) * LANES] * pat(exp_a)
            for ln, keep in ((ln0, low_half), (ln1, jnp.logical_not(low_half))):
                seg = a_cum[:, ln:ln + 1] - a_cum_t[ln:ln + 1, 0:cs]
                decay = jnp.where(tril, jnp.exp(jnp.where(tril, seg, 0.0)), 0.0)
                y_pair = y_pair + _dot((cb * decay).astype(BF16), jnp.where(keep, xd16, jnp.zeros_like(xd16)))
            y_pairs.append(y_pair)
            x_pairs.append((xd * pat(dte)).astype(BF16))
            dec_pairs.append(pat(exp_last))
        y = jnp.concatenate(y_pairs, axis=1)
        st_new = st * jnp.concatenate(dec_pairs, axis=1) + _dot_tn(bm, jnp.concatenate(x_pairs, axis=1))
        st_ref[g] = st_new

        gs = slice(g * gw, (g + 1) * gw)
        y = y + xc[:, gs] * dsk_ref[:, gs]
        y = y * _silu(z_ref[:, gs])
        y = y * lax.rsqrt(jnp.mean(y * y, axis=1, keepdims=True) + EPS) * ng_ref[:, gs]
        y_ref[:, gs] = y.astype(y_ref.dtype)

        @pl.when(ci == nci - 1)
        def _():
            snew_ref[0, g * hpg:(g + 1) * hpg] = st_new.T.reshape(hpg, C_HEAD_DIM, D_STATE)


def _ssd_call(p, cprev, sprev, cw, cb, dtb, alog, dsk, ng, *, batch, seq, cs, row0):
    nci = seq // cs
    rb0 = row0 // cs

    def pcol(width, off):
        return pl.BlockSpec((cs, width), lambda b, i: (rb0 + b * nci + i, off // width))

    def row(width, rows=1):
        return pl.BlockSpec((rows, width), lambda b, i: (0, 0))

    kern = functools.partial(_ssd_kernel, cs=cs)
    return pl.pallas_call(
        kern,
        grid=(batch, nci),
        in_specs=[pcol(C_D_INNER, COL_Z), pcol(C_CONV_DIM, COL_XBC), pcol(LANES, COL_MISC),
                  pl.BlockSpec((1, CONV_WIDTH - 1, C_CONV_DIM), lambda b, i: (b, 0, 0)),
                  pl.BlockSpec((1, C_HEADS, C_HEAD_DIM, D_STATE), lambda b, i: (b, 0, 0, 0)),
                  row(C_CONV_DIM, CONV_WIDTH), row(C_CONV_DIM), row(LANES), row(LANES),
                  row(C_D_INNER), row(C_D_INNER)],
        out_specs=[pl.BlockSpec((cs, C_D_INNER), lambda b, i: (b * nci + i, 0)),
                   pl.BlockSpec((1, C_HEADS, C_HEAD_DIM, D_STATE), lambda b, i: (b, 0, 0, 0))],
        out_shape=[jax.ShapeDtypeStruct((batch * seq, C_D_INNER), BF16),
                   jax.ShapeDtypeStruct((batch, C_HEADS, C_HEAD_DIM, D_STATE), F32)],
        scratch_shapes=[pltpu.VMEM((cs + 8, C_CONV_DIM), F32),
                        pltpu.VMEM((C_GROUPS, D_STATE, C_HEADS // C_GROUPS * C_HEAD_DIM), F32)],
        compiler_params=_params(("parallel", "arbitrary")),
        name="ssd_mixer",
    )(p, p, p, cprev, sprev, cw, cb, dtb, alog, dsk, ng)


def _outproj_kernel(x_ref, yap_ref, ybp_ref, ycp_ref, yas_ref, ybs_ref, ycs_ref, wa_ref, wb_ref, wc_ref, o_ref, *,
                    prompt_tiles):
    def mix(ya_ref, yb_ref, yc_ref):
        o_ref[...] = (x_ref[...] + _dot(ya_ref[...], wa_ref[...]) + _dot(yb_ref[...], wb_ref[...])
                      + _dot(yc_ref[...], wc_ref[...]))

    is_prompt = pl.program_id(0) < prompt_tiles

    @pl.when(is_prompt)
    def _():
        mix(yap_ref, ybp_ref, ycp_ref)

    @pl.when(jnp.logical_not(is_prompt))
    def _():
        mix(yas_ref, ybs_ref, ycs_ref)


def _outproj(x, y_prompt, y_sample, w):
    t, d = x.shape
    tp, ts = y_prompt[0].shape[0], y_sample[0].shape[0]
    tm = _tile(math.gcd(tp, ts), 256)
    npt = tp // tm

    def tok(width):
        return pl.BlockSpec((tm, width), lambda i: (i, 0))

    def ptok(width):
        return pl.BlockSpec((tm, width), lambda i: (jnp.minimum(i, npt - 1), 0))

    def stok(width):
        return pl.BlockSpec((tm, width), lambda i: (jnp.maximum(i - npt, 0), 0))

    widths = (A_WIDTH, B_WIDTH, C_D_INNER)
    return pl.pallas_call(
        functools.partial(_outproj_kernel, prompt_tiles=npt),
        grid=(t // tm,),
        in_specs=[tok(d)] + [ptok(wd) for wd in widths] + [stok(wd) for wd in widths] + [
            pl.BlockSpec((A_WIDTH, d), lambda i: (0, 0)),
            pl.BlockSpec((B_WIDTH, d), lambda i: (1, 0)),
            pl.BlockSpec((C_D_INNER, d), lambda i: (1, 0))],
        out_specs=tok(d),
        out_shape=jax.ShapeDtypeStruct((t, d), F32),
        compiler_params=_params(("parallel",)),
        name="out_proj",
    )(x, *y_prompt, *y_sample, w, w, w)


def _ffn_kernel(x_ref, g_ref, wg_ref, wu_ref, wd_ref, o_ref, h_ref):
    @pl.when(pl.program_id(1) == 0)
    def _():
        x = x_ref[...]
        ms = jnp.mean(x * x, axis=-1, keepdims=True)
        h_ref[...] = (x * lax.rsqrt(ms + EPS) * g_ref[...]).astype(BF16)
        o_ref[...] = x

    h = h_ref[...]
    act = _silu(_dot(h, wg_ref[...])) * _dot(h, wu_ref[...])
    o_ref[...] += _dot(act.astype(BF16), wd_ref[...])


def _ffn(x, g, wg, wu, wd):
    t, d = x.shape
    f = wg.shape[1]
    tm = _tile(t, 512)
    tf = 512
    return pl.pallas_call(
        _ffn_kernel,
        grid=(t // tm, f // tf),
        in_specs=[pl.BlockSpec((tm, d), lambda i, j: (i, 0)),
                  pl.BlockSpec((1, d), lambda i, j: (0, 0)),
                  pl.BlockSpec((d, tf), lambda i, j: (0, j)),
                  pl.BlockSpec((d, tf), lambda i, j: (0, j)),
                  pl.BlockSpec((tf, d), lambda i, j: (j, 0))],
        out_specs=pl.BlockSpec((tm, d), lambda i, j: (i, 0)),
        out_shape=jax.ShapeDtypeStruct((t, d), F32),
        scratch_shapes=[pltpu.VMEM((tm, d), BF16)],
        compiler_params=_params(("parallel", "arbitrary")),
        name="swiglu",
    )(x, g, wg, wu, wd)


def _rope_rows(pos):
    half = ROPE_DIM // 2
    inv = 1.0 / (ROPE_THETA ** (jnp.arange(0, ROPE_DIM, 2, dtype=F32) / ROPE_DIM))
    ang = pos.astype(F32)[:, None] * inv[None, :]
    cos, sin = jnp.cos(ang), jnp.sin(ang)
    n = pos.shape[0]
    one = jnp.ones((n, HEAD_DIM - ROPE_DIM), F32)
    zero = jnp.zeros((n, HEAD_DIM - ROPE_DIM), F32)
    zh = jnp.zeros((n, half), F32)
    rc = jnp.concatenate([cos, cos, one], axis=1)
    rs1 = jnp.concatenate([zh, sin, zero], axis=1)
    rs2 = jnp.concatenate([-sin, zh, zero], axis=1)
    return tuple(jnp.tile(a, (1, LANES // HEAD_DIM)) for a in (rc, rs1, rs2))


def _w_in_segments():
    offs = [0]
    for n in IN_SIZES:
        offs.append(offs[-1] + n)
    dst = (COL_AQ, COL_AK, COL_AV, COL_IQ, COL_MISC + MISC_IK, COL_MISC + MISC_IW, COL_BQ, COL_BK, COL_BV,
           COL_Z, COL_XBC, COL_MISC + MISC_DT)
    return [(offs[i], dst[i], IN_SIZES[i]) for i in range(len(IN_SIZES))]


def _regroup_kernel(w_ref, o_ref):
    for src, dst, width in _w_in_segments():
        o_ref[:, dst:dst + width] = w_ref[0, :, src:src + width].astype(BF16)
    used = COL_MISC + MISC_DT + C_HEADS
    o_ref[:, used:] = jnp.zeros((o_ref.shape[0], P_COLS - used), BF16)


def _rearrange_w_in(w, layer):
    _, d, n = w.shape
    tr = _tile(d, 256)
    return pl.pallas_call(
        _regroup_kernel,
        grid=(d // tr,),
        in_specs=[pl.BlockSpec((1, tr, n), lambda i: (layer, i, 0))],
        out_specs=pl.BlockSpec((tr, P_COLS), lambda i: (i, 0)),
        out_shape=jax.ShapeDtypeStruct((d, P_COLS), BF16),
        compiler_params=_params(("parallel",)),
        name="regroup_w_in",
    )(w)


def _lane_row(v, off):
    return jnp.zeros((1, LANES), F32).at[0, off:off + v.shape[0]].set(v)


def _gain_row(g):
    return jnp.tile(g, LANES // HEAD_DIM)[None, :]


def kernel(x_prompt, x_sample, cache_a_k, cache_a_v, cache_a_idx_k, cache_b_k, cache_b_v, state_c_ssm, state_c_conv, norm1_g, w_in, a_q_norm_g, a_k_norm_g, idx_k_norm_g, b_q_norm_g, b_k_norm_g, c_conv_w, c_conv_b, c_dt_bias, c_a_log, c_d, c_norm_g, w_out, norm2_g, w_gate, w_up, w_down):
    bp, sp, d = x_prompt.shape
    bs, ss, _ = x_sample.shape
    depth = w_in.shape[0]
    past = cache_a_k.shape[2]
    tp, ts = bp * sp, bs * ss
    tq_p = 128
    cs_p = 128 if sp % 128 == 0 else sp
    assert ss >= CONV_WIDTH - 1 and sp >= CONV_WIDTH - 1

    pos = jnp.concatenate([jnp.tile(jnp.arange(sp, dtype=I32), bp),
                           jnp.tile(past + jnp.arange(ss, dtype=I32), bs)])
    rc, rs1, rs2 = _rope_rows(pos)
    x = jnp.concatenate([x_prompt.reshape(tp, d), x_sample.reshape(ts, d)], axis=0)
    zero_conv = jnp.zeros((bp, CONV_WIDTH - 1, C_CONV_DIM), F32)
    zero_ssm = jnp.zeros((bp, C_HEADS, C_HEAD_DIM, D_STATE), F32)

    p_states, s_states = [], []
    for l in range(depth):
        p = _proj(x, norm1_g[l][None, :], _rearrange_w_in(w_in, l))
        gik = jnp.concatenate([idx_k_norm_g[l], jnp.ones((LANES - IDX_DIM,), F32)])[None, :]
        (aq, ak, iq, ik, iw, bq, bk, bk16, bv16) = _post(
            p, rc, rs1, rs2, _gain_row(a_q_norm_g[l]), _gain_row(a_k_norm_g[l]), gik,
            _gain_row(b_q_norm_g[l]), _gain_row(b_k_norm_g[l]))

        a_caches = (jnp.transpose(cache_a_idx_k, (0, 1, 3, 2)), jnp.transpose(cache_a_k, (0, 1, 3, 4, 2)),
                    jnp.transpose(cache_a_v, (0, 1, 3, 4, 2)))
        ya_p = _dsap_call(aq, iq, iw, ik, ak, p, None, batch=bp, seq_q=sp, tq=tq_p, row0=0)
        ya_s = _dsap_call(aq, iq, iw, ik, ak, p, a_caches, batch=bs, seq_q=ss, tq=ss, row0=tp, layer=l)
        b_caches = (jnp.transpose(cache_b_k, (0, 1, 3, 4, 2)), jnp.transpose(cache_b_v, (0, 1, 3, 4, 2)))
        yb_p = _sb_call(bq, bk16, bv16, None, batch=bp, seq_q=sp, tq=tq_p, row0=0)
        yb_s = _sb_call(bq, bk16, bv16, b_caches, batch=bs, seq_q=ss, tq=ss, row0=tp, layer=l)
        cw, cb = c_conv_w[l], c_conv_b[l][None, :]
        dtb, alog = _lane_row(c_dt_bias[l], MISC_DT), _lane_row(c_a_log[l], MISC_DT)
        dsk, ng = jnp.repeat(c_d[l], C_HEAD_DIM)[None, :], c_norm_g[l][None, :]
        yc_p, ssm_p = _ssd_call(p, zero_conv, zero_ssm, cw, cb, dtb, alog, dsk, ng,
                                batch=bp, seq=sp, cs=cs_p, row0=0)
        yc_s, ssm_s = _ssd_call(p, state_c_conv[l], state_c_ssm[l], cw, cb, dtb, alog, dsk, ng,
                                batch=bs, seq=ss, cs=ss, row0=tp)

        x = _outproj(x, (ya_p, yb_p, yc_p), (ya_s, yb_s, yc_s), w_out[l].astype(BF16))
        x = _ffn(x, norm2_g[l][None, :], w_gate[l].astype(BF16), w_up[l].astype(BF16), w_down[l].astype(BF16))

        def state(lo, hi, b, s, ssm):
            av = p[lo:hi, COL_AV:COL_AV + LANES]
            bv = p[lo:hi, COL_BV:COL_BV + B_WIDTH]
            conv = jnp.stack([p[lo + (i + 1) * s - (CONV_WIDTH - 1):lo + (i + 1) * s, COL_XBC:COL_XBC + C_CONV_DIM]
                              for i in range(b)])
            return (ak[lo:hi].reshape(b, s, A_KV_HEADS, HEAD_DIM), av.reshape(b, s, A_KV_HEADS, HEAD_DIM),
                    ik[lo:hi].reshape(b, s, IDX_DIM), bk[lo:hi].reshape(b, s, B_HEADS, HEAD_DIM),
                    bv.reshape(b, s, B_HEADS, HEAD_DIM), ssm, conv)

        p_states.append(state(0, tp, bp, sp, ssm_p))
        s_states.append(state(tp, tp + ts, bs, ss, ssm_s))

    p_out = [jnp.stack(zs) for zs in zip(*p_states)]
    s_out = [jnp.stack(zs) for zs in zip(*s_states)]
    return (x[:tp].reshape(bp, sp, d), x[tp:].reshape(bs, ss, d), *p_out, *s_out)
```
